```python
import jax, jax.numpy as jnp
from jax import lax
import numpy as np

D_MODEL = 1024
BATCH = 8
SEQ = 4096
DEPTH = 4

HEAD_DIM = 64
N_GROUP_HEADS = 4
GROUP_WIDTH = N_GROUP_HEADS * HEAD_DIM
MIX_WIDTH = 4 * GROUP_WIDTH
MLA_Q_RANK = 256
MLA_KV_RANK = 128
MLA_NOPE = 64
MLA_ROPE = 32
MLA_V = 64
SWA_KV_HEADS = 2
SWA_WINDOW = 128
IDX_HEADS = 8
IDX_DIM = 32
DSA_TOPK = 256
FOX_HEADS = 4
D_FF = 4 * D_MODEL
ROPE_THETA = 10000.0
Q_BLOCK = 128
EPS = 1e-6

IN_SPLITS = (
    MLA_Q_RANK, MLA_KV_RANK, MLA_ROPE,
    GROUP_WIDTH, SWA_KV_HEADS * HEAD_DIM, SWA_KV_HEADS * HEAD_DIM,
    GROUP_WIDTH, GROUP_WIDTH, GROUP_WIDTH, IDX_HEADS * IDX_DIM, IDX_DIM, IDX_HEADS,
    GROUP_WIDTH, GROUP_WIDTH, GROUP_WIDTH, FOX_HEADS,
)
IN_WIDTH = sum(IN_SPLITS)

kernel_name = "hybrid_parallel_mla_swa_dsa_fox"


def rms_norm(x, g):
    xf = x.astype(jnp.float32)
    y = xf * lax.rsqrt(jnp.mean(xf * xf, axis=-1, keepdims=True) + EPS)
    return (y * g.astype(jnp.float32)).astype(x.dtype)


def rope(x, pos):
    d = x.shape[-1]
    half = d // 2
    inv_freq = 1.0 / (ROPE_THETA ** (jnp.arange(0, half, dtype=jnp.float32) * 2.0 / d))
    ang = pos[:, None] * inv_freq[None, :]
    cos = jnp.cos(ang)[:, None, :]
    sin = jnp.sin(ang)[:, None, :]
    xf = x.astype(jnp.float32)
    x1, x2 = xf[..., :half], xf[..., half:]
    out = jnp.concatenate([x1 * cos - x2 * sin, x2 * cos + x1 * sin], axis=-1)
    return out.astype(x.dtype)


def split_cols(proj):
    offs, acc = [], 0
    for n in IN_SPLITS[:-1]:
        acc += n
        offs.append(acc)
    return jnp.split(proj, offs, axis=-1)


def causal_block_attention(q, k, v, scale, log_f_cum=None):
    B, S, H, dk = q.shape
    dv = v.shape[-1]
    nb = S // Q_BLOCK
    q_blocks = q.reshape(B, nb, Q_BLOCK, H, dk).swapaxes(0, 1)
    starts = jnp.arange(nb, dtype=jnp.int32) * Q_BLOCK
    key_pos = jnp.arange(S, dtype=jnp.int32)
    if log_f_cum is None:
        xs = (q_blocks, starts)
    else:
        cum_blocks = log_f_cum.reshape(B, nb, Q_BLOCK, H).swapaxes(0, 1)
        cum_keys = log_f_cum.transpose(0, 2, 1)[:, :, None, :]
        xs = (q_blocks, starts, cum_blocks)

    def block(args):
        q_blk, start = args[0], args[1]
        s = jnp.einsum('bqhd,bkhd->bhqk', q_blk, k).astype(jnp.float32) * scale
        if log_f_cum is not None:
            s = s + (args[2].transpose(0, 2, 1)[..., None] - cum_keys)
        qpos = start + jnp.arange(Q_BLOCK, dtype=jnp.int32)
        causal = key_pos[None, :] <= qpos[:, None]
        s = jnp.where(causal, s, -jnp.inf)
        p = jax.nn.softmax(s, axis=-1).astype(v.dtype)
        return jnp.einsum('bhqk,bkhd->bqhd', p, v)

    out = lax.map(block, xs)
    return out.swapaxes(0, 1).reshape(B, S, H, dv)


def sliding_window_attention(q, k, v, sinks, scale):
    B, S, H, d = q.shape
    Hk = k.shape[2]
    G = H // Hk
    W = SWA_WINDOW
    nb = S // W
    qb = q.reshape(B, nb, W, Hk, G, d)

    def with_prev(t):
        tb = t.reshape(B, nb, W, Hk, d)
        prev = jnp.concatenate([jnp.zeros_like(tb[:, :1]), tb[:, :-1]], axis=1)
        return jnp.concatenate([prev, tb], axis=2)

    kb, vb = with_prev(k), with_prev(v)
    s = jnp.einsum('bnqkgd,bnjkd->bnkgqj', qb, kb).astype(jnp.float32) * scale
    blk = jnp.arange(nb, dtype=jnp.int32)[:, None, None]
    qi = jnp.arange(W, dtype=jnp.int32)[None, :, None]
    kj = jnp.arange(2 * W, dtype=jnp.int32)[None, None, :]
    qpos = blk * W + qi
    kpos = blk * W - W + kj
    mask = (kpos >= 0) & (kpos <= qpos) & (qpos - kpos < SWA_WINDOW)
    s = jnp.where(mask[None, :, None, None], s, -jnp.inf)
    sink = jnp.broadcast_to(sinks.astype(jnp.float32).reshape(1, 1, Hk, G, 1, 1),
                            s.shape[:-1] + (1,))
    p = jax.nn.softmax(jnp.concatenate([s, sink], axis=-1), axis=-1)[..., :-1]
    o = jnp.einsum('bnkgqj,bnjkd->bnqkgd', p.astype(v.dtype), vb)
    return o.reshape(B, S, H, d)


def dsa_attention(q, k, v, q_idx, k_idx, w_idx, top_k, scale):
    B, S, H, d = q.shape
    nb = S // Q_BLOCK

    def blocks(t):
        return t.reshape((B, nb, Q_BLOCK) + t.shape[2:]).swapaxes(0, 1)

    starts = jnp.arange(nb, dtype=jnp.int32) * Q_BLOCK
    key_pos = jnp.arange(S, dtype=jnp.int32)

    def block(args):
        q_blk, qi_blk, w_blk, start = args
        qpos = start + jnp.arange(Q_BLOCK, dtype=jnp.int32)
        rel = jax.nn.relu(jnp.einsum('bqhd,bkd->bqhk', qi_blk, k_idx).astype(jnp.float32))
        score = jnp.einsum('bqhk,bqh->bqk', rel, w_blk.astype(jnp.float32))
        score = jnp.where(key_pos[None, None, :] <= qpos[None, :, None], score, -jnp.inf)
        _, idx = lax.top_k(score, top_k)
        k_sel = jax.vmap(lambda kb, ib: kb[ib])(k, idx)
        v_sel = jax.vmap(lambda vb, ib: vb[ib])(v, idx)
        s = jnp.einsum('bqhd,bqkhd->bqhk', q_blk, k_sel).astype(jnp.float32) * scale
        valid = (idx <= qpos[None, :, None])[:, :, None, :]
        s = jnp.where(valid, s, -jnp.inf)
        p = jax.nn.softmax(s, axis=-1).astype(v.dtype)
        return jnp.einsum('bqhk,bqkhd->bqhd', p, v_sel)

    out = lax.map(block, (blocks(q), blocks(q_idx), blocks(w_idx), starts))
    return out.swapaxes(0, 1).reshape(B, S, H, d)


def setup_inputs(seed: int = 0) -> dict:
    key = jax.random.key(seed)
    ks = jax.random.split(key, 16)
    H = N_GROUP_HEADS
    f32 = jnp.float32

    def nrm(k, shape, fan_in):
        return jax.random.normal(k, shape, f32) * (fan_in ** -0.5)

    def gain(k, shape):
        return 1.0 + 0.05 * jax.random.normal(k, shape, f32)

    return {
        "x": jax.random.normal(ks[0], (BATCH, SEQ, D_MODEL), f32),
        "norm1": gain(ks[1], (DEPTH, D_MODEL)),
        "w_in": nrm(ks[2], (DEPTH, D_MODEL, IN_WIDTH), D_MODEL),
        "mla_q_norm": gain(ks[3], (DEPTH, MLA_Q_RANK)),
        "mla_kv_norm": gain(ks[4], (DEPTH, MLA_KV_RANK)),
        "mla_w_uq": nrm(ks[5], (DEPTH, MLA_Q_RANK, H * (MLA_NOPE + MLA_ROPE)), MLA_Q_RANK),
        "mla_w_ukv": nrm(ks[6], (DEPTH, MLA_KV_RANK, H * (MLA_NOPE + MLA_V)), MLA_KV_RANK),
        "swa_sinks": 0.5 * jax.random.normal(ks[7], (DEPTH, H), f32),
        "fox_b_f": 0.1 * jax.random.normal(ks[8], (DEPTH, FOX_HEADS), f32),
        "w_out": nrm(ks[9], (DEPTH, MIX_WIDTH, D_MODEL), MIX_WIDTH),
        "norm2": gain(ks[10], (DEPTH, D_MODEL)),
        "w_up": nrm(ks[11], (DEPTH, D_MODEL, D_FF), D_MODEL),
        "w_down": nrm(ks[12], (DEPTH, D_FF, D_MODEL), D_FF),
        "final_norm": gain(ks[13], (D_MODEL,)),
    }


def reference(x, norm1, w_in, mla_q_norm, mla_kv_norm, mla_w_uq, mla_w_ukv, swa_sinks,
              fox_b_f, w_out, norm2, w_up, w_down, final_norm):
    B, S, _ = x.shape
    H = N_GROUP_HEADS
    Dh = HEAD_DIM
    pos = jnp.arange(S, dtype=jnp.float32)
    top_k = min(DSA_TOPK, S // 4)
    for l in range(DEPTH):
        h = rms_norm(x, norm1[l])
        proj = h @ w_in[l]
        (a_cq, a_ckv, a_kr, b_q, b_k, b_v, c_q, c_k, c_v, c_qi, c_ki, c_w,
         d_q, d_k, d_v, d_f) = split_cols(proj)

        qa = (rms_norm(a_cq, mla_q_norm[l]) @ mla_w_uq[l]).reshape(B, S, H, MLA_NOPE + MLA_ROPE)
        q_a = jnp.concatenate([qa[..., :MLA_NOPE], rope(qa[..., MLA_NOPE:], pos)], axis=-1)
        kv = (rms_norm(a_ckv, mla_kv_norm[l]) @ mla_w_ukv[l]).reshape(B, S, H, MLA_NOPE + MLA_V)
        k_rope = rope(a_kr.reshape(B, S, 1, MLA_ROPE), pos)
        k_a = jnp.concatenate([kv[..., :MLA_NOPE],
                               jnp.broadcast_to(k_rope, (B, S, H, MLA_ROPE))], axis=-1)
        o_a = causal_block_attention(q_a, k_a, kv[..., MLA_NOPE:],
                                     (MLA_NOPE + MLA_ROPE) ** -0.5)

        q_b = rope(b_q.reshape(B, S, H, Dh), pos)
        k_b = rope(b_k.reshape(B, S, SWA_KV_HEADS, Dh), pos)
        v_b = b_v.reshape(B, S, SWA_KV_HEADS, Dh)
        o_b = sliding_window_attention(q_b, k_b, v_b, swa_sinks[l], Dh ** -0.5)

        q_c = rope(c_q.reshape(B, S, H, Dh), pos)
        k_c = rope(c_k.reshape(B, S, H, Dh), pos)
        v_c = c_v.reshape(B, S, H, Dh)
        qi = rope(c_qi.reshape(B, S, IDX_HEADS, IDX_DIM), pos)
        ki = rope(c_ki.reshape(B, S, 1, IDX_DIM), pos)[:, :, 0, :]
        o_c = dsa_attention(q_c, k_c, v_c, qi, ki, c_w, top_k, Dh ** -0.5)

        log_f = jax.nn.log_sigmoid((d_f + fox_b_f[l]).astype(jnp.float32))
        cum = lax.cumsum(log_f, axis=1)
        o_d = causal_block_attention(d_q.reshape(B, S, FOX_HEADS, Dh),
                                     d_k.reshape(B, S, FOX_HEADS, Dh),
                                     d_v.reshape(B, S, FOX_HEADS, Dh),
                                     Dh ** -0.5, log_f_cum=cum)

        mixed = jnp.concatenate([o_a.reshape(B, S, GROUP_WIDTH), o_b.reshape(B, S, GROUP_WIDTH),
                                 o_c.reshape(B, S, GROUP_WIDTH), o_d.reshape(B, S, GROUP_WIDTH)],
                                axis=-1)
        x = x + mixed @ w_out[l]

        h2 = rms_norm(x, norm2[l])
        x = x + jnp.square(jax.nn.relu(h2 @ w_up[l])) @ w_down[l]
    return rms_norm(x, final_norm)
```

```python
import functools
import math

import jax
import jax.numpy as jnp
from jax import lax
from jax.experimental import pallas as pl
from jax.experimental.pallas import tpu as pltpu

F32 = jnp.float32
BF16 = jnp.bfloat16
I32 = jnp.int32

HEAD_DIM = 64
N_HEADS = 4
GROUP_WIDTH = N_HEADS * HEAD_DIM
MLA_Q_RANK = 256
MLA_KV_RANK = 128
MLA_NOPE = 64
MLA_ROPE = 32
SWA_KV_HEADS = 2
SWA_WINDOW = 128
IDX_HEADS = 8
IDX_DIM = 32
DSA_TOPK = 256
ROPE_THETA = 10000.0
EPS = 1e-6

LANES = 128
VMEM_LIMIT = 56 * 1024 * 1024
NEG = -1e30
INT_MIN = -(2 ** 31)

_ROW_GROUPS = (
    ("a_cq", 256), ("a_ckv", 128), ("a_kr", 128),
    ("b_q", 256), ("b_k", 128), ("b_v", 128),
    ("c_q", 256), ("c_k", 256), ("c_qi", 256), ("c_ki", 128),
    ("d_q", 256), ("d_k", 256), ("d_v", 256),
)
_ROW_OFF = {}
_acc = 0
for _n, _w in _ROW_GROUPS:
    _ROW_OFF[_n] = (_acc, _acc + _w)
    _acc += _w
ROW_WIDTH = _acc
T_ROWS = 256 + 16


def _dot(a, b):
    return jnp.dot(a, b, preferred_element_type=F32)


def _dot_nt(a, b):
    return lax.dot_general(a, b, (((1,), (1,)), ((), ())), preferred_element_type=F32)


def _rms(x, g):
    return x * lax.rsqrt(jnp.mean(x * x, axis=-1, keepdims=True) + EPS) * g


def _rope(x, cos, sin, half):
    width = x.shape[1]
    reps = width // LANES
    if reps > 1:
        cos = jnp.concatenate([cos] * reps, axis=1)
        sin = jnp.concatenate([sin] * reps, axis=1)
    lane = lax.broadcasted_iota(I32, x.shape, 1)
    first = (lane % (2 * half)) < half
    rot = jnp.where(first, pltpu.roll(x, width - half, 1), pltpu.roll(x, half, 1))
    return x * cos + rot * sin


def _inproj_kernel(x_ref, g1_ref, w_ref, wt_ref, qn_ref, kvn_ref, wuq_ref, wukv_ref,
                   c64_ref, s64_ref, c32_ref, s32_ref, cm_ref, sm_ref,
                   qa_ref, ka_ref, va_ref, qb_ref, kb_ref, vb_ref,
                   qc_ref, kc_ref, qi_ref, ki_ref, qd_ref, kd_ref, vd_ref,
                   vct_ref, misct_ref):
    h = _rms(x_ref[...], g1_ref[...]).astype(BF16)

    def proj(name):
        lo, hi = _ROW_OFF[name]
        return _dot(h, w_ref[:, lo:hi])

    c64, s64 = c64_ref[...], s64_ref[...]
    c32, s32 = c32_ref[...], s32_ref[...]
    cm, sm = cm_ref[...], sm_ref[...]
    scale_a = (MLA_NOPE + MLA_ROPE) ** -0.5
    scale = HEAD_DIM ** -0.5

    cq = _rms(proj("a_cq"), qn_ref[...]).astype(BF16)
    qa = _rope(_dot(cq, wuq_ref[...]), cm, sm, MLA_ROPE // 2) * scale_a
    qa_ref[...] = qa.astype(BF16)
    ckv = _rms(proj("a_ckv"), kvn_ref[...]).astype(BF16)
    kv = _dot(ckv, wukv_ref[...])
    kr = _rope(proj("a_kr"), cm, sm, MLA_ROPE // 2)
    ka_ref[...] = (kv[:, :4 * LANES] + jnp.concatenate([kr] * 4, axis=1)).astype(BF16)
    va_ref[...] = kv[:, 4 * LANES:].astype(BF16)

    qb_ref[...] = (_rope(proj("b_q"), c64, s64, HEAD_DIM // 2) * scale).astype(BF16)
    kb_ref[...] = _rope(proj("b_k"), c64, s64, HEAD_DIM // 2).astype(BF16)
    vb_ref[...] = proj("b_v").astype(BF16)

    qc_ref[...] = (_rope(proj("c_q"), c64, s64, HEAD_DIM // 2) * scale).astype(BF16)
    kc_ref[...] = _rope(proj("c_k"), c64, s64, HEAD_DIM // 2).astype(BF16)
    qi_ref[...] = _rope(proj("c_qi"), c32, s32, IDX_DIM // 2).astype(BF16)
    ki_ref[...] = _rope(proj("c_ki"), c32, s32, IDX_DIM // 2).astype(BF16)

    qd_ref[...] = (proj("d_q") * scale).astype(BF16)
    kd_ref[...] = proj("d_k").astype(BF16)
    vd_ref[...] = proj("d_v").astype(BF16)

    t = _dot_nt(wt_ref[...], h)
    vct_ref[0] = t[:256].astype(BF16)
    misct_ref[0] = t[256:]


def _inproj(x2d, g1, w_row, w_t, qn, kvn, wuq, wukv, tables, *, batch, seq, tm):
    n = x2d.shape[0]
    d = x2d.shape[1]
    spb = seq // tm
    row = lambda i: (i, 0)
    const = lambda i: (0, 0)
    tab = lambda i: (i % spb, 0)
    tr = lambda i: (i // spb, 0, i % spb)
    once = pl.Buffered(1)

    def out2(width, dtype=BF16):
        return jax.ShapeDtypeStruct((n, width), dtype), pl.BlockSpec((tm, width), row)

    outs = [out2(512), out2(512), out2(256),
            out2(256), out2(128), out2(128),
            out2(256), out2(256), out2(256), out2(128),
            out2(256), out2(256), out2(256)]
    out_shape = [o[0] for o in outs] + [
        jax.ShapeDtypeStruct((batch, 256, seq), BF16),
        jax.ShapeDtypeStruct((batch, 16, seq), F32)]
    out_specs = [o[1] for o in outs] + [
        pl.BlockSpec((1, 256, tm), tr), pl.BlockSpec((1, 16, tm), tr)]
    in_specs = [
        pl.BlockSpec((tm, d), row),
        pl.BlockSpec((1, d), const),
        pl.BlockSpec((d, ROW_WIDTH), const, pipeline_mode=once),
        pl.BlockSpec((T_ROWS, d), const, pipeline_mode=once),
        pl.BlockSpec((1, MLA_Q_RANK), const),
        pl.BlockSpec((1, MLA_KV_RANK), const),
        pl.BlockSpec((MLA_Q_RANK, 512), const, pipeline_mode=once),
        pl.BlockSpec((MLA_KV_RANK, 768), const, pipeline_mode=once),
    ] + [pl.BlockSpec((tm, LANES), tab)] * 6
    return pl.pallas_call(
        _inproj_kernel,
        grid=(n // tm,),
        in_specs=in_specs,
        out_specs=out_specs,
        out_shape=out_shape,
        compiler_params=pltpu.CompilerParams(
            dimension_semantics=("parallel",), vmem_limit_bytes=VMEM_LIMIT),
        name="inproj",
    )(x2d, g1, w_row, w_t, qn, kvn, wuq, wukv, *tables)


def _cumsum_kernel(f_ref, b_ref, o_ref, *, segs):
    x = f_ref[...] + b_ref[...]
    ls = jnp.minimum(x, 0.0) - jnp.log1p(jnp.exp(-jnp.abs(x)))
    rows = x.shape[0]
    r = lax.broadcasted_iota(I32, (LANES, LANES), 0)
    c = lax.broadcasted_iota(I32, (LANES, LANES), 1)
    upper = (r <= c).astype(F32)
    within = jnp.dot(ls, upper, preferred_element_type=F32, precision=lax.Precision.HIGHEST)
    tot = jnp.broadcast_to(within[:, LANES - 1:LANES], (rows, LANES))
    rr = lax.broadcasted_iota(I32, (rows, rows), 0)
    cc = lax.broadcasted_iota(I32, (rows, rows), 1)
    before = ((cc < rr) & (cc // segs == rr // segs)).astype(F32)
    off = jnp.dot(before, tot, preferred_element_type=F32, precision=lax.Precision.HIGHEST)
    o_ref[...] = within + off


def _forget_cumsum(f_t, bias, *, batch, seq):
    segs = seq // LANES
    rows = N_HEADS * segs
    f2 = f_t.reshape(batch * rows, LANES)
    b2 = jnp.broadcast_to(jnp.repeat(bias, segs)[:, None], (rows, LANES))
    out = pl.pallas_call(
        functools.partial(_cumsum_kernel, segs=segs),
        grid=(batch,),
        in_specs=[pl.BlockSpec((rows, LANES), lambda b: (b, 0)),
                  pl.BlockSpec((rows, LANES), lambda b: (0, 0))],
        out_specs=pl.BlockSpec((rows, LANES), lambda b: (b, 0)),
        out_shape=jax.ShapeDtypeStruct((batch * rows, LANES), F32),
        compiler_params=pltpu.CompilerParams(dimension_semantics=("parallel",)),
        name="forget_cumsum",
    )(f2, b2)
    return out.reshape(batch, N_HEADS, seq)


def _flash_kernel(*refs, heads, tq, tk, has_bias):
    if has_bias:
        q_ref, k_ref, v_ref, b_ref, o_ref, m_ref, l_ref, acc_ref = refs
    else:
        q_ref, k_ref, v_ref, o_ref, m_ref, l_ref, acc_ref = refs
        b_ref = None
    i = pl.program_id(1)
    q_start = i * tq
    n_full = q_start // tk
    lane = lax.broadcasted_iota(I32, (tq, LANES), 1)
    row = lax.broadcasted_iota(I32, (tq, tk), 0)
    col = lax.broadcasted_iota(I32, (tq, tk), 1)
    results = []
    for hidx, (ql, qhalf, kl, vl, vhalf) in enumerate(heads):
        q = q_ref[0, :, ql:ql + LANES]
        if qhalf is not None:
            q = jnp.where((lane >= HEAD_DIM) == bool(qhalf), q, jnp.zeros_like(q))
        m_ref[...] = jnp.full(m_ref.shape, NEG, F32)
        l_ref[...] = jnp.zeros(l_ref.shape, F32)
        acc_ref[...] = jnp.zeros(acc_ref.shape, F32)

        def step(j, masked, q=q, kl=kl, vl=vl, hidx=hidx):
            off = pl.multiple_of(j * tk, tk)
            k = k_ref[0, pl.ds(off, tk), kl:kl + LANES]
            s = _dot_nt(q, k)
            if has_bias:
                s = s - b_ref[0, hidx:hidx + 1, pl.ds(off, tk)]
            if masked:
                s = jnp.where(off + col <= q_start + row, s, NEG)
            m_prev = m_ref[...]
            m_new = jnp.maximum(m_prev, jnp.max(s, axis=1, keepdims=True))
            alpha = jnp.exp(m_prev - m_new)
            p = jnp.exp(s - m_new)
            l_ref[...] = alpha * l_ref[...] + jnp.sum(p, axis=1, keepdims=True)
            v = v_ref[0, pl.ds(off, tk), vl:vl + LANES]
            acc_ref[...] = alpha * acc_ref[...] + _dot(p.astype(BF16), v)
            m_ref[...] = m_new

        def loop_body(j, carry):
            step(j, False)
            return carry

        lax.fori_loop(0, n_full, loop_body, 0)
        for d in range(tq // tk):
            step(n_full + d, True)
        results.append((acc_ref[...] / l_ref[...], vl, vhalf))
    for pair in range(len(results) // 2):
        (o0, vl0, h0), (o1, vl1, h1) = results[2 * pair], results[2 * pair + 1]
        assert vl0 == vl1 and h0 == 0 and h1 == 1
        o_ref[0, :, vl0:vl0 + LANES] = jnp.where(lane < HEAD_DIM, o0, o1).astype(o_ref.dtype)


def _flash(q, k, v, bias, heads, *, tq, tk):
    batch, seq, wq = q.shape
    wk = k.shape[2]
    has_bias = bias is not None
    in_specs = [pl.BlockSpec((1, tq, wq), lambda b, i: (b, i, 0)),
                pl.BlockSpec((1, seq, wk), lambda b, i: (b, 0, 0)),
                pl.BlockSpec((1, seq, GROUP_WIDTH), lambda b, i: (b, 0, 0))]
    args = [q, k, v]
    if has_bias:
        in_specs.append(pl.BlockSpec((1, N_HEADS, seq), lambda b, i: (b, 0, 0)))
        args.append(bias)
    return pl.pallas_call(
        functools.partial(_flash_kernel, heads=heads, tq=tq, tk=tk, has_bias=has_bias),
        grid=(batch, seq // tq),
        in_specs=in_specs,
        out_specs=pl.BlockSpec((1, tq, GROUP_WIDTH), lambda b, i: (b, i, 0)),
        out_shape=jax.ShapeDtypeStruct((batch, seq, GROUP_WIDTH), BF16),
        scratch_shapes=[pltpu.VMEM((tq, 1), F32), pltpu.VMEM((tq, 1), F32),
                        pltpu.VMEM((tq, LANES), F32)],
        compiler_params=pltpu.CompilerParams(
            dimension_semantics=("parallel", "arbitrary"), vmem_limit_bytes=VMEM_LIMIT),
        name="flash_bias" if has_bias else "flash",
    )(*args)


def _swa_kernel(sink_ref, q_ref, k_ref, v_ref, o_ref, *, tq, window):
    i = pl.program_id(1)
    q_start = i * tq
    span = tq + window
    start = pl.multiple_of(jnp.maximum(q_start - window, 0), LANES)
    k = k_ref[0, pl.ds(start, span), :]
    v = v_ref[0, pl.ds(start, span), :]
    lane = lax.broadcasted_iota(I32, (tq, LANES), 1)
    qpos = q_start + lax.broadcasted_iota(I32, (tq, span), 0)
    kpos = start + lax.broadcasted_iota(I32, (tq, span), 1)
    valid = (kpos <= qpos) & (qpos - kpos < window)
    for pair in range(2):
        qp = q_ref[0, :, pair * LANES:(pair + 1) * LANES]
        outs = []
        for half in range(2):
            head = pair + 2 * half
            sink = sink_ref[head]
            q = jnp.where((lane >= HEAD_DIM) == bool(half), qp, jnp.zeros_like(qp))
            s = jnp.where(valid, _dot_nt(q, k), NEG)
            m = jnp.maximum(jnp.max(s, axis=1, keepdims=True), sink)
            p = jnp.exp(s - m)
            denom = jnp.sum(p, axis=1, keepdims=True) + jnp.exp(sink - m)
            outs.append(_dot(p.astype(BF16), v) / denom)
        o_ref[0, :, pair * LANES:(pair + 1) * LANES] = jnp.where(
            lane < HEAD_DIM, outs[0], outs[1]).astype(o_ref.dtype)


def _swa(q, k, v, sinks, *, tq):
    batch, seq, _ = q.shape
    return pl.pallas_call(
        functools.partial(_swa_kernel, tq=tq, window=SWA_WINDOW),
        grid=(batch, seq // tq),
        in_specs=[pl.BlockSpec(memory_space=pltpu.SMEM),
                  pl.BlockSpec((1, tq, GROUP_WIDTH), lambda b, i: (b, i, 0)),
                  pl.BlockSpec((1, seq, LANES), lambda b, i: (b, 0, 0)),
                  pl.BlockSpec((1, seq, LANES), lambda b, i: (b, 0, 0))],
        out_specs=pl.BlockSpec((1, tq, GROUP_WIDTH), lambda b, i: (b, i, 0)),
        out_shape=jax.ShapeDtypeStruct((batch, seq, GROUP_WIDTH), BF16),
        compiler_params=pltpu.CompilerParams(
            dimension_semantics=("parallel", "arbitrary"), vmem_limit_bytes=VMEM_LIMIT),
        name="swa",
    )(sinks, q, k, v)


def _dsa_kernel(qi_ref, w_ref, ki_ref, qc_ref, kc_ref, vct_ref, o_ref,
                keys_ref, m_ref, l_ref, acc_ref, *, tq, ck, topk, seq):
    i = pl.program_id(1)
    q_start = i * tq
    n_chunks = (q_start + tq) // ck
    n_full = q_start // ck
    lane = lax.broadcasted_iota(I32, (tq, LANES), 1)
    kpos_l = lax.broadcasted_iota(I32, (ck, tq), 0)
    qpos = q_start + lax.broadcasted_iota(I32, (ck, tq), 1)
    qpos_row = q_start + lax.broadcasted_iota(I32, (1, tq), 1)
    msb = jnp.int32(INT_MIN)

    qm = []
    for h in range(IDX_HEADS):
        g, r = divmod(h, LANES // IDX_DIM)
        qg = qi_ref[0, :, g * LANES:(g + 1) * LANES]
        qm.append(jnp.where(lane // IDX_DIM == r, qg, jnp.zeros_like(qg)))
    w = w_ref[0]

    def score_chunk(c, masked):
        off = pl.multiple_of(c * ck, ck)
        ki = ki_ref[0, pl.ds(off, ck), :]
        acc = jnp.zeros((ck, tq), F32)
        for h in range(IDX_HEADS):
            acc = acc + jnp.maximum(_dot_nt(ki, qm[h]), 0.0) * w[h:h + 1, :]
        bits = lax.bitcast_convert_type(acc, I32)
        key = bits ^ ((bits >> 31) & jnp.int32(0x7FFFFFFF))
        if masked:
            key = jnp.where(off + kpos_l <= qpos, key, msb)
        keys_ref[pl.ds(off, ck), :] = key

    def score_body(c, carry):
        score_chunk(c, False)
        return carry

    lax.fori_loop(0, n_full, score_body, 0)
    for d in range(tq // ck):
        score_chunk(n_full + d, True)

    def count(pred):
        def body(c, part):
            off = pl.multiple_of(c * ck, ck)
            hit = jnp.where(pred(keys_ref[pl.ds(off, ck), :], off), 1, 0)
            return part + jnp.sum(hit.reshape(ck // 8, 8, tq), axis=0)
        part = lax.fori_loop(0, n_chunks, body, jnp.zeros((8, tq), I32))
        return jnp.sum(part, axis=0, keepdims=True)

    need = qpos_row >= topk

    def unsettled(cnt):
        return jnp.max(jnp.where(need & (cnt != topk), 1, 0)) > 0

    def sel_cond(state):
        bit, _, cnt = state
        return (bit >= 0) & unsettled(cnt)

    def sel_body(state):
        bit, t_u, cnt = state
        cand_u = t_u | (jnp.int32(1) << bit)
        cand_s = cand_u ^ msb
        c = count(lambda kk, off: kk >= cand_s)
        take = c >= topk
        return bit - 1, jnp.where(take, cand_u, t_u), jnp.where(take, c, cnt)

    init = (jnp.int32(31), jnp.zeros((1, tq), I32), jnp.full((1, tq), seq + 1, I32))
    _, t_u, cnt = lax.while_loop(sel_cond, sel_body, init)
    t_s = t_u ^ msb
    tie = need & (cnt > topk)

    @pl.when(jnp.max(jnp.where(tie, 1, 0)) > 0)
    def _():
        n_gt = count(lambda kk, off: kk > t_s)
        want = topk - n_gt
        j_u = jnp.zeros((1, tq), I32)
        for bit in reversed(range(max(1, (seq - 1).bit_length()))):
            cand = j_u | (1 << bit)
            c = count(lambda kk, off: (kk == t_s) & (off + kpos_l < cand))
            j_u = jnp.where(c < want, cand, j_u)

        def demote(c, carry):
            off = pl.multiple_of(c * ck, ck)
            kk = keys_ref[pl.ds(off, ck), :]
            drop = tie & (kk == t_s) & (off + kpos_l > j_u)
            keys_ref[pl.ds(off, ck), :] = jnp.where(drop, msb, kk)
            return carry

        lax.fori_loop(0, n_chunks, demote, 0)

    t_fin = jnp.maximum(t_s, msb + 1)
    q_heads = []
    for h in range(N_HEADS):
        pair, half = divmod(h, 2)
        qp = qc_ref[0, :, pair * LANES:(pair + 1) * LANES]
        q_heads.append(jnp.where((lane >= HEAD_DIM) == bool(half), qp, jnp.zeros_like(qp)))
    m_ref[...] = jnp.full(m_ref.shape, NEG, F32)
    l_ref[...] = jnp.zeros(l_ref.shape, F32)
    acc_ref[...] = jnp.zeros(acc_ref.shape, F32)

    def attn_body(c, carry):
        off = pl.multiple_of(c * ck, ck)
        sel = keys_ref[pl.ds(off, ck), :] >= t_fin
        for h in range(N_HEADS):
            pair = h // 2
            k = kc_ref[0, pl.ds(off, ck), pair * LANES:(pair + 1) * LANES]
            s = jnp.where(sel, _dot_nt(k, q_heads[h]), NEG)
            m_prev = m_ref[h]
            m_new = jnp.maximum(m_prev, jnp.max(s, axis=0, keepdims=True))
            alpha = jnp.exp(m_prev - m_new)
            p = jnp.exp(s - m_new)
            l_ref[h] = alpha * l_ref[h] + jnp.sum(p, axis=0, keepdims=True)
            vt = vct_ref[0, pair * LANES:(pair + 1) * LANES, pl.ds(off, ck)]
            acc_ref[h] = alpha * acc_ref[h] + _dot(vt, p.astype(BF16))
            m_ref[h] = m_new
        return carry

    lax.fori_loop(0, n_chunks, attn_body, 0)
    sub = lax.broadcasted_iota(I32, (LANES, tq), 0)
    for pair in range(2):
        o0 = acc_ref[2 * pair] / l_ref[2 * pair]
        o1 = acc_ref[2 * pair + 1] / l_ref[2 * pair + 1]
        o_t = jnp.where(sub < HEAD_DIM, o0, o1)
        o_ref[0, :, pair * LANES:(pair + 1) * LANES] = o_t.T.astype(o_ref.dtype)


def _dsa(qi, misc_t, ki, qc, kc, vc_t, *, tq, ck, topk):
    batch, seq, _ = qc.shape
    return pl.pallas_call(
        functools.partial(_dsa_kernel, tq=tq, ck=ck, topk=topk, seq=seq),
        grid=(batch, seq // tq),
        in_specs=[pl.BlockSpec((1, tq, IDX_HEADS * IDX_DIM), lambda b, i: (b, i, 0)),
                  pl.BlockSpec((1, 16, tq), lambda b, i: (b, 0, i)),
                  pl.BlockSpec((1, seq, LANES), lambda b, i: (b, 0, 0)),
                  pl.BlockSpec((1, tq, GROUP_WIDTH), lambda b, i: (b, i, 0)),
                  pl.BlockSpec((1, seq, GROUP_WIDTH), lambda b, i: (b, 0, 0)),
                  pl.BlockSpec((1, GROUP_WIDTH, seq), lambda b, i: (b, 0, 0))],
        out_specs=pl.BlockSpec((1, tq, GROUP_WIDTH), lambda b, i: (b, i, 0)),
        out_shape=jax.ShapeDtypeStruct((batch, seq, GROUP_WIDTH), BF16),
        scratch_shapes=[pltpu.VMEM((seq, tq), I32),
                        pltpu.VMEM((N_HEADS, 1, tq), F32),
                        pltpu.VMEM((N_HEADS, 1, tq), F32),
                        pltpu.VMEM((N_HEADS, LANES, tq), F32)],
        compiler_params=pltpu.CompilerParams(
            dimension_semantics=("parallel", "arbitrary"), vmem_limit_bytes=VMEM_LIMIT),
        name="dsa",
    )(qi, misc_t, ki, qc, kc, vc_t)


def _mlp_kernel(oa_ref, ob_ref, oc_ref, od_ref, x_ref, wo_ref, g2_ref, wup_ref, wdn_ref,
                gf_ref, out_ref, *, ff_chunk, final):
    gw = GROUP_WIDTH
    mix = None
    for g, o_ref in enumerate((oa_ref, ob_ref, oc_ref, od_ref)):
        y = _dot(o_ref[...], wo_ref[g * gw:(g + 1) * gw, :])
        mix = y if mix is None else mix + y
    x1 = x_ref[...] + mix
    h2 = _rms(x1, g2_ref[...]).astype(BF16)
    ffn = None
    for c in range(wup_ref.shape[1] // ff_chunk):
        u = _dot(h2, wup_ref[:, c * ff_chunk:(c + 1) * ff_chunk])
        a = jnp.square(jnp.maximum(u, 0.0)).astype(BF16)
        y = _dot(a, wdn_ref[c * ff_chunk:(c + 1) * ff_chunk, :])
        ffn = y if ffn is None else ffn + y
    acc = x1 + ffn
    if final:
        acc = _rms(acc, gf_ref[...])
    out_ref[...] = acc


def _mlp(oa, ob, oc, od, x2d, wo, g2, wup, wdn, gf, *, tm, final):
    n, d = x2d.shape
    dff = wup.shape[1]
    row = lambda i: (i, 0)
    const = lambda i: (0, 0)
    once = pl.Buffered(1)
    o_spec = pl.BlockSpec((tm, GROUP_WIDTH), row)
    return pl.pallas_call(
        functools.partial(_mlp_kernel, ff_chunk=1024, final=final),
        grid=(n // tm,),
        in_specs=[o_spec, o_spec, o_spec, o_spec,
                  pl.BlockSpec((tm, d), row),
                  pl.BlockSpec((d, d), const, pipeline_mode=once),
                  pl.BlockSpec((1, d), const),
                  pl.BlockSpec((d, dff), const, pipeline_mode=once),
                  pl.BlockSpec((dff, d), const, pipeline_mode=once),
                  pl.BlockSpec((1, d), const)],
        out_specs=pl.BlockSpec((tm, d), row),
        out_shape=jax.ShapeDtypeStruct((n, d), F32),
        compiler_params=pltpu.CompilerParams(
            dimension_semantics=("parallel",), vmem_limit_bytes=VMEM_LIMIT),
        name="outproj_mlp",
    )(oa, ob, oc, od, x2d, wo, g2, wup, wdn, gf)


def _rope_tables(seq):
    pos = jnp.arange(seq, dtype=F32)[:, None]
    lane = jnp.arange(LANES)

    def table(dim, active):
        half = dim // 2
        inv_freq = 1.0 / (ROPE_THETA ** (jnp.arange(0, half, dtype=F32) * 2.0 / dim))
        ang = pos * inv_freq[None, :][:, lane % half]
        sign = jnp.where((lane % dim) < half, -1.0, 1.0)
        cos = jnp.where(active, jnp.cos(ang), 1.0)
        sin = jnp.where(active, jnp.sin(ang) * sign, 0.0)
        return cos.astype(F32), sin.astype(F32)

    c64, s64 = table(HEAD_DIM, jnp.ones((LANES,), bool))
    c32, s32 = table(IDX_DIM, jnp.ones((LANES,), bool))
    cm, sm = table(MLA_ROPE, (lane >= MLA_NOPE) & (lane < MLA_NOPE + MLA_ROPE))
    return c64, s64, c32, s32, cm, sm


def _pack_weights(w_in, mla_w_uq, mla_w_ukv, w_out):
    depth, d, _ = w_in.shape
    splits = (MLA_Q_RANK, MLA_KV_RANK, MLA_ROPE,
              GROUP_WIDTH, SWA_KV_HEADS * HEAD_DIM, SWA_KV_HEADS * HEAD_DIM,
              GROUP_WIDTH, GROUP_WIDTH, GROUP_WIDTH, IDX_HEADS * IDX_DIM, IDX_DIM, IDX_HEADS,
              GROUP_WIDTH, GROUP_WIDTH, GROUP_WIDTH, N_HEADS)
    offs = [0]
    for s in splits:
        offs.append(offs[-1] + s)
    (a_cq, a_ckv, a_kr, b_q, b_k, b_v, c_q, c_k, c_v, c_qi, c_ki, c_w,
     d_q, d_k, d_v, d_f) = [w_in[:, :, offs[j]:offs[j + 1]] for j in range(len(splits))]
    zeros = lambda n: jnp.zeros((depth, d, n), w_in.dtype)
    swap = jnp.array([0, 2, 1, 3])
    a_kr = jnp.concatenate([zeros(MLA_NOPE), a_kr, zeros(LANES - MLA_NOPE - MLA_ROPE)], -1)
    b_q = b_q.reshape(depth, d, N_HEADS, HEAD_DIM)[:, :, swap].reshape(depth, d, GROUP_WIDTH)
    c_ki = jnp.concatenate([c_ki] * (LANES // IDX_DIM), -1)
    w_row = jnp.concatenate(
        [a_cq, a_ckv, a_kr, b_q, b_k, b_v, c_q, c_k, c_qi, c_ki, d_q, d_k, d_v], -1).astype(BF16)
    w_t = jnp.concatenate([c_v, c_w, d_f, zeros(16 - IDX_HEADS - N_HEADS)], -1)
    w_t = jnp.swapaxes(w_t, 1, 2).astype(BF16)

    pad = LANES - MLA_NOPE - MLA_ROPE
    wuq = mla_w_uq.reshape(depth, MLA_Q_RANK, N_HEADS, MLA_NOPE + MLA_ROPE)
    wuq = jnp.pad(wuq, ((0, 0), (0, 0), (0, 0), (0, pad))).reshape(depth, MLA_Q_RANK, N_HEADS * LANES)
    wukv = mla_w_ukv.reshape(depth, MLA_KV_RANK, N_HEADS, MLA_NOPE + HEAD_DIM)
    wk = jnp.pad(wukv[..., :MLA_NOPE], ((0, 0), (0, 0), (0, 0), (0, LANES - MLA_NOPE)))
    wk = wk.reshape(depth, MLA_KV_RANK, N_HEADS * LANES)
    wv = wukv[..., MLA_NOPE:].reshape(depth, MLA_KV_RANK, GROUP_WIDTH)
    wukv = jnp.concatenate([wk, wv], -1)

    wo_b = w_out[:, GROUP_WIDTH:2 * GROUP_WIDTH].reshape(depth, N_HEADS, HEAD_DIM, -1)[:, swap]
    wo = jnp.concatenate([w_out[:, :GROUP_WIDTH], wo_b.reshape(depth, GROUP_WIDTH, -1),
                          w_out[:, 2 * GROUP_WIDTH:]], 1)
    return w_row, w_t, wuq.astype(BF16), wukv.astype(BF16), wo.astype(BF16)


_MLA_HEADS = tuple((h * LANES, None, h * LANES, (h // 2) * LANES, h % 2) for h in range(N_HEADS))
_FOX_HEADS = tuple(((h // 2) * LANES, h % 2, (h // 2) * LANES, (h // 2) * LANES, h % 2)
                   for h in range(N_HEADS))


def kernel(x, norm1, w_in, mla_q_norm, mla_kv_norm, mla_w_uq, mla_w_ukv, swa_sinks, fox_b_f,
           w_out, norm2, w_up, w_down, final_norm):
    batch, seq, d = x.shape
    depth = w_in.shape[0]
    n = batch * seq
    top_k = min(DSA_TOPK, seq // 4)
    tm = min(512, seq)
    tq_flash = min(512, seq)
    tk_flash = min(256, seq)
    tq_swa = min(256, seq)
    tq_dsa = min(256, seq)
    ck_dsa = min(256, seq)

    tables = _rope_tables(seq)
    w_row, w_t, wuq, wukv, wo = _pack_weights(w_in, mla_w_uq, mla_w_ukv, w_out)
    wup = w_up.astype(BF16)
    wdn = w_down.astype(BF16)

    x2d = x.reshape(n, d)
    for l in range(depth):
        (qa, ka, va, qb, kb, vb, qc, kc, qi, ki, qd, kd, vd, vc_t, misc_t) = _inproj(
            x2d, norm1[l][None], w_row[l], w_t[l], mla_q_norm[l][None], mla_kv_norm[l][None],
            wuq[l], wukv[l], tables, batch=batch, seq=seq, tm=tm)
        r3 = lambda a: a.reshape(batch, seq, a.shape[-1])
        cum = _forget_cumsum(misc_t[:, IDX_HEADS:IDX_HEADS + N_HEADS], fox_b_f[l],
                             batch=batch, seq=seq)
        o_a = _flash(r3(qa), r3(ka), r3(va), None, _MLA_HEADS, tq=tq_flash, tk=tk_flash)
        o_b = _swa(r3(qb), r3(kb), r3(vb), swa_sinks[l], tq=tq_swa)
        o_c = _dsa(r3(qi), misc_t, r3(ki), r3(qc), r3(kc), vc_t, tq=tq_dsa, ck=ck_dsa, topk=top_k)
        o_d = _flash(r3(qd), r3(kd), r3(vd), cum, _FOX_HEADS, tq=tq_flash, tk=tk_flash)
        x2d = _mlp(o_a.reshape(n, -1), o_b.reshape(n, -1), o_c.reshape(n, -1), o_d.reshape(n, -1),
                   x2d, wo[l], norm2[l][None], wup[l], wdn[l], final_norm[None],
                   tm=tm, final=(l == depth - 1))
    return x2d.reshape(batch, seq, d)
```

```python
import functools
import math

import jax
import jax.numpy as jnp
from jax import lax
from jax.experimental import pallas as pl
from jax.experimental.pallas import tpu as pltpu

F32 = jnp.float32
BF16 = jnp.bfloat16
I32 = jnp.int32

HEAD_DIM = 64
N_HEADS = 4
GROUP_WIDTH = N_HEADS * HEAD_DIM
MLA_Q_RANK = 256
MLA_KV_RANK = 128
MLA_NOPE = 64
MLA_ROPE = 32
SWA_KV_HEADS = 2
SWA_WINDOW = 128
IDX_HEADS = 8
IDX_DIM = 32
DSA_TOPK = 256
ROPE_THETA = 10000.0
EPS = 1e-6

LANES = 128
SUBLANES = 8
VMEM_LIMIT = 56 * 1024 * 1024
NEG = -1e30
INT_MIN = -(2 ** 31)
LOG2E = math.log2(math.e)

_ROW_GROUPS = (
    ("a_cq", 256), ("a_ckv", 128), ("a_kr", 128),
    ("b_q", 256), ("b_k", 128), ("b_v", 128),
    ("c_q", 256), ("c_k", 256), ("c_qi", 256), ("c_ki", 128),
    ("d_q", 256), ("d_k", 256),
)
_ROW_OFF = {}
_acc = 0
for _n, _w in _ROW_GROUPS:
    _ROW_OFF[_n] = (_acc, _acc + _w)
    _acc += _w
ROW_WIDTH = _acc
MISC_ROWS = 16
T_ROWS = 2 * GROUP_WIDTH + MISC_ROWS


def _dot(a, b):
    return jnp.dot(a, b, preferred_element_type=F32)


def _dot_nt(a, b):
    return lax.dot_general(a, b, (((1,), (1,)), ((), ())), preferred_element_type=F32)


def _rms(x, g):
    return x * lax.rsqrt(jnp.mean(x * x, axis=-1, keepdims=True) + EPS) * g


def _rope(x, cos, sin, half):
    width = x.shape[1]
    reps = width // LANES
    if reps > 1:
        cos = jnp.concatenate([cos] * reps, axis=1)
        sin = jnp.concatenate([sin] * reps, axis=1)
    lane = lax.broadcasted_iota(I32, x.shape, 1)
    first = (lane % (2 * half)) < half
    rot = jnp.where(first, pltpu.roll(x, width - half, 1), pltpu.roll(x, half, 1))
    return x * cos + rot * sin


def _half_mask(q, half):
    lane = lax.broadcasted_iota(I32, q.shape, 1)
    return jnp.where((lane >= HEAD_DIM) == bool(half), q, jnp.zeros_like(q))


def _softmax_step(s, m, l8, mask):
    ck, tq = s.shape
    if mask is not None:
        s = jnp.where(mask, s, NEG)
    m_new = jnp.maximum(m, jnp.max(s, axis=0, keepdims=True))
    alpha = jnp.exp2(m - m_new)
    p = jnp.exp2(s - m_new)
    l8 = alpha * l8 + jnp.sum(p.reshape(ck // SUBLANES, SUBLANES, tq), axis=0)
    return p.astype(BF16), m_new, l8, alpha


def _inproj_kernel(x_ref, g1_ref, w_ref, wt_ref, qn_ref, kvn_ref, wuq_ref, wuk_ref, wuvt_ref,
                   c64_ref, s64_ref, c32_ref, s32_ref, cm_ref, sm_ref,
                   qa_ref, ka_ref, qb_ref, kb_ref, vb_ref,
                   qc_ref, kc_ref, qi_ref, ki_ref, qd_ref, kd_ref,
                   vat_ref, vct_ref, vdt_ref, misct_ref):
    h = _rms(x_ref[...], g1_ref[...]).astype(BF16)

    def proj(name):
        lo, hi = _ROW_OFF[name]
        return _dot(h, w_ref[:, lo:hi])

    c64, s64 = c64_ref[...], s64_ref[...]
    c32, s32 = c32_ref[...], s32_ref[...]
    cm, sm = cm_ref[...], sm_ref[...]
    scale_a = (MLA_NOPE + MLA_ROPE) ** -0.5 * LOG2E
    scale = HEAD_DIM ** -0.5 * LOG2E

    cq = _rms(proj("a_cq"), qn_ref[...]).astype(BF16)
    qa = _rope(_dot(cq, wuq_ref[...]), cm, sm, MLA_ROPE // 2) * scale_a
    qa_ref[...] = qa.astype(BF16)
    ckv = _rms(proj("a_ckv"), kvn_ref[...]).astype(BF16)
    kr = _rope(proj("a_kr"), cm, sm, MLA_ROPE // 2)
    ka_ref[...] = (_dot(ckv, wuk_ref[...]) + jnp.concatenate([kr] * N_HEADS, axis=1)).astype(BF16)
    vat_ref[0] = _dot_nt(wuvt_ref[...], ckv).astype(BF16)

    qb_ref[...] = (_rope(proj("b_q"), c64, s64, HEAD_DIM // 2) * scale).astype(BF16)
    kb_ref[...] = _rope(proj("b_k"), c64, s64, HEAD_DIM // 2).astype(BF16)
    vb_ref[...] = proj("b_v").astype(BF16)

    qc_ref[...] = (_rope(proj("c_q"), c64, s64, HEAD_DIM // 2) * scale).astype(BF16)
    kc_ref[...] = _rope(proj("c_k"), c64, s64, HEAD_DIM // 2).astype(BF16)
    qi_ref[...] = _rope(proj("c_qi"), c32, s32, IDX_DIM // 2).astype(BF16)
    ki_ref[...] = _rope(proj("c_ki"), c32, s32, IDX_DIM // 2).astype(BF16)

    qd_ref[...] = (proj("d_q") * scale).astype(BF16)
    kd_ref[...] = proj("d_k").astype(BF16)

    t = _dot_nt(wt_ref[...], h)
    vct_ref[0] = t[:GROUP_WIDTH].astype(BF16)
    vdt_ref[0] = t[GROUP_WIDTH:2 * GROUP_WIDTH].astype(BF16)
    misct_ref[0] = t[2 * GROUP_WIDTH:]


def _inproj(x2d, g1, w_row, w_t, qn, kvn, wuq, wuk, wuvt, tables, *, batch, seq, tm):
    n = x2d.shape[0]
    d = x2d.shape[1]
    spb = seq // tm
    row = lambda i: (i, 0)
    const = lambda i: (0, 0)
    tab = lambda i: (i % spb, 0)
    tr = lambda i: (i // spb, 0, i % spb)
    once = pl.Buffered(1)

    def out2(width):
        return jax.ShapeDtypeStruct((n, width), BF16), pl.BlockSpec((tm, width), row)

    def out_t(rows, dtype):
        return jax.ShapeDtypeStruct((batch, rows, seq), dtype), pl.BlockSpec((1, rows, tm), tr)

    outs = [out2(512), out2(512),
            out2(256), out2(128), out2(128),
            out2(256), out2(256), out2(256), out2(128),
            out2(256), out2(256),
            out_t(GROUP_WIDTH, BF16), out_t(GROUP_WIDTH, BF16), out_t(GROUP_WIDTH, BF16),
            out_t(MISC_ROWS, F32)]
    in_specs = [
        pl.BlockSpec((tm, d), row),
        pl.BlockSpec((1, d), const),
        pl.BlockSpec((d, ROW_WIDTH), const, pipeline_mode=once),
        pl.BlockSpec((T_ROWS, d), const, pipeline_mode=once),
        pl.BlockSpec((1, MLA_Q_RANK), const),
        pl.BlockSpec((1, MLA_KV_RANK), const),
        pl.BlockSpec((MLA_Q_RANK, N_HEADS * LANES), const, pipeline_mode=once),
        pl.BlockSpec((MLA_KV_RANK, N_HEADS * LANES), const, pipeline_mode=once),
        pl.BlockSpec((GROUP_WIDTH, MLA_KV_RANK), const, pipeline_mode=once),
    ] + [pl.BlockSpec((tm, LANES), tab)] * 6
    return pl.pallas_call(
        _inproj_kernel,
        grid=(n // tm,),
        in_specs=in_specs,
        out_specs=[o[1] for o in outs],
        out_shape=[o[0] for o in outs],
        compiler_params=pltpu.CompilerParams(
            dimension_semantics=("parallel",), vmem_limit_bytes=VMEM_LIMIT),
        name="inproj",
    )(x2d, g1, w_row, w_t, qn, kvn, wuq, wuk, wuvt, *tables)


def _cumsum_kernel(f_ref, b_ref, o_ref, *, segs):
    x = f_ref[...] + b_ref[...]
    ls = jnp.minimum(x, 0.0) - jnp.log1p(jnp.exp(-jnp.abs(x)))
    rows = x.shape[0]
    r = lax.broadcasted_iota(I32, (LANES, LANES), 0)
    c = lax.broadcasted_iota(I32, (LANES, LANES), 1)
    upper = (r <= c).astype(F32)
    within = jnp.dot(ls, upper, preferred_element_type=F32, precision=lax.Precision.HIGHEST)
    tot = jnp.broadcast_to(within[:, LANES - 1:LANES], (rows, LANES))
    rr = lax.broadcasted_iota(I32, (rows, rows), 0)
    cc = lax.broadcasted_iota(I32, (rows, rows), 1)
    before = ((cc < rr) & (cc // segs == rr // segs)).astype(F32)
    off = jnp.dot(before, tot, preferred_element_type=F32, precision=lax.Precision.HIGHEST)
    o_ref[...] = (within + off) * LOG2E


def _forget_cumsum(f_t, bias, *, batch, seq):
    segs = seq // LANES
    rows = N_HEADS * segs
    f2 = f_t.reshape(batch * rows, LANES)
    b2 = jnp.broadcast_to(jnp.repeat(bias, segs)[:, None], (rows, LANES))
    out = pl.pallas_call(
        functools.partial(_cumsum_kernel, segs=segs),
        grid=(batch,),
        in_specs=[pl.BlockSpec((rows, LANES), lambda b: (b, 0)),
                  pl.BlockSpec((rows, LANES), lambda b: (0, 0))],
        out_specs=pl.BlockSpec((rows, LANES), lambda b: (b, 0)),
        out_shape=jax.ShapeDtypeStruct((batch * rows, LANES), F32),
        compiler_params=pltpu.CompilerParams(dimension_semantics=("parallel",)),
        name="forget_cumsum",
    )(f2, b2)
    return out.reshape(batch, N_HEADS, seq)


def _flash_kernel(*refs, heads, tq, ck, has_bias):
    if has_bias:
        q_ref, k_ref, vt_ref, b_ref, o_ref, acc_ref = refs
    else:
        q_ref, k_ref, vt_ref, o_ref, acc_ref = refs
        b_ref = None
    nh = len(heads)
    i = pl.program_id(1)
    q_start = i * tq
    n_full = q_start // ck
    kpos_l = lax.broadcasted_iota(I32, (ck, tq), 0)
    qpos = q_start + lax.broadcasted_iota(I32, (ck, tq), 1)
    qs = []
    for ql, qhalf, _, _ in heads:
        q = q_ref[0, :, ql:ql + LANES]
        qs.append(q if qhalf is None else _half_mask(q, qhalf))
    acc_ref[...] = jnp.zeros(acc_ref.shape, F32)

    def chunk(c, carry, masked):
        ms, ls = carry
        off = pl.multiple_of(c * ck, ck)
        mask = (off + kpos_l <= qpos) if masked else None
        ss = []
        for h, (_, _, kl, _) in enumerate(heads):
            s = _dot_nt(k_ref[0, pl.ds(off, ck), kl:kl + LANES], qs[h])
            if has_bias:
                s = s - b_ref[0, pl.ds(off, ck), h:h + 1]
            ss.append(s)
        new_ms, new_ls = [], []
        for h, (_, _, _, pair) in enumerate(heads):
            p, m_new, l8, alpha = _softmax_step(ss[h], ms[h], ls[h], mask)
            vt = vt_ref[0, pair * LANES:(pair + 1) * LANES, pl.ds(off, ck)]
            acc_ref[h] = alpha * acc_ref[h] + _dot(vt, p)
            new_ms.append(m_new)
            new_ls.append(l8)
        return tuple(new_ms), tuple(new_ls)

    init = (tuple(jnp.full((1, tq), NEG, F32) for _ in range(nh)),
            tuple(jnp.zeros((SUBLANES, tq), F32) for _ in range(nh)))
    carry = lax.fori_loop(0, n_full, lambda c, cr: chunk(c, cr, False), init)
    for d in range(tq // ck):
        carry = chunk(n_full + d, carry, True)
    _, ls = carry
    sub = lax.broadcasted_iota(I32, (LANES, tq), 0)
    for pair in range(nh // 2):
        o0 = acc_ref[2 * pair] / jnp.sum(ls[2 * pair], axis=0, keepdims=True)
        o1 = acc_ref[2 * pair + 1] / jnp.sum(ls[2 * pair + 1], axis=0, keepdims=True)
        o_t = jnp.where(sub < HEAD_DIM, o0, o1)
        o_ref[0, :, pair * LANES:(pair + 1) * LANES] = o_t.T.astype(o_ref.dtype)


def _flash(q, k, v_t, bias, heads, *, tq, ck):
    batch, seq, wq = q.shape
    wk = k.shape[2]
    has_bias = bias is not None
    in_specs = [pl.BlockSpec((1, tq, wq), lambda b, i: (b, i, 0)),
                pl.BlockSpec((1, seq, wk), lambda b, i: (b, 0, 0)),
                pl.BlockSpec((1, GROUP_WIDTH, seq), lambda b, i: (b, 0, 0))]
    args = [q, k, v_t]
    if has_bias:
        in_specs.append(pl.BlockSpec((1, seq, N_HEADS), lambda b, i: (b, 0, 0)))
        args.append(bias)
    return pl.pallas_call(
        functools.partial(_flash_kernel, heads=heads, tq=tq, ck=ck, has_bias=has_bias),
        grid=(batch, seq // tq),
        in_specs=in_specs,
        out_specs=pl.BlockSpec((1, tq, GROUP_WIDTH), lambda b, i: (b, i, 0)),
        out_shape=jax.ShapeDtypeStruct((batch, seq, GROUP_WIDTH), BF16),
        scratch_shapes=[pltpu.VMEM((len(heads), LANES, tq), F32)],
        compiler_params=pltpu.CompilerParams(
            dimension_semantics=("parallel", "arbitrary"), vmem_limit_bytes=VMEM_LIMIT),
        name="flash_bias" if has_bias else "flash",
    )(*args)


def _swa_kernel(sink_ref, q_ref, k_ref, v_ref, o_ref, *, tq, window):
    i = pl.program_id(1)
    q_start = i * tq
    span = tq + window
    start = pl.multiple_of(jnp.maximum(q_start - window, 0), LANES)
    k = k_ref[0, pl.ds(start, span), :]
    v = v_ref[0, pl.ds(start, span), :]
    lane = lax.broadcasted_iota(I32, (tq, LANES), 1)
    qpos = q_start + lax.broadcasted_iota(I32, (tq, span), 0)
    kpos = start + lax.broadcasted_iota(I32, (tq, span), 1)
    valid = (kpos <= qpos) & (qpos - kpos < window)
    for pair in range(2):
        qp = q_ref[0, :, pair * LANES:(pair + 1) * LANES]
        outs = []
        for half in range(2):
            head = pair + 2 * half
            sink = sink_ref[head] * LOG2E
            s = jnp.where(valid, _dot_nt(_half_mask(qp, half), k), NEG)
            m = jnp.maximum(jnp.max(s, axis=1, keepdims=True), sink)
            p = jnp.exp2(s - m)
            denom = jnp.sum(p, axis=1, keepdims=True) + jnp.exp2(sink - m)
            outs.append(_dot(p.astype(BF16), v) / denom)
        o_ref[0, :, pair * LANES:(pair + 1) * LANES] = jnp.where(
            lane < HEAD_DIM, outs[0], outs[1]).astype(o_ref.dtype)


def _swa(q, k, v, sinks, *, tq):
    batch, seq, _ = q.shape
    return pl.pallas_call(
        functools.partial(_swa_kernel, tq=tq, window=SWA_WINDOW),
        grid=(batch, seq // tq),
        in_specs=[pl.BlockSpec(memory_space=pltpu.SMEM),
                  pl.BlockSpec((1, tq, GROUP_WIDTH), lambda b, i: (b, i, 0)),
                  pl.BlockSpec((1, seq, LANES), lambda b, i: (b, 0, 0)),
                  pl.BlockSpec((1, seq, LANES), lambda b, i: (b, 0, 0))],
        out_specs=pl.BlockSpec((1, tq, GROUP_WIDTH), lambda b, i: (b, i, 0)),
        out_shape=jax.ShapeDtypeStruct((batch, seq, GROUP_WIDTH), BF16),
        compiler_params=pltpu.CompilerParams(
            dimension_semantics=("parallel", "arbitrary"), vmem_limit_bytes=VMEM_LIMIT),
        name="swa",
    )(sinks, q, k, v)


def _dsa_kernel(qi_ref, w_ref, ki_ref, qc_ref, kc_ref, vct_ref, o_ref,
                keys_ref, acc_ref, *, tq, ck, topk, seq):
    i = pl.program_id(1)
    q_start = i * tq
    n_chunks = (q_start + tq) // ck
    n_full = q_start // ck
    lane = lax.broadcasted_iota(I32, (tq, LANES), 1)
    kpos_l = lax.broadcasted_iota(I32, (ck, tq), 0)
    qpos = q_start + lax.broadcasted_iota(I32, (ck, tq), 1)
    qpos_row = q_start + lax.broadcasted_iota(I32, (1, tq), 1)
    msb = jnp.int32(INT_MIN)

    qm = []
    for h in range(IDX_HEADS):
        g, r = divmod(h, LANES // IDX_DIM)
        qg = qi_ref[0, :, g * LANES:(g + 1) * LANES]
        qm.append(jnp.where(lane // IDX_DIM == r, qg, jnp.zeros_like(qg)))
    w = w_ref[0]

    def score_chunk(c, masked):
        off = pl.multiple_of(c * ck, ck)
        ki = ki_ref[0, pl.ds(off, ck), :]
        acc = jnp.zeros((ck, tq), F32)
        for h in range(IDX_HEADS):
            acc = acc + jnp.maximum(_dot_nt(ki, qm[h]), 0.0) * w[h:h + 1, :]
        bits = lax.bitcast_convert_type(acc, I32)
        key = bits ^ ((bits >> 31) & jnp.int32(0x7FFFFFFF))
        if masked:
            key = jnp.where(off + kpos_l <= qpos, key, msb)
        keys_ref[pl.ds(off, ck), :] = key

    def score_body(c, carry):
        score_chunk(c, False)
        return carry

    lax.fori_loop(0, n_full, score_body, 0)
    for d in range(tq // ck):
        score_chunk(n_full + d, True)

    def count(pred):
        def body(c, part):
            off = pl.multiple_of(c * ck, ck)
            hit = jnp.where(pred(keys_ref[pl.ds(off, ck), :], off), 1, 0)
            return part + jnp.sum(hit.reshape(ck // SUBLANES, SUBLANES, tq), axis=0)
        part = lax.fori_loop(0, n_chunks, body, jnp.zeros((SUBLANES, tq), I32))
        return jnp.sum(part, axis=0, keepdims=True)

    need = qpos_row >= topk

    def unsettled(cnt):
        return jnp.max(jnp.where(need & (cnt != topk), 1, 0)) > 0

    def sel_cond(state):
        bit, _, cnt = state
        return (bit >= 0) & unsettled(cnt)

    def sel_body(state):
        bit, t_u, cnt = state
        cand_u = t_u | (jnp.int32(1) << bit)
        cand_s = cand_u ^ msb
        c = count(lambda kk, off: kk >= cand_s)
        take = c >= topk
        return bit - 1, jnp.where(take, cand_u, t_u), jnp.where(take, c, cnt)

    init = (jnp.int32(31), jnp.zeros((1, tq), I32), jnp.full((1, tq), seq + 1, I32))
    _, t_u, cnt = lax.while_loop(sel_cond, sel_body, init)
    t_s = t_u ^ msb
    tie = need & (cnt > topk)

    @pl.when(jnp.max(jnp.where(tie, 1, 0)) > 0)
    def _():
        n_gt = count(lambda kk, off: kk > t_s)
        want = topk - n_gt
        j_u = jnp.zeros((1, tq), I32)
        for bit in reversed(range(max(1, (seq - 1).bit_length()))):
            cand = j_u | (1 << bit)
            c = count(lambda kk, off: (kk == t_s) & (off + kpos_l < cand))
            j_u = jnp.where(c < want, cand, j_u)

        def demote(c, carry):
            off = pl.multiple_of(c * ck, ck)
            kk = keys_ref[pl.ds(off, ck), :]
            drop = tie & (kk == t_s) & (off + kpos_l > j_u)
            keys_ref[pl.ds(off, ck), :] = jnp.where(drop, msb, kk)
            return carry

        lax.fori_loop(0, n_chunks, demote, 0)

    t_fin = jnp.maximum(t_s, msb + 1)
    qs = []
    for h in range(N_HEADS):
        pair, half = divmod(h, 2)
        qs.append(_half_mask(qc_ref[0, :, pair * LANES:(pair + 1) * LANES], half))
    acc_ref[...] = jnp.zeros(acc_ref.shape, F32)

    def attn_body(c, carry):
        ms, ls = carry
        off = pl.multiple_of(c * ck, ck)
        sel = keys_ref[pl.ds(off, ck), :] >= t_fin
        ss = [_dot_nt(kc_ref[0, pl.ds(off, ck), (h // 2) * LANES:(h // 2 + 1) * LANES], qs[h])
              for h in range(N_HEADS)]
        new_ms, new_ls = [], []
        for h in range(N_HEADS):
            pair = h // 2
            p, m_new, l8, alpha = _softmax_step(ss[h], ms[h], ls[h], sel)
            vt = vct_ref[0, pair * LANES:(pair + 1) * LANES, pl.ds(off, ck)]
            acc_ref[h] = alpha * acc_ref[h] + _dot(vt, p)
            new_ms.append(m_new)
            new_ls.append(l8)
        return tuple(new_ms), tuple(new_ls)

    init = (tuple(jnp.full((1, tq), NEG, F32) for _ in range(N_HEADS)),
            tuple(jnp.zeros((SUBLANES, tq), F32) for _ in range(N_HEADS)))
    _, ls = lax.fori_loop(0, n_chunks, attn_body, init)
    sub = lax.broadcasted_iota(I32, (LANES, tq), 0)
    for pair in range(2):
        o0 = acc_ref[2 * pair] / jnp.sum(ls[2 * pair], axis=0, keepdims=True)
        o1 = acc_ref[2 * pair + 1] / jnp.sum(ls[2 * pair + 1], axis=0, keepdims=True)
        o_t = jnp.where(sub < HEAD_DIM, o0, o1)
        o_ref[0, :, pair * LANES:(pair + 1) * LANES] = o_t.T.astype(o_ref.dtype)


def _dsa(qi, misc_t, ki, qc, kc, vc_t, *, tq, ck, topk):
    batch, seq, _ = qc.shape
    return pl.pallas_call(
        functools.partial(_dsa_kernel, tq=tq, ck=ck, topk=topk, seq=seq),
        grid=(batch, seq // tq),
        in_specs=[pl.BlockSpec((1, tq, IDX_HEADS * IDX_DIM), lambda b, i: (b, i, 0)),
                  pl.BlockSpec((1, MISC_ROWS, tq), lambda b, i: (b, 0, i)),
                  pl.BlockSpec((1, seq, LANES), lambda b, i: (b, 0, 0)),
                  pl.BlockSpec((1, tq, GROUP_WIDTH), lambda b, i: (b, i, 0)),
                  pl.BlockSpec((1, seq, GROUP_WIDTH), lambda b, i: (b, 0, 0)),
                  pl.BlockSpec((1, GROUP_WIDTH, seq), lambda b, i: (b, 0, 0))],
        out_specs=pl.BlockSpec((1, tq, GROUP_WIDTH), lambda b, i: (b, i, 0)),
        out_shape=jax.ShapeDtypeStruct((batch, seq, GROUP_WIDTH), BF16),
        scratch_shapes=[pltpu.VMEM((seq, tq), I32),
                        pltpu.VMEM((N_HEADS, LANES, tq), F32)],
        compiler_params=pltpu.CompilerParams(
            dimension_semantics=("parallel", "arbitrary"), vmem_limit_bytes=VMEM_LIMIT),
        name="dsa",
    )(qi, misc_t, ki, qc, kc, vc_t)


def _mlp_kernel(oa_ref, ob_ref, oc_ref, od_ref, x_ref, wo_ref, g2_ref, wup_ref, wdn_ref,
                gf_ref, out_ref, *, ff_chunk, final):
    gw = GROUP_WIDTH
    mix = None
    for g, o_ref in enumerate((oa_ref, ob_ref, oc_ref, od_ref)):
        y = _dot(o_ref[...], wo_ref[g * gw:(g + 1) * gw, :])
        mix = y if mix is None else mix + y
    x1 = x_ref[...] + mix
    h2 = _rms(x1, g2_ref[...]).astype(BF16)
    ffn = None
    for c in range(wup_ref.shape[1] // ff_chunk):
        u = _dot(h2, wup_ref[:, c * ff_chunk:(c + 1) * ff_chunk])
        a = jnp.square(jnp.maximum(u, 0.0)).astype(BF16)
        y = _dot(a, wdn_ref[c * ff_chunk:(c + 1) * ff_chunk, :])
        ffn = y if ffn is None else ffn + y
    acc = x1 + ffn
    if final:
        acc = _rms(acc, gf_ref[...])
    out_ref[...] = acc


def _mlp(oa, ob, oc, od, x2d, wo, g2, wup, wdn, gf, *, tm, final):
    n, d = x2d.shape
    dff = wup.shape[1]
    row = lambda i: (i, 0)
    const = lambda i: (0, 0)
    once = pl.Buffered(1)
    o_spec = pl.BlockSpec((tm, GROUP_WIDTH), row)
    return pl.pallas_call(
        functools.partial(_mlp_kernel, ff_chunk=1024, final=final),
        grid=(n // tm,),
        in_specs=[o_spec, o_spec, o_spec, o_spec,
                  pl.BlockSpec((tm, d), row),
                  pl.BlockSpec((d, d), const, pipeline_mode=once),
                  pl.BlockSpec((1, d), const),
                  pl.BlockSpec((d, dff), const, pipeline_mode=once),
                  pl.BlockSpec((dff, d), const, pipeline_mode=once),
                  pl.BlockSpec((1, d), const)],
        out_specs=pl.BlockSpec((tm, d), row),
        out_shape=jax.ShapeDtypeStruct((n, d), F32),
        compiler_params=pltpu.CompilerParams(
            dimension_semantics=("parallel",), vmem_limit_bytes=VMEM_LIMIT),
        name="outproj_mlp",
    )(oa, ob, oc, od, x2d, wo, g2, wup, wdn, gf)


def _rope_tables(seq):
    pos = jnp.arange(seq, dtype=F32)[:, None]
    lane = jnp.arange(LANES)

    def table(dim, active):
        half = dim // 2
        inv_freq = 1.0 / (ROPE_THETA ** (jnp.arange(0, half, dtype=F32) * 2.0 / dim))
        ang = pos * inv_freq[None, :][:, lane % half]
        sign = jnp.where((lane % dim) < half, -1.0, 1.0)
        cos = jnp.where(active, jnp.cos(ang), 1.0)
        sin = jnp.where(active, jnp.sin(ang) * sign, 0.0)
        return cos.astype(F32), sin.astype(F32)

    c64, s64 = table(HEAD_DIM, jnp.ones((LANES,), bool))
    c32, s32 = table(IDX_DIM, jnp.ones((LANES,), bool))
    cm, sm = table(MLA_ROPE, (lane >= MLA_NOPE) & (lane < MLA_NOPE + MLA_ROPE))
    return c64, s64, c32, s32, cm, sm


def _pack_weights(w_in, mla_w_uq, mla_w_ukv, w_out):
    depth, d, _ = w_in.shape
    splits = (MLA_Q_RANK, MLA_KV_RANK, MLA_ROPE,
              GROUP_WIDTH, SWA_KV_HEADS * HEAD_DIM, SWA_KV_HEADS * HEAD_DIM,
              GROUP_WIDTH, GROUP_WIDTH, GROUP_WIDTH, IDX_HEADS * IDX_DIM, IDX_DIM, IDX_HEADS,
              GROUP_WIDTH, GROUP_WIDTH, GROUP_WIDTH, N_HEADS)
    offs = [0]
    for s in splits:
        offs.append(offs[-1] + s)
    (a_cq, a_ckv, a_kr, b_q, b_k, b_v, c_q, c_k, c_v, c_qi, c_ki, c_w,
     d_q, d_k, d_v, d_f) = [w_in[:, :, offs[j]:offs[j + 1]] for j in range(len(splits))]
    zeros = lambda n: jnp.zeros((depth, d, n), w_in.dtype)
    swap = jnp.array([0, 2, 1, 3])
    a_kr = jnp.concatenate([zeros(MLA_NOPE), a_kr, zeros(LANES - MLA_NOPE - MLA_ROPE)], -1)
    b_q = b_q.reshape(depth, d, N_HEADS, HEAD_DIM)[:, :, swap].reshape(depth, d, GROUP_WIDTH)
    c_ki = jnp.concatenate([c_ki] * (LANES // IDX_DIM), -1)
    w_row = jnp.concatenate(
        [a_cq, a_ckv, a_kr, b_q, b_k, b_v, c_q, c_k, c_qi, c_ki, d_q, d_k], -1).astype(BF16)
    w_t = jnp.concatenate([c_v, d_v, c_w, d_f, zeros(MISC_ROWS - IDX_HEADS - N_HEADS)], -1)
    w_t = jnp.swapaxes(w_t, 1, 2).astype(BF16)

    pad = LANES - MLA_NOPE - MLA_ROPE
    wuq = mla_w_uq.reshape(depth, MLA_Q_RANK, N_HEADS, MLA_NOPE + MLA_ROPE)
    wuq = jnp.pad(wuq, ((0, 0), (0, 0), (0, 0), (0, pad))).reshape(depth, MLA_Q_RANK, N_HEADS * LANES)
    wukv = mla_w_ukv.reshape(depth, MLA_KV_RANK, N_HEADS, MLA_NOPE + HEAD_DIM)
    wuk = jnp.pad(wukv[..., :MLA_NOPE], ((0, 0), (0, 0), (0, 0), (0, LANES - MLA_NOPE)))
    wuk = wuk.reshape(depth, MLA_KV_RANK, N_HEADS * LANES)
    wuvt = jnp.swapaxes(wukv[..., MLA_NOPE:].reshape(depth, MLA_KV_RANK, GROUP_WIDTH), 1, 2)

    wo_b = w_out[:, GROUP_WIDTH:2 * GROUP_WIDTH].reshape(depth, N_HEADS, HEAD_DIM, -1)[:, swap]
    wo = jnp.concatenate([w_out[:, :GROUP_WIDTH], wo_b.reshape(depth, GROUP_WIDTH, -1),
                          w_out[:, 2 * GROUP_WIDTH:]], 1)
    return w_row, w_t, wuq.astype(BF16), wuk.astype(BF16), wuvt.astype(BF16), wo.astype(BF16)


_MLA_HEADS = tuple((h * LANES, None, h * LANES, h // 2) for h in range(N_HEADS))
_FOX_HEADS = tuple(((h // 2) * LANES, h % 2, (h // 2) * LANES, h // 2) for h in range(N_HEADS))


def kernel(x, norm1, w_in, mla_q_norm, mla_kv_norm, mla_w_uq, mla_w_ukv, swa_sinks, fox_b_f,
           w_out, norm2, w_up, w_down, final_norm):
    batch, seq, d = x.shape
    depth = w_in.shape[0]
    n = batch * seq
    top_k = min(DSA_TOPK, seq // 4)
    tm = min(512, seq)
    tq_flash = min(256, seq)
    ck_flash = min(256, seq)
    tq_swa = min(256, seq)
    tq_dsa = min(256, seq)
    ck_dsa = min(256, seq)

    tables = _rope_tables(seq)
    w_row, w_t, wuq, wuk, wuvt, wo = _pack_weights(w_in, mla_w_uq, mla_w_ukv, w_out)
    wup = w_up.astype(BF16)
    wdn = w_down.astype(BF16)

    x2d = x.reshape(n, d)
    for l in range(depth):
        (qa, ka, qb, kb, vb, qc, kc, qi, ki, qd, kd, va_t, vc_t, vd_t, misc_t) = _inproj(
            x2d, norm1[l][None], w_row[l], w_t[l], mla_q_norm[l][None], mla_kv_norm[l][None],
            wuq[l], wuk[l], wuvt[l], tables, batch=batch, seq=seq, tm=tm)
        r3 = lambda a: a.reshape(batch, seq, a.shape[-1])
        cum = _forget_cumsum(misc_t[:, IDX_HEADS:IDX_HEADS + N_HEADS], fox_b_f[l],
                             batch=batch, seq=seq)
        cum_col = jnp.swapaxes(cum, 1, 2)
        o_a = _flash(r3(qa), r3(ka), va_t, None, _MLA_HEADS, tq=tq_flash, ck=ck_flash)
        o_b = _swa(r3(qb), r3(kb), r3(vb), swa_sinks[l], tq=tq_swa)
        o_c = _dsa(r3(qi), misc_t, r3(ki), r3(qc), r3(kc), vc_t, tq=tq_dsa, ck=ck_dsa, topk=top_k)
        o_d = _flash(r3(qd), r3(kd), vd_t, cum_col, _FOX_HEADS, tq=tq_flash, ck=ck_flash)
        x2d = _mlp(o_a.reshape(n, -1), o_b.reshape(n, -1), o_c.reshape(n, -1), o_d.reshape(n, -1),
                   x2d, wo[l], norm2[l][None], wup[l], wdn[l], final_norm[None],
                   tm=tm, final=(l == depth - 1))
    return x2d.reshape(batch, seq, d)
```

```python
import functools
import math

import jax
import jax.numpy as jnp
from jax import lax
from jax.experimental import pallas as pl
from jax.experimental.pallas import tpu as pltpu

F32 = jnp.float32
BF16 = jnp.bfloat16
I32 = jnp.int32

HEAD_DIM = 64
N_HEADS = 4
GROUP_WIDTH = N_HEADS * HEAD_DIM
MLA_Q_RANK = 256
MLA_KV_RANK = 128
MLA_NOPE = 64
MLA_ROPE = 32
SWA_KV_HEADS = 2
SWA_WINDOW = 128
IDX_HEADS = 8
IDX_DIM = 32
DSA_TOPK = 256
ROPE_THETA = 10000.0
EPS = 1e-6

LANES = 128
SUBLANES = 8
VMEM_LIMIT = 56 * 1024 * 1024
NEG = -1e30
INT_MIN = -(2 ** 31)
LOG2E = math.log2(math.e)
BISECT_STEPS = 18

_ROW_GROUPS = (
    ("a_cq", 256), ("a_ckv", 128), ("a_kr", 128),
    ("b_q", 256), ("b_k", 128), ("b_v", 128),
    ("c_q", 256), ("c_k", 256), ("c_qi", 256), ("c_ki", 128),
    ("d_q", 256), ("d_k", 256),
)
_ROW_OFF = {}
_acc = 0
for _n, _w in _ROW_GROUPS:
    _ROW_OFF[_n] = (_acc, _acc + _w)
    _acc += _w
ROW_WIDTH = _acc
MISC_ROWS = 16
T_ROWS = 2 * GROUP_WIDTH + MISC_ROWS


def _dot(a, b):
    return jnp.dot(a, b, preferred_element_type=F32)


def _dot_nt(a, b):
    return lax.dot_general(a, b, (((1,), (1,)), ((), ())), preferred_element_type=F32)


def _rms(x, g):
    return x * lax.rsqrt(jnp.mean(x * x, axis=-1, keepdims=True) + EPS) * g


def _rope(x, cos, sin, half):
    width = x.shape[1]
    reps = width // LANES
    if reps > 1:
        cos = jnp.concatenate([cos] * reps, axis=1)
        sin = jnp.concatenate([sin] * reps, axis=1)
    lane = lax.broadcasted_iota(I32, x.shape, 1)
    first = (lane % (2 * half)) < half
    rot = jnp.where(first, pltpu.roll(x, width - half, 1), pltpu.roll(x, half, 1))
    return x * cos + rot * sin


def _half_mask(q, half):
    lane = lax.broadcasted_iota(I32, q.shape, 1)
    return jnp.where((lane >= HEAD_DIM) == bool(half), q, jnp.zeros_like(q))


def _softmax_step(s, m, l8, mask):
    ck, tq = s.shape
    if mask is not None:
        s = jnp.where(mask, s, NEG)
    m_new = jnp.maximum(m, jnp.max(s, axis=0, keepdims=True))
    alpha = jnp.exp2(m - m_new)
    p = jnp.exp2(s - m_new)
    l8 = alpha * l8 + jnp.sum(p.reshape(ck // SUBLANES, SUBLANES, tq), axis=0)
    return p.astype(BF16), m_new, l8, alpha


def _inproj_kernel(x_ref, g1_ref, w_ref, wt_ref, qn_ref, kvn_ref, wuq_ref, wuk_ref, wuvt_ref,
                   c64_ref, s64_ref, c32_ref, s32_ref, cm_ref, sm_ref,
                   qa_ref, ka_ref, qb_ref, kb_ref, vb_ref,
                   qc_ref, kc_ref, qi_ref, ki_ref, qd_ref, kd_ref,
                   vat_ref, vct_ref, vdt_ref, misct_ref):
    h = _rms(x_ref[...], g1_ref[...]).astype(BF16)

    def proj(name):
        lo, hi = _ROW_OFF[name]
        return _dot(h, w_ref[:, lo:hi])

    c64, s64 = c64_ref[...], s64_ref[...]
    c32, s32 = c32_ref[...], s32_ref[...]
    cm, sm = cm_ref[...], sm_ref[...]
    scale_a = (MLA_NOPE + MLA_ROPE) ** -0.5 * LOG2E
    scale = HEAD_DIM ** -0.5 * LOG2E

    cq = _rms(proj("a_cq"), qn_ref[...]).astype(BF16)
    qa = _rope(_dot(cq, wuq_ref[...]), cm, sm, MLA_ROPE // 2) * scale_a
    qa_ref[...] = qa.astype(BF16)
    ckv = _rms(proj("a_ckv"), kvn_ref[...]).astype(BF16)
    kr = _rope(proj("a_kr"), cm, sm, MLA_ROPE // 2)
    ka_ref[...] = (_dot(ckv, wuk_ref[...]) + jnp.concatenate([kr] * N_HEADS, axis=1)).astype(BF16)
    vat_ref[0] = _dot_nt(wuvt_ref[...], ckv).astype(BF16)

    qb_ref[...] = (_rope(proj("b_q"), c64, s64, HEAD_DIM // 2) * scale).astype(BF16)
    kb_ref[...] = _rope(proj("b_k"), c64, s64, HEAD_DIM // 2).astype(BF16)
    vb_ref[...] = proj("b_v").astype(BF16)

    qc_ref[...] = (_rope(proj("c_q"), c64, s64, HEAD_DIM // 2) * scale).astype(BF16)
    kc_ref[...] = _rope(proj("c_k"), c64, s64, HEAD_DIM // 2).astype(BF16)
    qi_ref[...] = _rope(proj("c_qi"), c32, s32, IDX_DIM // 2).astype(BF16)
    ki_ref[...] = _rope(proj("c_ki"), c32, s32, IDX_DIM // 2).astype(BF16)

    qd_ref[...] = (proj("d_q") * scale).astype(BF16)
    kd_ref[...] = proj("d_k").astype(BF16)

    t = _dot_nt(wt_ref[...], h)
    vct_ref[0] = t[:GROUP_WIDTH].astype(BF16)
    vdt_ref[0] = t[GROUP_WIDTH:2 * GROUP_WIDTH].astype(BF16)
    misct_ref[0] = t[2 * GROUP_WIDTH:]


def _inproj(x2d, g1, w_row, w_t, qn, kvn, wuq, wuk, wuvt, tables, *, batch, seq, tm):
    n = x2d.shape[0]
    d = x2d.shape[1]
    spb = seq // tm
    row = lambda i: (i, 0)
    const = lambda i: (0, 0)
    tab = lambda i: (i % spb, 0)
    tr = lambda i: (i // spb, 0, i % spb)
    once = pl.Buffered(1)

    def out2(width):
        return jax.ShapeDtypeStruct((n, width), BF16), pl.BlockSpec((tm, width), row)

    def out_t(rows, dtype):
        return jax.ShapeDtypeStruct((batch, rows, seq), dtype), pl.BlockSpec((1, rows, tm), tr)

    outs = [out2(512), out2(512),
            out2(256), out2(128), out2(128),
            out2(256), out2(256), out2(256), out2(128),
            out2(256), out2(256),
            out_t(GROUP_WIDTH, BF16), out_t(GROUP_WIDTH, BF16), out_t(GROUP_WIDTH, BF16),
            out_t(MISC_ROWS, F32)]
    in_specs = [
        pl.BlockSpec((tm, d), row),
        pl.BlockSpec((1, d), const),
        pl.BlockSpec((d, ROW_WIDTH), const, pipeline_mode=once),
        pl.BlockSpec((T_ROWS, d), const, pipeline_mode=once),
        pl.BlockSpec((1, MLA_Q_RANK), const),
        pl.BlockSpec((1, MLA_KV_RANK), const),
        pl.BlockSpec((MLA_Q_RANK, N_HEADS * LANES), const, pipeline_mode=once),
        pl.BlockSpec((MLA_KV_RANK, N_HEADS * LANES), const, pipeline_mode=once),
        pl.BlockSpec((GROUP_WIDTH, MLA_KV_RANK), const, pipeline_mode=once),
    ] + [pl.BlockSpec((tm, LANES), tab)] * 6
    return pl.pallas_call(
        _inproj_kernel,
        grid=(n // tm,),
        in_specs=in_specs,
        out_specs=[o[1] for o in outs],
        out_shape=[o[0] for o in outs],
        compiler_params=pltpu.CompilerParams(
            dimension_semantics=("parallel",), vmem_limit_bytes=VMEM_LIMIT),
        name="inproj",
    )(x2d, g1, w_row, w_t, qn, kvn, wuq, wuk, wuvt, *tables)


def _cumsum_kernel(f_ref, b_ref, o_ref, *, segs):
    x = f_ref[...] + b_ref[...]
    ls = jnp.minimum(x, 0.0) - jnp.log1p(jnp.exp(-jnp.abs(x)))
    rows = x.shape[0]
    r = lax.broadcasted_iota(I32, (LANES, LANES), 0)
    c = lax.broadcasted_iota(I32, (LANES, LANES), 1)
    upper = (r <= c).astype(F32)
    within = jnp.dot(ls, upper, preferred_element_type=F32, precision=lax.Precision.HIGHEST)
    tot = jnp.broadcast_to(within[:, LANES - 1:LANES], (rows, LANES))
    rr = lax.broadcasted_iota(I32, (rows, rows), 0)
    cc = lax.broadcasted_iota(I32, (rows, rows), 1)
    before = ((cc < rr) & (cc // segs == rr // segs)).astype(F32)
    off = jnp.dot(before, tot, preferred_element_type=F32, precision=lax.Precision.HIGHEST)
    o_ref[...] = (within + off) * LOG2E


def _forget_cumsum(f_t, bias, *, batch, seq):
    segs = seq // LANES
    rows = N_HEADS * segs
    f2 = f_t.reshape(batch * rows, LANES)
    b2 = jnp.broadcast_to(jnp.repeat(bias, segs)[:, None], (rows, LANES))
    out = pl.pallas_call(
        functools.partial(_cumsum_kernel, segs=segs),
        grid=(batch,),
        in_specs=[pl.BlockSpec((rows, LANES), lambda b: (b, 0)),
                  pl.BlockSpec((rows, LANES), lambda b: (0, 0))],
        out_specs=pl.BlockSpec((rows, LANES), lambda b: (b, 0)),
        out_shape=jax.ShapeDtypeStruct((batch * rows, LANES), F32),
        compiler_params=pltpu.CompilerParams(dimension_semantics=("parallel",)),
        name="forget_cumsum",
    )(f2, b2)
    return out.reshape(batch, N_HEADS, seq)


def _flash_kernel(*refs, heads, tq, ck, has_bias):
    if has_bias:
        q_ref, k_ref, vt_ref, b_ref, o_ref, acc_ref = refs
    else:
        q_ref, k_ref, vt_ref, o_ref, acc_ref = refs
        b_ref = None
    nh = len(heads)
    i = pl.program_id(1)
    q_start = i * tq
    n_full = q_start // ck
    kpos_l = lax.broadcasted_iota(I32, (ck, tq), 0)
    qpos = q_start + lax.broadcasted_iota(I32, (ck, tq), 1)
    qs = []
    for ql, qhalf, _, _ in heads:
        q = q_ref[0, :, ql:ql + LANES]
        qs.append(q if qhalf is None else _half_mask(q, qhalf))
    acc_ref[...] = jnp.zeros(acc_ref.shape, F32)

    def chunk(c, carry, masked):
        ms, ls = carry
        off = pl.multiple_of(c * ck, ck)
        mask = (off + kpos_l <= qpos) if masked else None
        ss = []
        for h, (_, _, kl, _) in enumerate(heads):
            s = _dot_nt(k_ref[0, pl.ds(off, ck), kl:kl + LANES], qs[h])
            if has_bias:
                s = s - b_ref[0, pl.ds(off, ck), h:h + 1]
            ss.append(s)
        new_ms, new_ls = [], []
        for h, (_, _, _, pair) in enumerate(heads):
            p, m_new, l8, alpha = _softmax_step(ss[h], ms[h], ls[h], mask)
            vt = vt_ref[0, pair * LANES:(pair + 1) * LANES, pl.ds(off, ck)]
            acc_ref[h] = alpha * acc_ref[h] + _dot(vt, p)
            new_ms.append(m_new)
            new_ls.append(l8)
        return tuple(new_ms), tuple(new_ls)

    init = (tuple(jnp.full((1, tq), NEG, F32) for _ in range(nh)),
            tuple(jnp.zeros((SUBLANES, tq), F32) for _ in range(nh)))
    carry = lax.fori_loop(0, n_full, lambda c, cr: chunk(c, cr, False), init)
    for d in range(tq // ck):
        carry = chunk(n_full + d, carry, True)
    _, ls = carry
    sub = lax.broadcasted_iota(I32, (LANES, tq), 0)
    for pair in range(nh // 2):
        o0 = acc_ref[2 * pair] / jnp.sum(ls[2 * pair], axis=0, keepdims=True)
        o1 = acc_ref[2 * pair + 1] / jnp.sum(ls[2 * pair + 1], axis=0, keepdims=True)
        o_t = jnp.where(sub < HEAD_DIM, o0, o1)
        o_ref[0, :, pair * LANES:(pair + 1) * LANES] = o_t.T.astype(o_ref.dtype)


def _flash(q, k, v_t, bias, heads, *, tq, ck):
    batch, seq, wq = q.shape
    wk = k.shape[2]
    has_bias = bias is not None
    in_specs = [pl.BlockSpec((1, tq, wq), lambda b, i: (b, i, 0)),
                pl.BlockSpec((1, seq, wk), lambda b, i: (b, 0, 0)),
                pl.BlockSpec((1, GROUP_WIDTH, seq), lambda b, i: (b, 0, 0))]
    args = [q, k, v_t]
    if has_bias:
        in_specs.append(pl.BlockSpec((1, seq, N_HEADS), lambda b, i: (b, 0, 0)))
        args.append(bias)
    return pl.pallas_call(
        functools.partial(_flash_kernel, heads=heads, tq=tq, ck=ck, has_bias=has_bias),
        grid=(batch, seq // tq),
        in_specs=in_specs,
        out_specs=pl.BlockSpec((1, tq, GROUP_WIDTH), lambda b, i: (b, i, 0)),
        out_shape=jax.ShapeDtypeStruct((batch, seq, GROUP_WIDTH), BF16),
        scratch_shapes=[pltpu.VMEM((len(heads), LANES, tq), F32)],
        compiler_params=pltpu.CompilerParams(
            dimension_semantics=("parallel", "arbitrary"), vmem_limit_bytes=VMEM_LIMIT),
        name="flash_bias" if has_bias else "flash",
    )(*args)


def _swa_kernel(sink_ref, q_ref, k_ref, v_ref, o_ref, *, tq, window):
    i = pl.program_id(1)
    q_start = i * tq
    span = tq + window
    start = pl.multiple_of(jnp.maximum(q_start - window, 0), LANES)
    k = k_ref[0, pl.ds(start, span), :]
    v = v_ref[0, pl.ds(start, span), :]
    lane = lax.broadcasted_iota(I32, (tq, LANES), 1)
    qpos = q_start + lax.broadcasted_iota(I32, (tq, span), 0)
    kpos = start + lax.broadcasted_iota(I32, (tq, span), 1)
    valid = (kpos <= qpos) & (qpos - kpos < window)
    for pair in range(2):
        qp = q_ref[0, :, pair * LANES:(pair + 1) * LANES]
        outs = []
        for half in range(2):
            head = pair + 2 * half
            sink = sink_ref[head] * LOG2E
            s = jnp.where(valid, _dot_nt(_half_mask(qp, half), k), NEG)
            m = jnp.maximum(jnp.max(s, axis=1, keepdims=True), sink)
            p = jnp.exp2(s - m)
            denom = jnp.sum(p, axis=1, keepdims=True) + jnp.exp2(sink - m)
            outs.append(_dot(p.astype(BF16), v) / denom)
        o_ref[0, :, pair * LANES:(pair + 1) * LANES] = jnp.where(
            lane < HEAD_DIM, outs[0], outs[1]).astype(o_ref.dtype)


def _swa(q, k, v, sinks, *, tq):
    batch, seq, _ = q.shape
    return pl.pallas_call(
        functools.partial(_swa_kernel, tq=tq, window=SWA_WINDOW),
        grid=(batch, seq // tq),
        in_specs=[pl.BlockSpec(memory_space=pltpu.SMEM),
                  pl.BlockSpec((1, tq, GROUP_WIDTH), lambda b, i: (b, i, 0)),
                  pl.BlockSpec((1, seq, LANES), lambda b, i: (b, 0, 0)),
                  pl.BlockSpec((1, seq, LANES), lambda b, i: (b, 0, 0))],
        out_specs=pl.BlockSpec((1, tq, GROUP_WIDTH), lambda b, i: (b, i, 0)),
        out_shape=jax.ShapeDtypeStruct((batch, seq, GROUP_WIDTH), BF16),
        compiler_params=pltpu.CompilerParams(
            dimension_semantics=("parallel", "arbitrary"), vmem_limit_bytes=VMEM_LIMIT),
        name="swa",
    )(sinks, q, k, v)


def _dsa_kernel(qi_ref, w_ref, ki_ref, qc_ref, kc_ref, vct_ref, o_ref,
                keys_ref, acc_ref, *, tq, ck, topk, seq):
    i = pl.program_id(1)
    q_start = i * tq
    n_chunks = (q_start + tq) // ck
    n_full = q_start // ck
    lane = lax.broadcasted_iota(I32, (tq, LANES), 1)
    kpos_l = lax.broadcasted_iota(I32, (ck, tq), 0)
    qpos = q_start + lax.broadcasted_iota(I32, (ck, tq), 1)
    qpos_row = q_start + lax.broadcasted_iota(I32, (1, tq), 1)
    msb = jnp.int32(INT_MIN)

    qm = []
    for h in range(IDX_HEADS):
        g, r = divmod(h, LANES // IDX_DIM)
        qg = qi_ref[0, :, g * LANES:(g + 1) * LANES]
        qm.append(jnp.where(lane // IDX_DIM == r, qg, jnp.zeros_like(qg)))
    w = w_ref[0]

    def flip(v):
        return v ^ ((v >> 31) & jnp.int32(0x7FFFFFFF))

    def group_reduce(fn, v):
        return fn(v.reshape(ck // SUBLANES, SUBLANES, tq), axis=0)

    def score_chunk(c, carry, masked):
        kmax8, kmin8 = carry
        off = pl.multiple_of(c * ck, ck)
        ki = ki_ref[0, pl.ds(off, ck), :]
        acc = jnp.zeros((ck, tq), F32)
        for h in range(IDX_HEADS):
            acc = acc + jnp.maximum(_dot_nt(ki, qm[h]), 0.0) * w[h:h + 1, :]
        key = flip(lax.bitcast_convert_type(acc, I32))
        key_lo = key
        if masked:
            causal = off + kpos_l <= qpos
            key_lo = jnp.where(causal, key, jnp.int32(2 ** 31 - 1))
            key = jnp.where(causal, key, msb)
        keys_ref[pl.ds(off, ck), :] = key
        return (jnp.maximum(kmax8, group_reduce(jnp.max, key)),
                jnp.minimum(kmin8, group_reduce(jnp.min, key_lo)))

    ext = (jnp.full((SUBLANES, tq), INT_MIN, I32), jnp.full((SUBLANES, tq), 2 ** 31 - 1, I32))
    ext = lax.fori_loop(0, n_full, lambda c, cr: score_chunk(c, cr, False), ext)
    for d in range(tq // ck):
        ext = score_chunk(n_full + d, ext, True)
    kmax = jnp.max(ext[0], axis=0, keepdims=True)
    kmin = jnp.min(ext[1], axis=0, keepdims=True)

    def count(pred):
        def body(c, part):
            off = pl.multiple_of(c * ck, ck)
            hit = jnp.where(pred(keys_ref[pl.ds(off, ck), :], off), 1, 0)
            return part + group_reduce(jnp.sum, hit)
        part = lax.fori_loop(0, n_chunks, body, jnp.zeros((SUBLANES, tq), I32))
        return jnp.sum(part, axis=0, keepdims=True)

    need = qpos_row >= topk

    def settle(state, cand_k, c, lo_k, hi_k):
        done, tie, t_res = state
        hit = c == topk
        fin = (done == 0) & (hit | (hi_k - lo_k == 1))
        t_res = jnp.where(fin, jnp.where(hit, cand_k, lo_k), t_res)
        tie = jnp.where(fin & ~hit, 1, tie)
        return jnp.where(fin, 1, done), tie, t_res

    def bisect_step(state):
        it, flags, lo_k, hi_k = state
        cand_k = lo_k + lax.shift_right_logical(hi_k - lo_k, 1)
        c = count(lambda kk, off: kk >= cand_k)
        ge = c >= topk
        lo_k = jnp.where(ge, cand_k, lo_k)
        hi_k = jnp.where(ge, hi_k, cand_k)
        return it + 1, settle(flags, cand_k, c, lo_k, hi_k), lo_k, hi_k

    def snap_step(state):
        it, flags, lo_k, hi_k = state

        def top_body(c, part):
            off = pl.multiple_of(c * ck, ck)
            kk = keys_ref[pl.ds(off, ck), :]
            return jnp.maximum(part, group_reduce(jnp.max, jnp.where(kk < hi_k, kk, msb)))

        top8 = lax.fori_loop(0, n_chunks, top_body, jnp.full((SUBLANES, tq), INT_MIN, I32))
        cand_k = jnp.maximum(jnp.max(top8, axis=0, keepdims=True), lo_k)
        c = count(lambda kk, off: kk >= cand_k)
        ge = c >= topk
        lo_k = jnp.where(ge, cand_k, lo_k)
        hi_k = jnp.where(ge, cand_k + 1, cand_k)
        return it + 1, settle(flags, cand_k, c, lo_k, hi_k), lo_k, hi_k

    def unsettled(state):
        return jnp.min(state[1][0]) == 0

    lo0, hi0 = kmin, kmax + 1
    flat0 = need & (hi0 - lo0 == 1)
    flags = (jnp.where(need & ~flat0, 0, 1), jnp.where(flat0, 1, 0), jnp.where(need, lo0, msb + 1))
    state = (jnp.int32(0), flags, lo0, hi0)
    state = lax.fori_loop(0, BISECT_STEPS, lambda _, st: bisect_step(st), state)
    state = lax.while_loop(unsettled, lambda st: bisect_step(snap_step(st)), state)
    _, (_, tie_flag, t_s), _, _ = state
    tie = tie_flag > 0

    @pl.when(jnp.max(jnp.where(tie, 1, 0)) > 0)
    def _():
        n_gt = count(lambda kk, off: kk > t_s)
        want = topk - n_gt

        def idx_body(state):
            jdone, j_res, lo_j, hi_j = state
            mid = (lo_j + hi_j) >> 1
            c = count(lambda kk, off: (kk == t_s) & (off + kpos_l <= mid))
            lo_j = jnp.where(c < want, mid, lo_j)
            hi_j = jnp.where(c < want, hi_j, mid)
            fin = (jdone == 0) & ((c == want) | (hi_j - lo_j == 1))
            j_res = jnp.where(fin, jnp.where(c == want, mid, hi_j), j_res)
            return jnp.where(fin, 1, jdone), j_res, lo_j, hi_j

        init_j = (jnp.where(tie, 0, 1), jnp.zeros((1, tq), I32),
                  jnp.full((1, tq), -1, I32), jnp.full((1, tq), seq - 1, I32))
        _, j_keep, _, _ = lax.while_loop(lambda st: jnp.min(st[0]) == 0, idx_body, init_j)

        def demote(c, carry):
            off = pl.multiple_of(c * ck, ck)
            kk = keys_ref[pl.ds(off, ck), :]
            drop = tie & (kk == t_s) & (off + kpos_l > j_keep)
            keys_ref[pl.ds(off, ck), :] = jnp.where(drop, msb, kk)
            return carry

        lax.fori_loop(0, n_chunks, demote, 0)

    t_fin = jnp.maximum(t_s, msb + 1)
    qs = []
    for h in range(N_HEADS):
        pair, half = divmod(h, 2)
        qs.append(_half_mask(qc_ref[0, :, pair * LANES:(pair + 1) * LANES], half))
    acc_ref[...] = jnp.zeros(acc_ref.shape, F32)

    def attn_body(c, carry):
        ms, ls = carry
        off = pl.multiple_of(c * ck, ck)
        sel = keys_ref[pl.ds(off, ck), :] >= t_fin
        ss = [_dot_nt(kc_ref[0, pl.ds(off, ck), (h // 2) * LANES:(h // 2 + 1) * LANES], qs[h])
              for h in range(N_HEADS)]
        new_ms, new_ls = [], []
        for h in range(N_HEADS):
            pair = h // 2
            p, m_new, l8, alpha = _softmax_step(ss[h], ms[h], ls[h], sel)
            vt = vct_ref[0, pair * LANES:(pair + 1) * LANES, pl.ds(off, ck)]
            acc_ref[h] = alpha * acc_ref[h] + _dot(vt, p)
            new_ms.append(m_new)
            new_ls.append(l8)
        return tuple(new_ms), tuple(new_ls)

    init = (tuple(jnp.full((1, tq), NEG, F32) for _ in range(N_HEADS)),
            tuple(jnp.zeros((SUBLANES, tq), F32) for _ in range(N_HEADS)))
    _, ls = lax.fori_loop(0, n_chunks, attn_body, init)
    sub = lax.broadcasted_iota(I32, (LANES, tq), 0)
    for pair in range(2):
        o0 = acc_ref[2 * pair] / jnp.sum(ls[2 * pair], axis=0, keepdims=True)
        o1 = acc_ref[2 * pair + 1] / jnp.sum(ls[2 * pair + 1], axis=0, keepdims=True)
        o_t = jnp.where(sub < HEAD_DIM, o0, o1)
        o_ref[0, :, pair * LANES:(pair + 1) * LANES] = o_t.T.astype(o_ref.dtype)


def _dsa(qi, misc_t, ki, qc, kc, vc_t, *, tq, ck, topk):
    batch, seq, _ = qc.shape
    return pl.pallas_call(
        functools.partial(_dsa_kernel, tq=tq, ck=ck, topk=topk, seq=seq),
        grid=(batch, seq // tq),
        in_specs=[pl.BlockSpec((1, tq, IDX_HEADS * IDX_DIM), lambda b, i: (b, i, 0)),
                  pl.BlockSpec((1, MISC_ROWS, tq), lambda b, i: (b, 0, i)),
                  pl.BlockSpec((1, seq, LANES), lambda b, i: (b, 0, 0)),
                  pl.BlockSpec((1, tq, GROUP_WIDTH), lambda b, i: (b, i, 0)),
                  pl.BlockSpec((1, seq, GROUP_WIDTH), lambda b, i: (b, 0, 0)),
                  pl.BlockSpec((1, GROUP_WIDTH, seq), lambda b, i: (b, 0, 0))],
        out_specs=pl.BlockSpec((1, tq, GROUP_WIDTH), lambda b, i: (b, i, 0)),
        out_shape=jax.ShapeDtypeStruct((batch, seq, GROUP_WIDTH), BF16),
        scratch_shapes=[pltpu.VMEM((seq, tq), I32),
                        pltpu.VMEM((N_HEADS, LANES, tq), F32)],
        compiler_params=pltpu.CompilerParams(
            dimension_semantics=("parallel", "arbitrary"), vmem_limit_bytes=VMEM_LIMIT),
        name="dsa",
    )(qi, misc_t, ki, qc, kc, vc_t)


def _mlp_kernel(oa_ref, ob_ref, oc_ref, od_ref, x_ref, wo_ref, g2_ref, wup_ref, wdn_ref,
                gf_ref, out_ref, *, ff_chunk, final):
    gw = GROUP_WIDTH
    mix = None
    for g, o_ref in enumerate((oa_ref, ob_ref, oc_ref, od_ref)):
        y = _dot(o_ref[...], wo_ref[g * gw:(g + 1) * gw, :])
        mix = y if mix is None else mix + y
    x1 = x_ref[...] + mix
    h2 = _rms(x1, g2_ref[...]).astype(BF16)
    ffn = None
    for c in range(wup_ref.shape[1] // ff_chunk):
        u = _dot(h2, wup_ref[:, c * ff_chunk:(c + 1) * ff_chunk])
        a = jnp.square(jnp.maximum(u, 0.0)).astype(BF16)
        y = _dot(a, wdn_ref[c * ff_chunk:(c + 1) * ff_chunk, :])
        ffn = y if ffn is None else ffn + y
    acc = x1 + ffn
    if final:
        acc = _rms(acc, gf_ref[...])
    out_ref[...] = acc


def _mlp(oa, ob, oc, od, x2d, wo, g2, wup, wdn, gf, *, tm, final):
    n, d = x2d.shape
    dff = wup.shape[1]
    row = lambda i: (i, 0)
    const = lambda i: (0, 0)
    once = pl.Buffered(1)
    o_spec = pl.BlockSpec((tm, GROUP_WIDTH), row)
    return pl.pallas_call(
        functools.partial(_mlp_kernel, ff_chunk=1024, final=final),
        grid=(n // tm,),
        in_specs=[o_spec, o_spec, o_spec, o_spec,
                  pl.BlockSpec((tm, d), row),
                  pl.BlockSpec((d, d), const, pipeline_mode=once),
                  pl.BlockSpec((1, d), const),
                  pl.BlockSpec((d, dff), const, pipeline_mode=once),
                  pl.BlockSpec((dff, d), const, pipeline_mode=once),
                  pl.BlockSpec((1, d), const)],
        out_specs=pl.BlockSpec((tm, d), row),
        out_shape=jax.ShapeDtypeStruct((n, d), F32),
        compiler_params=pltpu.CompilerParams(
            dimension_semantics=("parallel",), vmem_limit_bytes=VMEM_LIMIT),
        name="outproj_mlp",
    )(oa, ob, oc, od, x2d, wo, g2, wup, wdn, gf)


def _rope_tables(seq):
    pos = jnp.arange(seq, dtype=F32)[:, None]
    lane = jnp.arange(LANES)

    def table(dim, active):
        half = dim // 2
        inv_freq = 1.0 / (ROPE_THETA ** (jnp.arange(0, half, dtype=F32) * 2.0 / dim))
        ang = pos * inv_freq[None, :][:, lane % half]
        sign = jnp.where((lane % dim) < half, -1.0, 1.0)
        cos = jnp.where(active, jnp.cos(ang), 1.0)
        sin = jnp.where(active, jnp.sin(ang) * sign, 0.0)
        return cos.astype(F32), sin.astype(F32)

    c64, s64 = table(HEAD_DIM, jnp.ones((LANES,), bool))
    c32, s32 = table(IDX_DIM, jnp.ones((LANES,), bool))
    cm, sm = table(MLA_ROPE, (lane >= MLA_NOPE) & (lane < MLA_NOPE + MLA_ROPE))
    return c64, s64, c32, s32, cm, sm


def _pack_weights(w_in, mla_w_uq, mla_w_ukv, w_out):
    depth, d, _ = w_in.shape
    splits = (MLA_Q_RANK, MLA_KV_RANK, MLA_ROPE,
              GROUP_WIDTH, SWA_KV_HEADS * HEAD_DIM, SWA_KV_HEADS * HEAD_DIM,
              GROUP_WIDTH, GROUP_WIDTH, GROUP_WIDTH, IDX_HEADS * IDX_DIM, IDX_DIM, IDX_HEADS,
              GROUP_WIDTH, GROUP_WIDTH, GROUP_WIDTH, N_HEADS)
    offs = [0]
    for s in splits:
        offs.append(offs[-1] + s)
    (a_cq, a_ckv, a_kr, b_q, b_k, b_v, c_q, c_k, c_v, c_qi, c_ki, c_w,
     d_q, d_k, d_v, d_f) = [w_in[:, :, offs[j]:offs[j + 1]] for j in range(len(splits))]
    zeros = lambda n: jnp.zeros((depth, d, n), w_in.dtype)
    swap = jnp.array([0, 2, 1, 3])
    a_kr = jnp.concatenate([zeros(MLA_NOPE), a_kr, zeros(LANES - MLA_NOPE - MLA_ROPE)], -1)
    b_q = b_q.reshape(depth, d, N_HEADS, HEAD_DIM)[:, :, swap].reshape(depth, d, GROUP_WIDTH)
    c_ki = jnp.concatenate([c_ki] * (LANES // IDX_DIM), -1)
    w_row = jnp.concatenate(
        [a_cq, a_ckv, a_kr, b_q, b_k, b_v, c_q, c_k, c_qi, c_ki, d_q, d_k], -1).astype(BF16)
    w_t = jnp.concatenate([c_v, d_v, c_w, d_f, zeros(MISC_ROWS - IDX_HEADS - N_HEADS)], -1)
    w_t = jnp.swapaxes(w_t, 1, 2).astype(BF16)

    pad = LANES - MLA_NOPE - MLA_ROPE
    wuq = mla_w_uq.reshape(depth, MLA_Q_RANK, N_HEADS, MLA_NOPE + MLA_ROPE)
    wuq = jnp.pad(wuq, ((0, 0), (0, 0), (0, 0), (0, pad))).reshape(depth, MLA_Q_RANK, N_HEADS * LANES)
    wukv = mla_w_ukv.reshape(depth, MLA_KV_RANK, N_HEADS, MLA_NOPE + HEAD_DIM)
    wuk = jnp.pad(wukv[..., :MLA_NOPE], ((0, 0), (0, 0), (0, 0), (0, LANES - MLA_NOPE)))
    wuk = wuk.reshape(depth, MLA_KV_RANK, N_HEADS * LANES)
    wuvt = jnp.swapaxes(wukv[..., MLA_NOPE:].reshape(depth, MLA_KV_RANK, GROUP_WIDTH), 1, 2)

    wo_b = w_out[:, GROUP_WIDTH:2 * GROUP_WIDTH].reshape(depth, N_HEADS, HEAD_DIM, -1)[:, swap]
    wo = jnp.concatenate([w_out[:, :GROUP_WIDTH], wo_b.reshape(depth, GROUP_WIDTH, -1),
                          w_out[:, 2 * GROUP_WIDTH:]], 1)
    return w_row, w_t, wuq.astype(BF16), wuk.astype(BF16), wuvt.astype(BF16), wo.astype(BF16)


_MLA_HEADS = tuple((h * LANES, None, h * LANES, h // 2) for h in range(N_HEADS))
_FOX_HEADS = tuple(((h // 2) * LANES, h % 2, (h // 2) * LANES, h // 2) for h in range(N_HEADS))


def kernel(x, norm1, w_in, mla_q_norm, mla_kv_norm, mla_w_uq, mla_w_ukv, swa_sinks, fox_b_f,
           w_out, norm2, w_up, w_down, final_norm):
    batch, seq, d = x.shape
    depth = w_in.shape[0]
    n = batch * seq
    top_k = min(DSA_TOPK, seq // 4)
    tm = min(512, seq)
    tq_flash = min(256, seq)
    ck_flash = min(256, seq)
    tq_swa = min(256, seq)
    tq_dsa = min(256, seq)
    ck_dsa = min(256, seq)

    tables = _rope_tables(seq)
    w_row, w_t, wuq, wuk, wuvt, wo = _pack_weights(w_in, mla_w_uq, mla_w_ukv, w_out)
    wup = w_up.astype(BF16)
    wdn = w_down.astype(BF16)

    x2d = x.reshape(n, d)
    for l in range(depth):
        (qa, ka, qb, kb, vb, qc, kc, qi, ki, qd, kd, va_t, vc_t, vd_t, misc_t) = _inproj(
            x2d, norm1[l][None], w_row[l], w_t[l], mla_q_norm[l][None], mla_kv_norm[l][None],
            wuq[l], wuk[l], wuvt[l], tables, batch=batch, seq=seq, tm=tm)
        r3 = lambda a: a.reshape(batch, seq, a.shape[-1])
        cum = _forget_cumsum(misc_t[:, IDX_HEADS:IDX_HEADS + N_HEADS], fox_b_f[l],
                             batch=batch, seq=seq)
        cum_col = jnp.swapaxes(cum, 1, 2)
        o_a = _flash(r3(qa), r3(ka), va_t, None, _MLA_HEADS, tq=tq_flash, ck=ck_flash)
        o_b = _swa(r3(qb), r3(kb), r3(vb), swa_sinks[l], tq=tq_swa)
        o_c = _dsa(r3(qi), misc_t, r3(ki), r3(qc), r3(kc), vc_t, tq=tq_dsa, ck=ck_dsa, topk=top_k)
        o_d = _flash(r3(qd), r3(kd), vd_t, cum_col, _FOX_HEADS, tq=tq_flash, ck=ck_flash)
        x2d = _mlp(o_a.reshape(n, -1), o_b.reshape(n, -1), o_c.reshape(n, -1), o_d.reshape(n, -1),
                   x2d, wo[l], norm2[l][None], wup[l], wdn[l], final_norm[None],
                   tm=tm, final=(l == depth - 1))
    return x2d.reshape(batch, seq, d)
```

```python
import functools
import math

import jax
import jax.numpy as jnp
from jax import lax
from jax.experimental import pallas as pl
from jax.experimental.pallas import tpu as pltpu

F32 = jnp.float32
BF16 = jnp.bfloat16
I32 = jnp.int32

HEAD_DIM = 64
N_HEADS = 4
GROUP_WIDTH = N_HEADS * HEAD_DIM
MLA_Q_RANK = 256
MLA_KV_RANK = 128
MLA_NOPE = 64
MLA_ROPE = 32
SWA_KV_HEADS = 2
SWA_WINDOW = 128
IDX_HEADS = 8
IDX_DIM = 32
DSA_TOPK = 256
ROPE_THETA = 10000.0
EPS = 1e-6

LANES = 128
SUBLANES = 8
VMEM_LIMIT = 56 * 1024 * 1024
NEG = -1e30
INT_MIN = -(2 ** 31)
LOG2E = math.log2(math.e)
BISECT_STEPS = 18

_ROW_GROUPS = (
    ("a_cq", 256), ("a_ckv", 128), ("a_kr", 128),
    ("b_q", 256), ("b_k", 128), ("b_v", 128),
    ("c_q", 256), ("c_k", 256), ("c_qi", 256), ("c_ki", 128),
    ("d_q", 256), ("d_k", 256),
)
_ROW_OFF = {}
_acc = 0
for _n, _w in _ROW_GROUPS:
    _ROW_OFF[_n] = (_acc, _acc + _w)
    _acc += _w
ROW_WIDTH = _acc
MISC_ROWS = 16
T_ROWS = 2 * GROUP_WIDTH + MISC_ROWS


def _dot(a, b):
    return jnp.dot(a, b, preferred_element_type=F32)


def _dot_nt(a, b):
    return lax.dot_general(a, b, (((1,), (1,)), ((), ())), preferred_element_type=F32)


def _rms(x, g):
    return x * lax.rsqrt(jnp.mean(x * x, axis=-1, keepdims=True) + EPS) * g


def _rope(x, cos, sin, half):
    width = x.shape[1]
    reps = width // LANES
    if reps > 1:
        cos = jnp.concatenate([cos] * reps, axis=1)
        sin = jnp.concatenate([sin] * reps, axis=1)
    lane = lax.broadcasted_iota(I32, x.shape, 1)
    first = (lane % (2 * half)) < half
    rot = jnp.where(first, pltpu.roll(x, width - half, 1), pltpu.roll(x, half, 1))
    return x * cos + rot * sin


def _half_mask(q, half):
    lane = lax.broadcasted_iota(I32, q.shape, 1)
    return jnp.where((lane >= HEAD_DIM) == bool(half), q, jnp.zeros_like(q))


def _softmax_step(s, m, l8, mask):
    ck, tq = s.shape
    if mask is not None:
        s = jnp.where(mask, s, NEG)
    m_new = jnp.maximum(m, jnp.max(s, axis=0, keepdims=True))
    alpha = jnp.exp2(m - m_new)
    p = jnp.exp2(s - m_new)
    l8 = alpha * l8 + jnp.sum(p.reshape(ck // SUBLANES, SUBLANES, tq), axis=0)
    return p.astype(BF16), m_new, l8, alpha


def _inproj_kernel(x_ref, g1_ref, w_ref, wt_ref, qn_ref, kvn_ref, wuq_ref, wuk_ref, wuvt_ref,
                   c64_ref, s64_ref, c32_ref, s32_ref, cm_ref, sm_ref,
                   qa_ref, ka_ref, qb_ref, kb_ref, vb_ref,
                   qc_ref, kc_ref, qi_ref, ki_ref, qd_ref, kd_ref,
                   vat_ref, vct_ref, vdt_ref, misct_ref):
    h = _rms(x_ref[...], g1_ref[...]).astype(BF16)

    def proj(name):
        lo, hi = _ROW_OFF[name]
        return _dot(h, w_ref[:, lo:hi])

    c64, s64 = c64_ref[...], s64_ref[...]
    c32, s32 = c32_ref[...], s32_ref[...]
    cm, sm = cm_ref[...], sm_ref[...]
    scale_a = (MLA_NOPE + MLA_ROPE) ** -0.5 * LOG2E
    scale = HEAD_DIM ** -0.5 * LOG2E

    cq = _rms(proj("a_cq"), qn_ref[...]).astype(BF16)
    qa = _rope(_dot(cq, wuq_ref[...]), cm, sm, MLA_ROPE // 2) * scale_a
    qa_ref[...] = qa.astype(BF16)
    ckv = _rms(proj("a_ckv"), kvn_ref[...]).astype(BF16)
    kr = _rope(proj("a_kr"), cm, sm, MLA_ROPE // 2)
    ka_ref[...] = (_dot(ckv, wuk_ref[...]) + jnp.concatenate([kr] * N_HEADS, axis=1)).astype(BF16)
    vat_ref[0] = _dot_nt(wuvt_ref[...], ckv).astype(BF16)

    qb_ref[...] = (_rope(proj("b_q"), c64, s64, HEAD_DIM // 2) * scale).astype(BF16)
    kb_ref[...] = _rope(proj("b_k"), c64, s64, HEAD_DIM // 2).astype(BF16)
    vb_ref[...] = proj("b_v").astype(BF16)

    qc_ref[...] = (_rope(proj("c_q"), c64, s64, HEAD_DIM // 2) * scale).astype(BF16)
    kc_ref[...] = _rope(proj("c_k"), c64, s64, HEAD_DIM // 2).astype(BF16)
    qi_ref[...] = _rope(proj("c_qi"), c32, s32, IDX_DIM // 2).astype(BF16)
    ki_ref[...] = _rope(proj("c_ki"), c32, s32, IDX_DIM // 2).astype(BF16)

    qd_ref[...] = (proj("d_q") * scale).astype(BF16)
    kd_ref[...] = proj("d_k").astype(BF16)

    t = _dot_nt(wt_ref[...], h)
    vct_ref[0] = t[:GROUP_WIDTH].astype(BF16)
    vdt_ref[0] = t[GROUP_WIDTH:2 * GROUP_WIDTH].astype(BF16)
    misct_ref[0] = t[2 * GROUP_WIDTH:]


def _inproj(x2d, g1, w_row, w_t, qn, kvn, wuq, wuk, wuvt, tables, *, batch, seq, tm):
    n = x2d.shape[0]
    d = x2d.shape[1]
    spb = seq // tm
    row = lambda i: (i, 0)
    const = lambda i: (0, 0)
    tab = lambda i: (i % spb, 0)
    tr = lambda i: (i // spb, 0, i % spb)
    once = pl.Buffered(1)

    def out2(width):
        return jax.ShapeDtypeStruct((n, width), BF16), pl.BlockSpec((tm, width), row)

    def out_t(rows, dtype):
        return jax.ShapeDtypeStruct((batch, rows, seq), dtype), pl.BlockSpec((1, rows, tm), tr)

    outs = [out2(512), out2(512),
            out2(256), out2(128), out2(128),
            out2(256), out2(256), out2(256), out2(128),
            out2(256), out2(256),
            out_t(GROUP_WIDTH, BF16), out_t(GROUP_WIDTH, BF16), out_t(GROUP_WIDTH, BF16),
            out_t(MISC_ROWS, F32)]
    in_specs = [
        pl.BlockSpec((tm, d), row),
        pl.BlockSpec((1, d), const),
        pl.BlockSpec((d, ROW_WIDTH), const, pipeline_mode=once),
        pl.BlockSpec((T_ROWS, d), const, pipeline_mode=once),
        pl.BlockSpec((1, MLA_Q_RANK), const),
        pl.BlockSpec((1, MLA_KV_RANK), const),
        pl.BlockSpec((MLA_Q_RANK, N_HEADS * LANES), const, pipeline_mode=once),
        pl.BlockSpec((MLA_KV_RANK, N_HEADS * LANES), const, pipeline_mode=once),
        pl.BlockSpec((GROUP_WIDTH, MLA_KV_RANK), const, pipeline_mode=once),
    ] + [pl.BlockSpec((tm, LANES), tab)] * 6
    return pl.pallas_call(
        _inproj_kernel,
        grid=(n // tm,),
        in_specs=in_specs,
        out_specs=[o[1] for o in outs],
        out_shape=[o[0] for o in outs],
        compiler_params=pltpu.CompilerParams(
            dimension_semantics=("parallel",), vmem_limit_bytes=VMEM_LIMIT),
        name="inproj",
    )(x2d, g1, w_row, w_t, qn, kvn, wuq, wuk, wuvt, *tables)


def _cumsum_kernel(f_ref, b_ref, o_ref, *, segs):
    x = f_ref[...] + b_ref[...]
    ls = jnp.minimum(x, 0.0) - jnp.log1p(jnp.exp(-jnp.abs(x)))
    rows = x.shape[0]
    r = lax.broadcasted_iota(I32, (LANES, LANES), 0)
    c = lax.broadcasted_iota(I32, (LANES, LANES), 1)
    upper = (r <= c).astype(F32)
    within = jnp.dot(ls, upper, preferred_element_type=F32, precision=lax.Precision.HIGHEST)
    tot = jnp.broadcast_to(within[:, LANES - 1:LANES], (rows, LANES))
    rr = lax.broadcasted_iota(I32, (rows, rows), 0)
    cc = lax.broadcasted_iota(I32, (rows, rows), 1)
    before = ((cc < rr) & (cc // segs == rr // segs)).astype(F32)
    off = jnp.dot(before, tot, preferred_element_type=F32, precision=lax.Precision.HIGHEST)
    o_ref[...] = (within + off) * LOG2E


def _forget_cumsum(f_t, bias, *, batch, seq):
    segs = seq // LANES
    rows = N_HEADS * segs
    f2 = f_t.reshape(batch * rows, LANES)
    b2 = jnp.broadcast_to(jnp.repeat(bias, segs)[:, None], (rows, LANES))
    out = pl.pallas_call(
        functools.partial(_cumsum_kernel, segs=segs),
        grid=(batch,),
        in_specs=[pl.BlockSpec((rows, LANES), lambda b: (b, 0)),
                  pl.BlockSpec((rows, LANES), lambda b: (0, 0))],
        out_specs=pl.BlockSpec((rows, LANES), lambda b: (b, 0)),
        out_shape=jax.ShapeDtypeStruct((batch * rows, LANES), F32),
        compiler_params=pltpu.CompilerParams(dimension_semantics=("parallel",)),
        name="forget_cumsum",
    )(f2, b2)
    return out.reshape(batch, N_HEADS, seq)


def _flash_kernel(*refs, heads, tq, ck, has_bias):
    if has_bias:
        q_ref, k_ref, vt_ref, b_ref, o_ref, acc_ref, sa_ref, sb_ref = refs
    else:
        q_ref, k_ref, vt_ref, o_ref, acc_ref, sa_ref, sb_ref = refs
        b_ref = None
    assert tq == ck
    nh = len(heads)
    i = pl.program_id(1)
    q_start = i * tq
    n_full = q_start // ck
    kpos_l = lax.broadcasted_iota(I32, (ck, tq), 0)
    qpos = q_start + lax.broadcasted_iota(I32, (ck, tq), 1)
    qs = []
    for ql, qhalf, _, _ in heads:
        q = q_ref[0, :, ql:ql + LANES]
        qs.append(q if qhalf is None else _half_mask(q, qhalf))
    acc_ref[...] = jnp.zeros(acc_ref.shape, F32)

    def scores(c, s_ref):
        off = pl.multiple_of(c * ck, ck)
        for h, (_, _, kl, _) in enumerate(heads):
            s = _dot_nt(k_ref[0, pl.ds(off, ck), kl:kl + LANES], qs[h])
            if has_bias:
                s = s - b_ref[0, pl.ds(off, ck), h:h + 1]
            s_ref[h] = s

    def consume(c, s_ref, carry, masked):
        ms, ls = carry
        off = pl.multiple_of(c * ck, ck)
        mask = (off + kpos_l <= qpos) if masked else None
        new_ms, new_ls = [], []
        for h, (_, _, _, pair) in enumerate(heads):
            p, m_new, l8, alpha = _softmax_step(s_ref[h], ms[h], ls[h], mask)
            vt = vt_ref[0, pair * LANES:(pair + 1) * LANES, pl.ds(off, ck)]
            acc_ref[h] = alpha * acc_ref[h] + _dot(vt, p)
            new_ms.append(m_new)
            new_ls.append(l8)
        return tuple(new_ms), tuple(new_ls)

    def pair_body(p, carry):
        c = 2 * p
        scores(c + 1, sb_ref)
        carry = consume(c, sa_ref, carry, False)
        scores(c + 2, sa_ref)
        return consume(c + 1, sb_ref, carry, False)

    carry = (tuple(jnp.full((1, tq), NEG, F32) for _ in range(nh)),
             tuple(jnp.zeros((SUBLANES, tq), F32) for _ in range(nh)))
    scores(0, sa_ref)
    carry = lax.fori_loop(0, n_full // 2, pair_body, carry)
    odd = n_full % 2 == 1

    def tail_odd(carry):
        scores(n_full, sb_ref)
        carry = consume(n_full - 1, sa_ref, carry, False)
        return consume(n_full, sb_ref, carry, True)

    carry = lax.cond(odd, tail_odd, lambda cr: consume(n_full, sa_ref, cr, True), carry)
    _, ls = carry
    sub = lax.broadcasted_iota(I32, (LANES, tq), 0)
    for pair in range(nh // 2):
        o0 = acc_ref[2 * pair] / jnp.sum(ls[2 * pair], axis=0, keepdims=True)
        o1 = acc_ref[2 * pair + 1] / jnp.sum(ls[2 * pair + 1], axis=0, keepdims=True)
        o_t = jnp.where(sub < HEAD_DIM, o0, o1)
        o_ref[0, :, pair * LANES:(pair + 1) * LANES] = o_t.T.astype(o_ref.dtype)


def _flash(q, k, v_t, bias, heads, *, tq, ck):
    batch, seq, wq = q.shape
    wk = k.shape[2]
    has_bias = bias is not None
    in_specs = [pl.BlockSpec((1, tq, wq), lambda b, i: (b, i, 0)),
                pl.BlockSpec((1, seq, wk), lambda b, i: (b, 0, 0)),
                pl.BlockSpec((1, GROUP_WIDTH, seq), lambda b, i: (b, 0, 0))]
    args = [q, k, v_t]
    if has_bias:
        in_specs.append(pl.BlockSpec((1, seq, N_HEADS), lambda b, i: (b, 0, 0)))
        args.append(bias)
    return pl.pallas_call(
        functools.partial(_flash_kernel, heads=heads, tq=tq, ck=ck, has_bias=has_bias),
        grid=(batch, seq // tq),
        in_specs=in_specs,
        out_specs=pl.BlockSpec((1, tq, GROUP_WIDTH), lambda b, i: (b, i, 0)),
        out_shape=jax.ShapeDtypeStruct((batch, seq, GROUP_WIDTH), BF16),
        scratch_shapes=[pltpu.VMEM((len(heads), LANES, tq), F32),
                        pltpu.VMEM((len(heads), ck, tq), F32),
                        pltpu.VMEM((len(heads), ck, tq), F32)],
        compiler_params=pltpu.CompilerParams(
            dimension_semantics=("parallel", "arbitrary"), vmem_limit_bytes=VMEM_LIMIT),
        name="flash_bias" if has_bias else "flash",
    )(*args)


def _swa_kernel(sink_ref, q_ref, k_ref, v_ref, o_ref, *, tq, window):
    i = pl.program_id(1)
    q_start = i * tq
    span = tq + window
    start = pl.multiple_of(jnp.maximum(q_start - window, 0), LANES)
    k = k_ref[0, pl.ds(start, span), :]
    v = v_ref[0, pl.ds(start, span), :]
    lane = lax.broadcasted_iota(I32, (tq, LANES), 1)
    qpos = q_start + lax.broadcasted_iota(I32, (tq, span), 0)
    kpos = start + lax.broadcasted_iota(I32, (tq, span), 1)
    valid = (kpos <= qpos) & (qpos - kpos < window)
    for pair in range(2):
        qp = q_ref[0, :, pair * LANES:(pair + 1) * LANES]
        outs = []
        for half in range(2):
            head = pair + 2 * half
            sink = sink_ref[head] * LOG2E
            s = jnp.where(valid, _dot_nt(_half_mask(qp, half), k), NEG)
            m = jnp.maximum(jnp.max(s, axis=1, keepdims=True), sink)
            p = jnp.exp2(s - m)
            denom = jnp.sum(p, axis=1, keepdims=True) + jnp.exp2(sink - m)
            outs.append(_dot(p.astype(BF16), v) / denom)
        o_ref[0, :, pair * LANES:(pair + 1) * LANES] = jnp.where(
            lane < HEAD_DIM, outs[0], outs[1]).astype(o_ref.dtype)


def _swa(q, k, v, sinks, *, tq):
    batch, seq, _ = q.shape
    return pl.pallas_call(
        functools.partial(_swa_kernel, tq=tq, window=SWA_WINDOW),
        grid=(batch, seq // tq),
        in_specs=[pl.BlockSpec(memory_space=pltpu.SMEM),
                  pl.BlockSpec((1, tq, GROUP_WIDTH), lambda b, i: (b, i, 0)),
                  pl.BlockSpec((1, seq, LANES), lambda b, i: (b, 0, 0)),
                  pl.BlockSpec((1, seq, LANES), lambda b, i: (b, 0, 0))],
        out_specs=pl.BlockSpec((1, tq, GROUP_WIDTH), lambda b, i: (b, i, 0)),
        out_shape=jax.ShapeDtypeStruct((batch, seq, GROUP_WIDTH), BF16),
        compiler_params=pltpu.CompilerParams(
            dimension_semantics=("parallel", "arbitrary"), vmem_limit_bytes=VMEM_LIMIT),
        name="swa",
    )(sinks, q, k, v)


def _dsa_kernel(qi_ref, w_ref, ki_ref, qc_ref, kc_ref, vct_ref, o_ref,
                keys_ref, acc_ref, sa_ref, sb_ref, *, tq, ck, topk, seq):
    assert tq == ck
    i = pl.program_id(1)
    q_start = i * tq
    n_chunks = (q_start + tq) // ck
    n_full = q_start // ck
    lane = lax.broadcasted_iota(I32, (tq, LANES), 1)
    kpos_l = lax.broadcasted_iota(I32, (ck, tq), 0)
    qpos = q_start + lax.broadcasted_iota(I32, (ck, tq), 1)
    qpos_row = q_start + lax.broadcasted_iota(I32, (1, tq), 1)
    msb = jnp.int32(INT_MIN)

    qm = []
    for h in range(IDX_HEADS):
        g, r = divmod(h, LANES // IDX_DIM)
        qg = qi_ref[0, :, g * LANES:(g + 1) * LANES]
        qm.append(jnp.where(lane // IDX_DIM == r, qg, jnp.zeros_like(qg)))
    w = w_ref[0]

    def flip(v):
        return v ^ ((v >> 31) & jnp.int32(0x7FFFFFFF))

    def group_reduce(fn, v):
        return fn(v.reshape(ck // SUBLANES, SUBLANES, tq), axis=0)

    def score_chunk(c, carry, masked):
        kmax8, kmin8 = carry
        off = pl.multiple_of(c * ck, ck)
        ki = ki_ref[0, pl.ds(off, ck), :]
        acc = jnp.zeros((ck, tq), F32)
        for h in range(IDX_HEADS):
            acc = acc + jnp.maximum(_dot_nt(ki, qm[h]), 0.0) * w[h:h + 1, :]
        key = flip(lax.bitcast_convert_type(acc, I32))
        key_lo = key
        if masked:
            causal = off + kpos_l <= qpos
            key_lo = jnp.where(causal, key, jnp.int32(2 ** 31 - 1))
            key = jnp.where(causal, key, msb)
        keys_ref[pl.ds(off, ck), :] = key
        return (jnp.maximum(kmax8, group_reduce(jnp.max, key)),
                jnp.minimum(kmin8, group_reduce(jnp.min, key_lo)))

    ext = (jnp.full((SUBLANES, tq), INT_MIN, I32), jnp.full((SUBLANES, tq), 2 ** 31 - 1, I32))
    ext = lax.fori_loop(0, n_full, lambda c, cr: score_chunk(c, cr, False), ext)
    for d in range(tq // ck):
        ext = score_chunk(n_full + d, ext, True)
    kmax = jnp.max(ext[0], axis=0, keepdims=True)
    kmin = jnp.min(ext[1], axis=0, keepdims=True)

    def count(pred):
        def body(c, part):
            off = pl.multiple_of(c * ck, ck)
            hit = jnp.where(pred(keys_ref[pl.ds(off, ck), :], off), 1, 0)
            return part + group_reduce(jnp.sum, hit)
        part = lax.fori_loop(0, n_chunks, body, jnp.zeros((SUBLANES, tq), I32))
        return jnp.sum(part, axis=0, keepdims=True)

    need = qpos_row >= topk

    def settle(state, cand_k, c, lo_k, hi_k):
        done, tie, t_res = state
        hit = c == topk
        fin = (done == 0) & (hit | (hi_k - lo_k == 1))
        t_res = jnp.where(fin, jnp.where(hit, cand_k, lo_k), t_res)
        tie = jnp.where(fin & ~hit, 1, tie)
        return jnp.where(fin, 1, done), tie, t_res

    def bisect_step(state):
        it, flags, lo_k, hi_k = state
        cand_k = lo_k + lax.shift_right_logical(hi_k - lo_k, 1)
        c = count(lambda kk, off: kk >= cand_k)
        ge = c >= topk
        lo_k = jnp.where(ge, cand_k, lo_k)
        hi_k = jnp.where(ge, hi_k, cand_k)
        return it + 1, settle(flags, cand_k, c, lo_k, hi_k), lo_k, hi_k

    def snap_step(state):
        it, flags, lo_k, hi_k = state

        def top_body(c, part):
            off = pl.multiple_of(c * ck, ck)
            kk = keys_ref[pl.ds(off, ck), :]
            return jnp.maximum(part, group_reduce(jnp.max, jnp.where(kk < hi_k, kk, msb)))

        top8 = lax.fori_loop(0, n_chunks, top_body, jnp.full((SUBLANES, tq), INT_MIN, I32))
        cand_k = jnp.maximum(jnp.max(top8, axis=0, keepdims=True), lo_k)
        c = count(lambda kk, off: kk >= cand_k)
        ge = c >= topk
        lo_k = jnp.where(ge, cand_k, lo_k)
        hi_k = jnp.where(ge, cand_k + 1, cand_k)
        return it + 1, settle(flags, cand_k, c, lo_k, hi_k), lo_k, hi_k

    def unsettled(state):
        return jnp.min(state[1][0]) == 0

    lo0, hi0 = kmin, kmax + 1
    flat0 = need & (hi0 - lo0 == 1)
    flags = (jnp.where(need & ~flat0, 0, 1), jnp.where(flat0, 1, 0), jnp.where(need, lo0, msb + 1))
    state = (jnp.int32(0), flags, lo0, hi0)
    state = lax.fori_loop(0, BISECT_STEPS, lambda _, st: bisect_step(st), state)
    state = lax.while_loop(unsettled, lambda st: bisect_step(snap_step(st)), state)
    _, (_, tie_flag, t_s), _, _ = state
    tie = tie_flag > 0

    @pl.when(jnp.max(jnp.where(tie, 1, 0)) > 0)
    def _():
        n_gt = count(lambda kk, off: kk > t_s)
        want = topk - n_gt

        def idx_body(state):
            jdone, j_res, lo_j, hi_j = state
            mid = (lo_j + hi_j) >> 1
            c = count(lambda kk, off: (kk == t_s) & (off + kpos_l <= mid))
            lo_j = jnp.where(c < want, mid, lo_j)
            hi_j = jnp.where(c < want, hi_j, mid)
            fin = (jdone == 0) & ((c == want) | (hi_j - lo_j == 1))
            j_res = jnp.where(fin, jnp.where(c == want, mid, hi_j), j_res)
            return jnp.where(fin, 1, jdone), j_res, lo_j, hi_j

        init_j = (jnp.where(tie, 0, 1), jnp.zeros((1, tq), I32),
                  jnp.full((1, tq), -1, I32), jnp.full((1, tq), seq - 1, I32))
        _, j_keep, _, _ = lax.while_loop(lambda st: jnp.min(st[0]) == 0, idx_body, init_j)

        def demote(c, carry):
            off = pl.multiple_of(c * ck, ck)
            kk = keys_ref[pl.ds(off, ck), :]
            drop = tie & (kk == t_s) & (off + kpos_l > j_keep)
            keys_ref[pl.ds(off, ck), :] = jnp.where(drop, msb, kk)
            return carry

        lax.fori_loop(0, n_chunks, demote, 0)

    t_fin = jnp.maximum(t_s, msb + 1)
    qs = []
    for h in range(N_HEADS):
        pair, half = divmod(h, 2)
        qs.append(_half_mask(qc_ref[0, :, pair * LANES:(pair + 1) * LANES], half))
    acc_ref[...] = jnp.zeros(acc_ref.shape, F32)

    def scores(c, s_ref):
        off = pl.multiple_of(c * ck, ck)
        for h in range(N_HEADS):
            k = kc_ref[0, pl.ds(off, ck), (h // 2) * LANES:(h // 2 + 1) * LANES]
            s_ref[h] = _dot_nt(k, qs[h])

    def consume(c, s_ref, carry):
        ms, ls = carry
        off = pl.multiple_of(c * ck, ck)
        sel = keys_ref[pl.ds(off, ck), :] >= t_fin
        new_ms, new_ls = [], []
        for h in range(N_HEADS):
            pair = h // 2
            p, m_new, l8, alpha = _softmax_step(s_ref[h], ms[h], ls[h], sel)
            vt = vct_ref[0, pair * LANES:(pair + 1) * LANES, pl.ds(off, ck)]
            acc_ref[h] = alpha * acc_ref[h] + _dot(vt, p)
            new_ms.append(m_new)
            new_ls.append(l8)
        return tuple(new_ms), tuple(new_ls)

    def pair_body(p, carry):
        c = 2 * p
        scores(c + 1, sb_ref)
        carry = consume(c, sa_ref, carry)
        scores(c + 2, sa_ref)
        return consume(c + 1, sb_ref, carry)

    carry = (tuple(jnp.full((1, tq), NEG, F32) for _ in range(N_HEADS)),
             tuple(jnp.zeros((SUBLANES, tq), F32) for _ in range(N_HEADS)))
    scores(0, sa_ref)
    carry = lax.fori_loop(0, n_full // 2, pair_body, carry)

    def tail_odd(carry):
        scores(n_full, sb_ref)
        return consume(n_full, sb_ref, consume(n_full - 1, sa_ref, carry))

    _, ls = lax.cond(n_full % 2 == 1, tail_odd, lambda cr: consume(n_full, sa_ref, cr), carry)
    sub = lax.broadcasted_iota(I32, (LANES, tq), 0)
    for pair in range(2):
        o0 = acc_ref[2 * pair] / jnp.sum(ls[2 * pair], axis=0, keepdims=True)
        o1 = acc_ref[2 * pair + 1] / jnp.sum(ls[2 * pair + 1], axis=0, keepdims=True)
        o_t = jnp.where(sub < HEAD_DIM, o0, o1)
        o_ref[0, :, pair * LANES:(pair + 1) * LANES] = o_t.T.astype(o_ref.dtype)


def _dsa(qi, misc_t, ki, qc, kc, vc_t, *, tq, ck, topk):
    batch, seq, _ = qc.shape
    return pl.pallas_call(
        functools.partial(_dsa_kernel, tq=tq, ck=ck, topk=topk, seq=seq),
        grid=(batch, seq // tq),
        in_specs=[pl.BlockSpec((1, tq, IDX_HEADS * IDX_DIM), lambda b, i: (b, i, 0)),
                  pl.BlockSpec((1, MISC_ROWS, tq), lambda b, i: (b, 0, i)),
                  pl.BlockSpec((1, seq, LANES), lambda b, i: (b, 0, 0)),
                  pl.BlockSpec((1, tq, GROUP_WIDTH), lambda b, i: (b, i, 0)),
                  pl.BlockSpec((1, seq, GROUP_WIDTH), lambda b, i: (b, 0, 0)),
                  pl.BlockSpec((1, GROUP_WIDTH, seq), lambda b, i: (b, 0, 0))],
        out_specs=pl.BlockSpec((1, tq, GROUP_WIDTH), lambda b, i: (b, i, 0)),
        out_shape=jax.ShapeDtypeStruct((batch, seq, GROUP_WIDTH), BF16),
        scratch_shapes=[pltpu.VMEM((seq, tq), I32),
                        pltpu.VMEM((N_HEADS, LANES, tq), F32),
                        pltpu.VMEM((N_HEADS, ck, tq), F32),
                        pltpu.VMEM((N_HEADS, ck, tq), F32)],
        compiler_params=pltpu.CompilerParams(
            dimension_semantics=("parallel", "arbitrary"), vmem_limit_bytes=VMEM_LIMIT),
        name="dsa",
    )(qi, misc_t, ki, qc, kc, vc_t)


def _mlp_kernel(oa_ref, ob_ref, oc_ref, od_ref, x_ref, wo_ref, g2_ref, wup_ref, wdn_ref,
                gf_ref, out_ref, *, ff_chunk, final):
    gw = GROUP_WIDTH
    mix = None
    for g, o_ref in enumerate((oa_ref, ob_ref, oc_ref, od_ref)):
        y = _dot(o_ref[...], wo_ref[g * gw:(g + 1) * gw, :])
        mix = y if mix is None else mix + y
    x1 = x_ref[...] + mix
    h2 = _rms(x1, g2_ref[...]).astype(BF16)
    ffn = None
    for c in range(wup_ref.shape[1] // ff_chunk):
        u = _dot(h2, wup_ref[:, c * ff_chunk:(c + 1) * ff_chunk])
        a = jnp.square(jnp.maximum(u, 0.0)).astype(BF16)
        y = _dot(a, wdn_ref[c * ff_chunk:(c + 1) * ff_chunk, :])
        ffn = y if ffn is None else ffn + y
    acc = x1 + ffn
    if final:
        acc = _rms(acc, gf_ref[...])
    out_ref[...] = acc


def _mlp(oa, ob, oc, od, x2d, wo, g2, wup, wdn, gf, *, tm, final):
    n, d = x2d.shape
    dff = wup.shape[1]
    row = lambda i: (i, 0)
    const = lambda i: (0, 0)
    once = pl.Buffered(1)
    o_spec = pl.BlockSpec((tm, GROUP_WIDTH), row)
    return pl.pallas_call(
        functools.partial(_mlp_kernel, ff_chunk=1024, final=final),
        grid=(n // tm,),
        in_specs=[o_spec, o_spec, o_spec, o_spec,
                  pl.BlockSpec((tm, d), row),
                  pl.BlockSpec((d, d), const, pipeline_mode=once),
                  pl.BlockSpec((1, d), const),
                  pl.BlockSpec((d, dff), const, pipeline_mode=once),
                  pl.BlockSpec((dff, d), const, pipeline_mode=once),
                  pl.BlockSpec((1, d), const)],
        out_specs=pl.BlockSpec((tm, d), row),
        out_shape=jax.ShapeDtypeStruct((n, d), F32),
        compiler_params=pltpu.CompilerParams(
            dimension_semantics=("parallel",), vmem_limit_bytes=VMEM_LIMIT),
        name="outproj_mlp",
    )(oa, ob, oc, od, x2d, wo, g2, wup, wdn, gf)


def _rope_tables(seq):
    pos = jnp.arange(seq, dtype=F32)[:, None]
    lane = jnp.arange(LANES)

    def table(dim, active):
        half = dim // 2
        inv_freq = 1.0 / (ROPE_THETA ** (jnp.arange(0, half, dtype=F32) * 2.0 / dim))
        ang = pos * inv_freq[None, :][:, lane % half]
        sign = jnp.where((lane % dim) < half, -1.0, 1.0)
        cos = jnp.where(active, jnp.cos(ang), 1.0)
        sin = jnp.where(active, jnp.sin(ang) * sign, 0.0)
        return cos.astype(F32), sin.astype(F32)

    c64, s64 = table(HEAD_DIM, jnp.ones((LANES,), bool))
    c32, s32 = table(IDX_DIM, jnp.ones((LANES,), bool))
    cm, sm = table(MLA_ROPE, (lane >= MLA_NOPE) & (lane < MLA_NOPE + MLA_ROPE))
    return c64, s64, c32, s32, cm, sm


def _pack_weights(w_in, mla_w_uq, mla_w_ukv, w_out):
    depth, d, _ = w_in.shape
    splits = (MLA_Q_RANK, MLA_KV_RANK, MLA_ROPE,
              GROUP_WIDTH, SWA_KV_HEADS * HEAD_DIM, SWA_KV_HEADS * HEAD_DIM,
              GROUP_WIDTH, GROUP_WIDTH, GROUP_WIDTH, IDX_HEADS * IDX_DIM, IDX_DIM, IDX_HEADS,
              GROUP_WIDTH, GROUP_WIDTH, GROUP_WIDTH, N_HEADS)
    offs = [0]
    for s in splits:
        offs.append(offs[-1] + s)
    (a_cq, a_ckv, a_kr, b_q, b_k, b_v, c_q, c_k, c_v, c_qi, c_ki, c_w,
     d_q, d_k, d_v, d_f) = [w_in[:, :, offs[j]:offs[j + 1]] for j in range(len(splits))]
    zeros = lambda n: jnp.zeros((depth, d, n), w_in.dtype)
    swap = jnp.array([0, 2, 1, 3])
    a_kr = jnp.concatenate([zeros(MLA_NOPE), a_kr, zeros(LANES - MLA_NOPE - MLA_ROPE)], -1)
    b_q = b_q.reshape(depth, d, N_HEADS, HEAD_DIM)[:, :, swap].reshape(depth, d, GROUP_WIDTH)
    c_ki = jnp.concatenate([c_ki] * (LANES // IDX_DIM), -1)
    w_row = jnp.concatenate(
        [a_cq, a_ckv, a_kr, b_q, b_k, b_v, c_q, c_k, c_qi, c_ki, d_q, d_k], -1).astype(BF16)
    w_t = jnp.concatenate([c_v, d_v, c_w, d_f, zeros(MISC_ROWS - IDX_HEADS - N_HEADS)], -1)
    w_t = jnp.swapaxes(w_t, 1, 2).astype(BF16)

    pad = LANES - MLA_NOPE - MLA_ROPE
    wuq = mla_w_uq.reshape(depth, MLA_Q_RANK, N_HEADS, MLA_NOPE + MLA_ROPE)
    wuq = jnp.pad(wuq, ((0, 0), (0, 0), (0, 0), (0, pad))).reshape(depth, MLA_Q_RANK, N_HEADS * LANES)
    wukv = mla_w_ukv.reshape(depth, MLA_KV_RANK, N_HEADS, MLA_NOPE + HEAD_DIM)
    wuk = jnp.pad(wukv[..., :MLA_NOPE], ((0, 0), (0, 0), (0, 0), (0, LANES - MLA_NOPE)))
    wuk = wuk.reshape(depth, MLA_KV_RANK, N_HEADS * LANES)
    wuvt = jnp.swapaxes(wukv[..., MLA_NOPE:].reshape(depth, MLA_KV_RANK, GROUP_WIDTH), 1, 2)

    wo_b = w_out[:, GROUP_WIDTH:2 * GROUP_WIDTH].reshape(depth, N_HEADS, HEAD_DIM, -1)[:, swap]
    wo = jnp.concatenate([w_out[:, :GROUP_WIDTH], wo_b.reshape(depth, GROUP_WIDTH, -1),
                          w_out[:, 2 * GROUP_WIDTH:]], 1)
    return w_row, w_t, wuq.astype(BF16), wuk.astype(BF16), wuvt.astype(BF16), wo.astype(BF16)


_MLA_HEADS = tuple((h * LANES, None, h * LANES, h // 2) for h in range(N_HEADS))
_FOX_HEADS = tuple(((h // 2) * LANES, h % 2, (h // 2) * LANES, h // 2) for h in range(N_HEADS))


def kernel(x, norm1, w_in, mla_q_norm, mla_kv_norm, mla_w_uq, mla_w_ukv, swa_sinks, fox_b_f,
           w_out, norm2, w_up, w_down, final_norm):
    batch, seq, d = x.shape
    depth = w_in.shape[0]
    n = batch * seq
    top_k = min(DSA_TOPK, seq // 4)
    tm = min(512, seq)
    tq_flash = min(256, seq)
    ck_flash = min(256, seq)
    tq_swa = min(256, seq)
    tq_dsa = min(256, seq)
    ck_dsa = min(256, seq)

    tables = _rope_tables(seq)
    w_row, w_t, wuq, wuk, wuvt, wo = _pack_weights(w_in, mla_w_uq, mla_w_ukv, w_out)
    wup = w_up.astype(BF16)
    wdn = w_down.astype(BF16)

    x2d = x.reshape(n, d)
    for l in range(depth):
        (qa, ka, qb, kb, vb, qc, kc, qi, ki, qd, kd, va_t, vc_t, vd_t, misc_t) = _inproj(
            x2d, norm1[l][None], w_row[l], w_t[l], mla_q_norm[l][None], mla_kv_norm[l][None],
            wuq[l], wuk[l], wuvt[l], tables, batch=batch, seq=seq, tm=tm)
        r3 = lambda a: a.reshape(batch, seq, a.shape[-1])
        cum = _forget_cumsum(misc_t[:, IDX_HEADS:IDX_HEADS + N_HEADS], fox_b_f[l],
                             batch=batch, seq=seq)
        cum_col = jnp.swapaxes(cum, 1, 2)
        o_a = _flash(r3(qa), r3(ka), va_t, None, _MLA_HEADS, tq=tq_flash, ck=ck_flash)
        o_b = _swa(r3(qb), r3(kb), r3(vb), swa_sinks[l], tq=tq_swa)
        o_c = _dsa(r3(qi), misc_t, r3(ki), r3(qc), r3(kc), vc_t, tq=tq_dsa, ck=ck_dsa, topk=top_k)
        o_d = _flash(r3(qd), r3(kd), vd_t, cum_col, _FOX_HEADS, tq=tq_flash, ck=ck_flash)
        x2d = _mlp(o_a.reshape(n, -1), o_b.reshape(n, -1), o_c.reshape(n, -1), o_d.reshape(n, -1),
                   x2d, wo[l], norm2[l][None], wup[l], wdn[l], final_norm[None],
                   tm=tm, final=(l == depth - 1))
    return x2d.reshape(batch, seq, d)
```

```python
import functools
import math

import jax
import jax.numpy as jnp
from jax import lax
from jax.experimental import pallas as pl
from jax.experimental.pallas import tpu as pltpu

F32 = jnp.float32
BF16 = jnp.bfloat16
I32 = jnp.int32

HEAD_DIM = 64
N_HEADS = 4
GROUP_WIDTH = N_HEADS * HEAD_DIM
MLA_Q_RANK = 256
MLA_KV_RANK = 128
MLA_NOPE = 64
MLA_ROPE = 32
SWA_KV_HEADS = 2
SWA_WINDOW = 128
IDX_HEADS = 8
IDX_DIM = 32
DSA_TOPK = 256
ROPE_THETA = 10000.0
EPS = 1e-6

LANES = 128
SUBLANES = 8
VMEM_LIMIT = 56 * 1024 * 1024
NEG = -1e30
INT_MIN = -(2 ** 31)
LOG2E = math.log2(math.e)
BISECT_STEPS = 18

_ROW_GROUPS = (
    ("a_cq", 256), ("a_ckv", 128), ("a_kr", 128),
    ("b_q", 256), ("b_k", 128), ("b_v", 128),
    ("c_q", 256), ("c_k", 256), ("c_qi", 256), ("c_ki", 128),
    ("d_q", 256), ("d_k", 256),
)
_ROW_OFF = {}
_acc = 0
for _n, _w in _ROW_GROUPS:
    _ROW_OFF[_n] = (_acc, _acc + _w)
    _acc += _w
ROW_WIDTH = _acc
MISC_ROWS = 16
T_ROWS = 2 * GROUP_WIDTH + MISC_ROWS


def _dot(a, b):
    return jnp.dot(a, b, preferred_element_type=F32)


def _dot_nt(a, b):
    return lax.dot_general(a, b, (((1,), (1,)), ((), ())), preferred_element_type=F32)


def _rms(x, g):
    return x * lax.rsqrt(jnp.mean(x * x, axis=-1, keepdims=True) + EPS) * g


def _rope(x, cos, sin, half):
    width = x.shape[1]
    reps = width // LANES
    if reps > 1:
        cos = jnp.concatenate([cos] * reps, axis=1)
        sin = jnp.concatenate([sin] * reps, axis=1)
    lane = lax.broadcasted_iota(I32, x.shape, 1)
    first = (lane % (2 * half)) < half
    rot = jnp.where(first, pltpu.roll(x, width - half, 1), pltpu.roll(x, half, 1))
    return x * cos + rot * sin


def _half_mask(q, half):
    lane = lax.broadcasted_iota(I32, q.shape, 1)
    return jnp.where((lane >= HEAD_DIM) == bool(half), q, jnp.zeros_like(q))


def _softmax_step(s, m, l8, mask):
    ck, tq = s.shape
    if mask is not None:
        s = jnp.where(mask, s, NEG)
    m_new = jnp.maximum(m, jnp.max(s, axis=0, keepdims=True))
    alpha = jnp.exp2(m - m_new)
    p = jnp.exp2(s - m_new)
    l8 = alpha * l8 + jnp.sum(p.reshape(ck // SUBLANES, SUBLANES, tq), axis=0)
    return p.astype(BF16), m_new, l8, alpha


def _inproj_kernel(x_ref, g1_ref, w_ref, wt_ref, qn_ref, kvn_ref, wuq_ref, wuk_ref, wuvt_ref,
                   c64_ref, s64_ref, c32_ref, s32_ref, cm_ref, sm_ref,
                   qa_ref, ka_ref, qb_ref, kb_ref, vb_ref,
                   qc_ref, kc_ref, qi_ref, ki_ref, qd_ref, kd_ref,
                   vat_ref, vct_ref, vdt_ref, misct_ref):
    h = _rms(x_ref[...], g1_ref[...]).astype(BF16)

    def proj(name):
        lo, hi = _ROW_OFF[name]
        return _dot(h, w_ref[:, lo:hi])

    c64, s64 = c64_ref[...], s64_ref[...]
    c32, s32 = c32_ref[...], s32_ref[...]
    cm, sm = cm_ref[...], sm_ref[...]
    scale_a = (MLA_NOPE + MLA_ROPE) ** -0.5 * LOG2E
    scale = HEAD_DIM ** -0.5 * LOG2E

    cq = _rms(proj("a_cq"), qn_ref[...]).astype(BF16)
    qa = _rope(_dot(cq, wuq_ref[...]), cm, sm, MLA_ROPE // 2) * scale_a
    qa_ref[...] = qa.astype(BF16)
    ckv = _rms(proj("a_ckv"), kvn_ref[...]).astype(BF16)
    kr = _rope(proj("a_kr"), cm, sm, MLA_ROPE // 2)
    ka_ref[...] = (_dot(ckv, wuk_ref[...]) + jnp.concatenate([kr] * N_HEADS, axis=1)).astype(BF16)
    vat_ref[0] = _dot_nt(wuvt_ref[...], ckv).astype(BF16)

    qb_ref[...] = (_rope(proj("b_q"), c64, s64, HEAD_DIM // 2) * scale).astype(BF16)
    kb_ref[...] = _rope(proj("b_k"), c64, s64, HEAD_DIM // 2).astype(BF16)
    vb_ref[...] = proj("b_v").astype(BF16)

    qc_ref[...] = (_rope(proj("c_q"), c64, s64, HEAD_DIM // 2) * scale).astype(BF16)
    kc_ref[...] = _rope(proj("c_k"), c64, s64, HEAD_DIM // 2).astype(BF16)
    qi_ref[...] = _rope(proj("c_qi"), c32, s32, IDX_DIM // 2).astype(BF16)
    ki_ref[...] = _rope(proj("c_ki"), c32, s32, IDX_DIM // 2).astype(BF16)

    qd_ref[...] = (proj("d_q") * scale).astype(BF16)
    kd_ref[...] = proj("d_k").astype(BF16)

    t = _dot_nt(wt_ref[...], h)
    vct_ref[0] = t[:GROUP_WIDTH].astype(BF16)
    vdt_ref[0] = t[GROUP_WIDTH:2 * GROUP_WIDTH].astype(BF16)
    misct_ref[0] = t[2 * GROUP_WIDTH:]


def _inproj(x2d, g1, w_row, w_t, qn, kvn, wuq, wuk, wuvt, tables, *, batch, seq, tm):
    n = x2d.shape[0]
    d = x2d.shape[1]
    spb = seq // tm
    row = lambda i: (i, 0)
    const = lambda i: (0, 0)
    tab = lambda i: (i % spb, 0)
    tr = lambda i: (i // spb, 0, i % spb)
    once = pl.Buffered(1)

    def out2(width):
        return jax.ShapeDtypeStruct((n, width), BF16), pl.BlockSpec((tm, width), row)

    def out_t(rows, dtype):
        return jax.ShapeDtypeStruct((batch, rows, seq), dtype), pl.BlockSpec((1, rows, tm), tr)

    outs = [out2(512), out2(512),
            out2(256), out2(128), out2(128),
            out2(256), out2(256), out2(256), out2(128),
            out2(256), out2(256),
            out_t(GROUP_WIDTH, BF16), out_t(GROUP_WIDTH, BF16), out_t(GROUP_WIDTH, BF16),
            out_t(MISC_ROWS, F32)]
    in_specs = [
        pl.BlockSpec((tm, d), row),
        pl.BlockSpec((1, d), const),
        pl.BlockSpec((d, ROW_WIDTH), const, pipeline_mode=once),
        pl.BlockSpec((T_ROWS, d), const, pipeline_mode=once),
        pl.BlockSpec((1, MLA_Q_RANK), const),
        pl.BlockSpec((1, MLA_KV_RANK), const),
        pl.BlockSpec((MLA_Q_RANK, N_HEADS * LANES), const, pipeline_mode=once),
        pl.BlockSpec((MLA_KV_RANK, N_HEADS * LANES), const, pipeline_mode=once),
        pl.BlockSpec((GROUP_WIDTH, MLA_KV_RANK), const, pipeline_mode=once),
    ] + [pl.BlockSpec((tm, LANES), tab)] * 6
    return pl.pallas_call(
        _inproj_kernel,
        grid=(n // tm,),
        in_specs=in_specs,
        out_specs=[o[1] for o in outs],
        out_shape=[o[0] for o in outs],
        compiler_params=pltpu.CompilerParams(
            dimension_semantics=("parallel",), vmem_limit_bytes=VMEM_LIMIT),
        name="inproj",
    )(x2d, g1, w_row, w_t, qn, kvn, wuq, wuk, wuvt, *tables)


def _cumsum_kernel(f_ref, b_ref, o_ref, *, segs):
    x = f_ref[...] + b_ref[...]
    ls = jnp.minimum(x, 0.0) - jnp.log1p(jnp.exp(-jnp.abs(x)))
    rows = x.shape[0]
    r = lax.broadcasted_iota(I32, (LANES, LANES), 0)
    c = lax.broadcasted_iota(I32, (LANES, LANES), 1)
    upper = (r <= c).astype(F32)
    within = jnp.dot(ls, upper, preferred_element_type=F32, precision=lax.Precision.HIGHEST)
    tot = jnp.broadcast_to(within[:, LANES - 1:LANES], (rows, LANES))
    rr = lax.broadcasted_iota(I32, (rows, rows), 0)
    cc = lax.broadcasted_iota(I32, (rows, rows), 1)
    before = ((cc < rr) & (cc // segs == rr // segs)).astype(F32)
    off = jnp.dot(before, tot, preferred_element_type=F32, precision=lax.Precision.HIGHEST)
    o_ref[...] = (within + off) * LOG2E


def _forget_cumsum(f_t, bias, *, batch, seq):
    segs = seq // LANES
    rows = N_HEADS * segs
    f2 = f_t.reshape(batch * rows, LANES)
    b2 = jnp.broadcast_to(jnp.repeat(bias, segs)[:, None], (rows, LANES))
    out = pl.pallas_call(
        functools.partial(_cumsum_kernel, segs=segs),
        grid=(batch,),
        in_specs=[pl.BlockSpec((rows, LANES), lambda b: (b, 0)),
                  pl.BlockSpec((rows, LANES), lambda b: (0, 0))],
        out_specs=pl.BlockSpec((rows, LANES), lambda b: (b, 0)),
        out_shape=jax.ShapeDtypeStruct((batch * rows, LANES), F32),
        compiler_params=pltpu.CompilerParams(dimension_semantics=("parallel",)),
        name="forget_cumsum",
    )(f2, b2)
    return out.reshape(batch, N_HEADS, seq)


def _flash_kernel(*refs, heads, tq, ck, has_bias):
    if has_bias:
        q_ref, k_ref, vt_ref, b_ref, o_ref, acc_ref, sa_ref, sb_ref = refs
    else:
        q_ref, k_ref, vt_ref, o_ref, acc_ref, sa_ref, sb_ref = refs
        b_ref = None
    assert tq == ck
    nh = len(heads)
    i = pl.program_id(1)
    q_start = i * tq
    n_full = q_start // ck
    kpos_l = lax.broadcasted_iota(I32, (ck, tq), 0)
    qpos = q_start + lax.broadcasted_iota(I32, (ck, tq), 1)
    qs = []
    for ql, qhalf, _, _ in heads:
        q = q_ref[0, :, ql:ql + LANES]
        qs.append(q if qhalf is None else _half_mask(q, qhalf))
    acc_ref[...] = jnp.zeros(acc_ref.shape, F32)

    def scores(c, s_ref):
        off = pl.multiple_of(c * ck, ck)
        for h, (_, _, kl, _) in enumerate(heads):
            s = _dot_nt(k_ref[0, pl.ds(off, ck), kl:kl + LANES], qs[h])
            if has_bias:
                s = s - b_ref[0, pl.ds(off, ck), h:h + 1]
            s_ref[h] = s

    def consume(c, s_ref, carry, masked):
        ms, ls = carry
        off = pl.multiple_of(c * ck, ck)
        mask = (off + kpos_l <= qpos) if masked else None
        new_ms, new_ls = [], []
        for h, (_, _, _, pair) in enumerate(heads):
            p, m_new, l8, alpha = _softmax_step(s_ref[h], ms[h], ls[h], mask)
            vt = vt_ref[0, pair * LANES:(pair + 1) * LANES, pl.ds(off, ck)]
            acc_ref[h] = alpha * acc_ref[h] + _dot(vt, p)
            new_ms.append(m_new)
            new_ls.append(l8)
        return tuple(new_ms), tuple(new_ls)

    def pair_body(p, carry):
        c = 2 * p
        scores(c + 1, sb_ref)
        carry = consume(c, sa_ref, carry, False)
        scores(c + 2, sa_ref)
        return consume(c + 1, sb_ref, carry, False)

    carry = (tuple(jnp.full((1, tq), NEG, F32) for _ in range(nh)),
             tuple(jnp.zeros((SUBLANES, tq), F32) for _ in range(nh)))
    scores(0, sa_ref)
    carry = lax.fori_loop(0, n_full // 2, pair_body, carry)
    odd = n_full % 2 == 1

    def tail_odd(carry):
        scores(n_full, sb_ref)
        carry = consume(n_full - 1, sa_ref, carry, False)
        return consume(n_full, sb_ref, carry, True)

    carry = lax.cond(odd, tail_odd, lambda cr: consume(n_full, sa_ref, cr, True), carry)
    _, ls = carry
    sub = lax.broadcasted_iota(I32, (LANES, tq), 0)
    for pair in range(nh // 2):
        o0 = acc_ref[2 * pair] / jnp.sum(ls[2 * pair], axis=0, keepdims=True)
        o1 = acc_ref[2 * pair + 1] / jnp.sum(ls[2 * pair + 1], axis=0, keepdims=True)
        o_t = jnp.where(sub < HEAD_DIM, o0, o1)
        o_ref[0, :, pair * LANES:(pair + 1) * LANES] = o_t.T.astype(o_ref.dtype)


def _flash(q, k, v_t, bias, heads, *, tq, ck):
    batch, seq, wq = q.shape
    wk = k.shape[2]
    has_bias = bias is not None
    in_specs = [pl.BlockSpec((1, tq, wq), lambda b, i: (b, i, 0)),
                pl.BlockSpec((1, seq, wk), lambda b, i: (b, 0, 0)),
                pl.BlockSpec((1, GROUP_WIDTH, seq), lambda b, i: (b, 0, 0))]
    args = [q, k, v_t]
    if has_bias:
        in_specs.append(pl.BlockSpec((1, seq, N_HEADS), lambda b, i: (b, 0, 0)))
        args.append(bias)
    return pl.pallas_call(
        functools.partial(_flash_kernel, heads=heads, tq=tq, ck=ck, has_bias=has_bias),
        grid=(batch, seq // tq),
        in_specs=in_specs,
        out_specs=pl.BlockSpec((1, tq, GROUP_WIDTH), lambda b, i: (b, i, 0)),
        out_shape=jax.ShapeDtypeStruct((batch, seq, GROUP_WIDTH), BF16),
        scratch_shapes=[pltpu.VMEM((len(heads), LANES, tq), F32),
                        pltpu.VMEM((len(heads), ck, tq), F32),
                        pltpu.VMEM((len(heads), ck, tq), F32)],
        compiler_params=pltpu.CompilerParams(
            dimension_semantics=("parallel", "arbitrary"), vmem_limit_bytes=VMEM_LIMIT),
        name="flash_bias" if has_bias else "flash",
    )(*args)


def _swa_kernel(sink_ref, q_ref, k_ref, v_ref, o_ref, *, tq, window):
    i = pl.program_id(1)
    q_start = i * tq
    span = tq + window
    start = pl.multiple_of(jnp.maximum(q_start - window, 0), LANES)
    k = k_ref[0, pl.ds(start, span), :]
    v = v_ref[0, pl.ds(start, span), :]
    lane = lax.broadcasted_iota(I32, (tq, LANES), 1)
    qpos = q_start + lax.broadcasted_iota(I32, (tq, span), 0)
    kpos = start + lax.broadcasted_iota(I32, (tq, span), 1)
    valid = (kpos <= qpos) & (qpos - kpos < window)
    for pair in range(2):
        qp = q_ref[0, :, pair * LANES:(pair + 1) * LANES]
        outs = []
        for half in range(2):
            head = pair + 2 * half
            sink = sink_ref[head] * LOG2E
            s = jnp.where(valid, _dot_nt(_half_mask(qp, half), k), NEG)
            m = jnp.maximum(jnp.max(s, axis=1, keepdims=True), sink)
            p = jnp.exp2(s - m)
            denom = jnp.sum(p, axis=1, keepdims=True) + jnp.exp2(sink - m)
            outs.append(_dot(p.astype(BF16), v) / denom)
        o_ref[0, :, pair * LANES:(pair + 1) * LANES] = jnp.where(
            lane < HEAD_DIM, outs[0], outs[1]).astype(o_ref.dtype)


def _swa(q, k, v, sinks, *, tq):
    batch, seq, _ = q.shape
    return pl.pallas_call(
        functools.partial(_swa_kernel, tq=tq, window=SWA_WINDOW),
        grid=(batch, seq // tq),
        in_specs=[pl.BlockSpec(memory_space=pltpu.SMEM),
                  pl.BlockSpec((1, tq, GROUP_WIDTH), lambda b, i: (b, i, 0)),
                  pl.BlockSpec((1, seq, LANES), lambda b, i: (b, 0, 0)),
                  pl.BlockSpec((1, seq, LANES), lambda b, i: (b, 0, 0))],
        out_specs=pl.BlockSpec((1, tq, GROUP_WIDTH), lambda b, i: (b, i, 0)),
        out_shape=jax.ShapeDtypeStruct((batch, seq, GROUP_WIDTH), BF16),
        compiler_params=pltpu.CompilerParams(
            dimension_semantics=("parallel", "arbitrary"), vmem_limit_bytes=VMEM_LIMIT),
        name="swa",
    )(sinks, q, k, v)


def _dsa_kernel(qi_ref, w_ref, ki_ref, qc_ref, kc_ref, vct_ref, o_ref,
                keys_ref, acc_ref, sa_ref, sb_ref, *, tq, ck, topk, seq):
    assert tq == ck
    i = pl.program_id(1)
    q_start = i * tq
    n_chunks = (q_start + tq) // ck
    n_full = q_start // ck
    lane = lax.broadcasted_iota(I32, (tq, LANES), 1)
    kpos_l = lax.broadcasted_iota(I32, (ck, tq), 0)
    qpos = q_start + lax.broadcasted_iota(I32, (ck, tq), 1)
    qpos_row = q_start + lax.broadcasted_iota(I32, (1, tq), 1)
    msb = jnp.int32(INT_MIN)

    qm = []
    for h in range(IDX_HEADS):
        g, r = divmod(h, LANES // IDX_DIM)
        qg = qi_ref[0, :, g * LANES:(g + 1) * LANES]
        qm.append(jnp.where(lane // IDX_DIM == r, qg, jnp.zeros_like(qg)))
    w = w_ref[0]

    def flip(v):
        return v ^ ((v >> 31) & jnp.int32(0x7FFFFFFF))

    def group_reduce(fn, v):
        quarter = v.shape[0] // 4
        parts = [fn(v[j * quarter:(j + 1) * quarter].reshape(quarter // SUBLANES, SUBLANES, tq), axis=0)
                 for j in range(4)]
        return fn(jnp.stack([fn(jnp.stack(parts[:2]), axis=0), fn(jnp.stack(parts[2:]), axis=0)]),
                  axis=0)

    half_heads = IDX_HEADS // 2

    def index_dots(c, first, s_ref):
        ki = ki_ref[0, pl.ds(pl.multiple_of(c * ck, ck), ck), :]
        for j in range(half_heads):
            s_ref[j] = _dot_nt(ki, qm[first + j])

    def weighted_relu(first, s_ref):
        acc = None
        for j in range(half_heads):
            term = jnp.maximum(s_ref[j], 0.0) * w[first + j:first + j + 1, :]
            acc = term if acc is None else acc + term
        return acc

    def score_chunk(c, carry, last):
        kmax8, kmin8 = carry
        off = pl.multiple_of(c * ck, ck)
        index_dots(c, half_heads, sb_ref)
        acc = weighted_relu(0, sa_ref)
        if not last:
            index_dots(c + 1, 0, sa_ref)
        acc = acc + weighted_relu(half_heads, sb_ref)
        key = flip(lax.bitcast_convert_type(acc, I32))
        key_lo = key
        if last:
            causal = off + kpos_l <= qpos
            key_lo = jnp.where(causal, key, jnp.int32(2 ** 31 - 1))
            key = jnp.where(causal, key, msb)
        keys_ref[pl.ds(off, ck), :] = key
        return (jnp.maximum(kmax8, group_reduce(jnp.max, key)),
                jnp.minimum(kmin8, group_reduce(jnp.min, key_lo)))

    ext = (jnp.full((SUBLANES, tq), INT_MIN, I32), jnp.full((SUBLANES, tq), 2 ** 31 - 1, I32))
    index_dots(0, 0, sa_ref)
    ext = lax.fori_loop(0, n_full, lambda c, cr: score_chunk(c, cr, False), ext)
    ext = score_chunk(n_full, ext, True)
    kmax = jnp.max(ext[0], axis=0, keepdims=True)
    kmin = jnp.min(ext[1], axis=0, keepdims=True)

    cb = 2 * ck
    n_blocks = (n_chunks + 1) // 2
    kpos_b = lax.broadcasted_iota(I32, (cb, tq), 0)

    @pl.when(n_chunks % 2 == 1)
    def _():
        keys_ref[pl.ds(pl.multiple_of(n_chunks * ck, ck), ck), :] = jnp.full((ck, tq), INT_MIN, I32)

    def key_blocks(fn, init):
        def body(b, carry):
            off = pl.multiple_of(b * cb, cb)
            return fn(keys_ref[pl.ds(off, cb), :], off, carry)
        return lax.fori_loop(0, n_blocks, body, init)

    def count(pred):
        part = key_blocks(
            lambda kk, off, part: part + group_reduce(jnp.sum, jnp.where(pred(kk, off), 1, 0)),
            jnp.zeros((SUBLANES, tq), I32))
        return jnp.sum(part, axis=0, keepdims=True)

    need = qpos_row >= topk

    def settle(state, cand_k, c, lo_k, hi_k):
        done, tie, t_res = state
        hit = c == topk
        fin = (done == 0) & (hit | (hi_k - lo_k == 1))
        t_res = jnp.where(fin, jnp.where(hit, cand_k, lo_k), t_res)
        tie = jnp.where(fin & ~hit, 1, tie)
        return jnp.where(fin, 1, done), tie, t_res

    def bisect_step(state):
        it, flags, lo_k, hi_k = state
        cand_k = lo_k + lax.shift_right_logical(hi_k - lo_k, 1)
        c = count(lambda kk, off: kk >= cand_k)
        ge = c >= topk
        lo_k = jnp.where(ge, cand_k, lo_k)
        hi_k = jnp.where(ge, hi_k, cand_k)
        return it + 1, settle(flags, cand_k, c, lo_k, hi_k), lo_k, hi_k

    def snap_step(state):
        it, flags, lo_k, hi_k = state

        top8 = key_blocks(
            lambda kk, off, part: jnp.maximum(
                part, group_reduce(jnp.max, jnp.where(kk < hi_k, kk, msb))),
            jnp.full((SUBLANES, tq), INT_MIN, I32))
        cand_k = jnp.maximum(jnp.max(top8, axis=0, keepdims=True), lo_k)
        c = count(lambda kk, off: kk >= cand_k)
        ge = c >= topk
        lo_k = jnp.where(ge, cand_k, lo_k)
        hi_k = jnp.where(ge, cand_k + 1, cand_k)
        return it + 1, settle(flags, cand_k, c, lo_k, hi_k), lo_k, hi_k

    def unsettled(state):
        return jnp.min(state[1][0]) == 0

    lo0, hi0 = kmin, kmax + 1
    flat0 = need & (hi0 - lo0 == 1)
    flags = (jnp.where(need & ~flat0, 0, 1), jnp.where(flat0, 1, 0), jnp.where(need, lo0, msb + 1))
    state = (jnp.int32(0), flags, lo0, hi0)
    state = lax.fori_loop(0, BISECT_STEPS, lambda _, st: bisect_step(st), state)
    state = lax.while_loop(unsettled, lambda st: bisect_step(snap_step(st)), state)
    _, (_, tie_flag, t_s), _, _ = state
    tie = tie_flag > 0

    @pl.when(jnp.max(jnp.where(tie, 1, 0)) > 0)
    def _():
        n_gt = count(lambda kk, off: kk > t_s)
        want = topk - n_gt

        def idx_body(state):
            jdone, j_res, lo_j, hi_j = state
            mid = (lo_j + hi_j) >> 1
            c = count(lambda kk, off: (kk == t_s) & (off + kpos_b <= mid))
            lo_j = jnp.where(c < want, mid, lo_j)
            hi_j = jnp.where(c < want, hi_j, mid)
            fin = (jdone == 0) & ((c == want) | (hi_j - lo_j == 1))
            j_res = jnp.where(fin, jnp.where(c == want, mid, hi_j), j_res)
            return jnp.where(fin, 1, jdone), j_res, lo_j, hi_j

        init_j = (jnp.where(tie, 0, 1), jnp.zeros((1, tq), I32),
                  jnp.full((1, tq), -1, I32), jnp.full((1, tq), seq - 1, I32))
        _, j_keep, _, _ = lax.while_loop(lambda st: jnp.min(st[0]) == 0, idx_body, init_j)

        def demote(kk, off, carry):
            drop = tie & (kk == t_s) & (off + kpos_b > j_keep)
            keys_ref[pl.ds(off, cb), :] = jnp.where(drop, msb, kk)
            return carry

        key_blocks(demote, 0)

    t_fin = jnp.maximum(t_s, msb + 1)
    qs = []
    for h in range(N_HEADS):
        pair, half = divmod(h, 2)
        qs.append(_half_mask(qc_ref[0, :, pair * LANES:(pair + 1) * LANES], half))
    acc_ref[...] = jnp.zeros(acc_ref.shape, F32)

    def scores(c, s_ref):
        off = pl.multiple_of(c * ck, ck)
        for h in range(N_HEADS):
            k = kc_ref[0, pl.ds(off, ck), (h // 2) * LANES:(h // 2 + 1) * LANES]
            s_ref[h] = _dot_nt(k, qs[h])

    def consume(c, s_ref, carry):
        ms, ls = carry
        off = pl.multiple_of(c * ck, ck)
        sel = keys_ref[pl.ds(off, ck), :] >= t_fin
        new_ms, new_ls = [], []
        for h in range(N_HEADS):
            pair = h // 2
            p, m_new, l8, alpha = _softmax_step(s_ref[h], ms[h], ls[h], sel)
            vt = vct_ref[0, pair * LANES:(pair + 1) * LANES, pl.ds(off, ck)]
            acc_ref[h] = alpha * acc_ref[h] + _dot(vt, p)
            new_ms.append(m_new)
            new_ls.append(l8)
        return tuple(new_ms), tuple(new_ls)

    def pair_body(p, carry):
        c = 2 * p
        scores(c + 1, sb_ref)
        carry = consume(c, sa_ref, carry)
        scores(c + 2, sa_ref)
        return consume(c + 1, sb_ref, carry)

    carry = (tuple(jnp.full((1, tq), NEG, F32) for _ in range(N_HEADS)),
             tuple(jnp.zeros((SUBLANES, tq), F32) for _ in range(N_HEADS)))
    scores(0, sa_ref)
    carry = lax.fori_loop(0, n_full // 2, pair_body, carry)

    def tail_odd(carry):
        scores(n_full, sb_ref)
        return consume(n_full, sb_ref, consume(n_full - 1, sa_ref, carry))

    _, ls = lax.cond(n_full % 2 == 1, tail_odd, lambda cr: consume(n_full, sa_ref, cr), carry)
    sub = lax.broadcasted_iota(I32, (LANES, tq), 0)
    for pair in range(2):
        o0 = acc_ref[2 * pair] / jnp.sum(ls[2 * pair], axis=0, keepdims=True)
        o1 = acc_ref[2 * pair + 1] / jnp.sum(ls[2 * pair + 1], axis=0, keepdims=True)
        o_t = jnp.where(sub < HEAD_DIM, o0, o1)
        o_ref[0, :, pair * LANES:(pair + 1) * LANES] = o_t.T.astype(o_ref.dtype)


def _dsa(qi, misc_t, ki, qc, kc, vc_t, *, tq, ck, topk):
    batch, seq, _ = qc.shape
    return pl.pallas_call(
        functools.partial(_dsa_kernel, tq=tq, ck=ck, topk=topk, seq=seq),
        grid=(batch, seq // tq),
        in_specs=[pl.BlockSpec((1, tq, IDX_HEADS * IDX_DIM), lambda b, i: (b, i, 0)),
                  pl.BlockSpec((1, MISC_ROWS, tq), lambda b, i: (b, 0, i)),
                  pl.BlockSpec((1, seq, LANES), lambda b, i: (b, 0, 0)),
                  pl.BlockSpec((1, tq, GROUP_WIDTH), lambda b, i: (b, i, 0)),
                  pl.BlockSpec((1, seq, GROUP_WIDTH), lambda b, i: (b, 0, 0)),
                  pl.BlockSpec((1, GROUP_WIDTH, seq), lambda b, i: (b, 0, 0))],
        out_specs=pl.BlockSpec((1, tq, GROUP_WIDTH), lambda b, i: (b, i, 0)),
        out_shape=jax.ShapeDtypeStruct((batch, seq, GROUP_WIDTH), BF16),
        scratch_shapes=[pltpu.VMEM((seq, tq), I32),
                        pltpu.VMEM((N_HEADS, LANES, tq), F32),
                        pltpu.VMEM((N_HEADS, ck, tq), F32),
                        pltpu.VMEM((N_HEADS, ck, tq), F32)],
        compiler_params=pltpu.CompilerParams(
            dimension_semantics=("parallel", "arbitrary"), vmem_limit_bytes=VMEM_LIMIT),
        name="dsa",
    )(qi, misc_t, ki, qc, kc, vc_t)


def _mlp_kernel(oa_ref, ob_ref, oc_ref, od_ref, x_ref, wo_ref, g2_ref, wup_ref, wdn_ref,
                gf_ref, out_ref, *, ff_chunk, final):
    gw = GROUP_WIDTH
    mix = None
    for g, o_ref in enumerate((oa_ref, ob_ref, oc_ref, od_ref)):
        y = _dot(o_ref[...], wo_ref[g * gw:(g + 1) * gw, :])
        mix = y if mix is None else mix + y
    x1 = x_ref[...] + mix
    h2 = _rms(x1, g2_ref[...]).astype(BF16)
    ffn = None
    for c in range(wup_ref.shape[1] // ff_chunk):
        u = _dot(h2, wup_ref[:, c * ff_chunk:(c + 1) * ff_chunk])
        a = jnp.square(jnp.maximum(u, 0.0)).astype(BF16)
        y = _dot(a, wdn_ref[c * ff_chunk:(c + 1) * ff_chunk, :])
        ffn = y if ffn is None else ffn + y
    acc = x1 + ffn
    if final:
        acc = _rms(acc, gf_ref[...])
    out_ref[...] = acc


def _mlp(oa, ob, oc, od, x2d, wo, g2, wup, wdn, gf, *, tm, final):
    n, d = x2d.shape
    dff = wup.shape[1]
    row = lambda i: (i, 0)
    const = lambda i: (0, 0)
    once = pl.Buffered(1)
    o_spec = pl.BlockSpec((tm, GROUP_WIDTH), row)
    return pl.pallas_call(
        functools.partial(_mlp_kernel, ff_chunk=1024, final=final),
        grid=(n // tm,),
        in_specs=[o_spec, o_spec, o_spec, o_spec,
                  pl.BlockSpec((tm, d), row),
                  pl.BlockSpec((d, d), const, pipeline_mode=once),
                  pl.BlockSpec((1, d), const),
                  pl.BlockSpec((d, dff), const, pipeline_mode=once),
                  pl.BlockSpec((dff, d), const, pipeline_mode=once),
                  pl.BlockSpec((1, d), const)],
        out_specs=pl.BlockSpec((tm, d), row),
        out_shape=jax.ShapeDtypeStruct((n, d), F32),
        compiler_params=pltpu.CompilerParams(
            dimension_semantics=("parallel",), vmem_limit_bytes=VMEM_LIMIT),
        name="outproj_mlp",
    )(oa, ob, oc, od, x2d, wo, g2, wup, wdn, gf)


def _rope_tables(seq):
    pos = jnp.arange(seq, dtype=F32)[:, None]
    lane = jnp.arange(LANES)

    def table(dim, active):
        half = dim // 2
        inv_freq = 1.0 / (ROPE_THETA ** (jnp.arange(0, half, dtype=F32) * 2.0 / dim))
        ang = pos * inv_freq[None, :][:, lane % half]
        sign = jnp.where((lane % dim) < half, -1.0, 1.0)
        cos = jnp.where(active, jnp.cos(ang), 1.0)
        sin = jnp.where(active, jnp.sin(ang) * sign, 0.0)
        return cos.astype(F32), sin.astype(F32)

    c64, s64 = table(HEAD_DIM, jnp.ones((LANES,), bool))
    c32, s32 = table(IDX_DIM, jnp.ones((LANES,), bool))
    cm, sm = table(MLA_ROPE, (lane >= MLA_NOPE) & (lane < MLA_NOPE + MLA_ROPE))
    return c64, s64, c32, s32, cm, sm


def _pack_weights(w_in, mla_w_uq, mla_w_ukv, w_out):
    depth, d, _ = w_in.shape
    splits = (MLA_Q_RANK, MLA_KV_RANK, MLA_ROPE,
              GROUP_WIDTH, SWA_KV_HEADS * HEAD_DIM, SWA_KV_HEADS * HEAD_DIM,
              GROUP_WIDTH, GROUP_WIDTH, GROUP_WIDTH, IDX_HEADS * IDX_DIM, IDX_DIM, IDX_HEADS,
              GROUP_WIDTH, GROUP_WIDTH, GROUP_WIDTH, N_HEADS)
    offs = [0]
    for s in splits:
        offs.append(offs[-1] + s)
    (a_cq, a_ckv, a_kr, b_q, b_k, b_v, c_q, c_k, c_v, c_qi, c_ki, c_w,
     d_q, d_k, d_v, d_f) = [w_in[:, :, offs[j]:offs[j + 1]] for j in range(len(splits))]
    zeros = lambda n: jnp.zeros((depth, d, n), w_in.dtype)
    swap = jnp.array([0, 2, 1, 3])
    a_kr = jnp.concatenate([zeros(MLA_NOPE), a_kr, zeros(LANES - MLA_NOPE - MLA_ROPE)], -1)
    b_q = b_q.reshape(depth, d, N_HEADS, HEAD_DIM)[:, :, swap].reshape(depth, d, GROUP_WIDTH)
    c_ki = jnp.concatenate([c_ki] * (LANES // IDX_DIM), -1)
    w_row = jnp.concatenate(
        [a_cq, a_ckv, a_kr, b_q, b_k, b_v, c_q, c_k, c_qi, c_ki, d_q, d_k], -1).astype(BF16)
    w_t = jnp.concatenate([c_v, d_v, c_w, d_f, zeros(MISC_ROWS - IDX_HEADS - N_HEADS)], -1)
    w_t = jnp.swapaxes(w_t, 1, 2).astype(BF16)

    pad = LANES - MLA_NOPE - MLA_ROPE
    wuq = mla_w_uq.reshape(depth, MLA_Q_RANK, N_HEADS, MLA_NOPE + MLA_ROPE)
    wuq = jnp.pad(wuq, ((0, 0), (0, 0), (0, 0), (0, pad))).reshape(depth, MLA_Q_RANK, N_HEADS * LANES)
    wukv = mla_w_ukv.reshape(depth, MLA_KV_RANK, N_HEADS, MLA_NOPE + HEAD_DIM)
    wuk = jnp.pad(wukv[..., :MLA_NOPE], ((0, 0), (0, 0), (0, 0), (0, LANES - MLA_NOPE)))
    wuk = wuk.reshape(depth, MLA_KV_RANK, N_HEADS * LANES)
    wuvt = jnp.swapaxes(wukv[..., MLA_NOPE:].reshape(depth, MLA_KV_RANK, GROUP_WIDTH), 1, 2)

    wo_b = w_out[:, GROUP_WIDTH:2 * GROUP_WIDTH].reshape(depth, N_HEADS, HEAD_DIM, -1)[:, swap]
    wo = jnp.concatenate([w_out[:, :GROUP_WIDTH], wo_b.reshape(depth, GROUP_WIDTH, -1),
                          w_out[:, 2 * GROUP_WIDTH:]], 1)
    return w_row, w_t, wuq.astype(BF16), wuk.astype(BF16), wuvt.astype(BF16), wo.astype(BF16)


_MLA_HEADS = tuple((h * LANES, None, h * LANES, h // 2) for h in range(N_HEADS))
_FOX_HEADS = tuple(((h // 2) * LANES, h % 2, (h // 2) * LANES, h // 2) for h in range(N_HEADS))


def kernel(x, norm1, w_in, mla_q_norm, mla_kv_norm, mla_w_uq, mla_w_ukv, swa_sinks, fox_b_f,
           w_out, norm2, w_up, w_down, final_norm):
    batch, seq, d = x.shape
    depth = w_in.shape[0]
    n = batch * seq
    top_k = min(DSA_TOPK, seq // 4)
    tm = min(512, seq)
    tq_flash = min(256, seq)
    ck_flash = min(256, seq)
    tq_swa = min(256, seq)
    tq_dsa = min(256, seq)
    ck_dsa = min(256, seq)

    tables = _rope_tables(seq)
    w_row, w_t, wuq, wuk, wuvt, wo = _pack_weights(w_in, mla_w_uq, mla_w_ukv, w_out)
    wup = w_up.astype(BF16)
    wdn = w_down.astype(BF16)

    x2d = x.reshape(n, d)
    for l in range(depth):
        (qa, ka, qb, kb, vb, qc, kc, qi, ki, qd, kd, va_t, vc_t, vd_t, misc_t) = _inproj(
            x2d, norm1[l][None], w_row[l], w_t[l], mla_q_norm[l][None], mla_kv_norm[l][None],
            wuq[l], wuk[l], wuvt[l], tables, batch=batch, seq=seq, tm=tm)
        r3 = lambda a: a.reshape(batch, seq, a.shape[-1])
        cum = _forget_cumsum(misc_t[:, IDX_HEADS:IDX_HEADS + N_HEADS], fox_b_f[l],
                             batch=batch, seq=seq)
        cum_col = jnp.swapaxes(cum, 1, 2)
        o_a = _flash(r3(qa), r3(ka), va_t, None, _MLA_HEADS, tq=tq_flash, ck=ck_flash)
        o_b = _swa(r3(qb), r3(kb), r3(vb), swa_sinks[l], tq=tq_swa)
        o_c = _dsa(r3(qi), misc_t, r3(ki), r3(qc), r3(kc), vc_t, tq=tq_dsa, ck=ck_dsa, topk=top_k)
        o_d = _flash(r3(qd), r3(kd), vd_t, cum_col, _FOX_HEADS, tq=tq_flash, ck=ck_flash)
        x2d = _mlp(o_a.reshape(n, -1), o_b.reshape(n, -1), o_c.reshape(n, -1), o_d.reshape(n, -1),
                   x2d, wo[l], norm2[l][None], wup[l], wdn[l], final_norm[None],
                   tm=tm, final=(l == depth - 1))
    return x2d.reshape(batch, seq, d)
```

```python
import functools
import math

import jax
import jax.numpy as jnp
from jax import lax
from jax.experimental import pallas as pl
from jax.experimental.pallas import tpu as pltpu

F32 = jnp.float32
BF16 = jnp.bfloat16
I32 = jnp.int32

HEAD_DIM = 64
N_HEADS = 4
GROUP_WIDTH = N_HEADS * HEAD_DIM
MLA_Q_RANK = 256
MLA_KV_RANK = 128
MLA_NOPE = 64
MLA_ROPE = 32
SWA_KV_HEADS = 2
SWA_WINDOW = 128
IDX_HEADS = 8
IDX_DIM = 32
DSA_TOPK = 256
ROPE_THETA = 10000.0
EPS = 1e-6

LANES = 128
SUBLANES = 8
VMEM_LIMIT = 56 * 1024 * 1024
NEG = -1e30
INT_MIN = -(2 ** 31)
LOG2E = math.log2(math.e)
BISECT_STEPS = 18

_ROW_GROUPS = (
    ("a_cq", 256), ("a_ckv", 128), ("a_kr", 128),
    ("b_q", 256), ("b_k", 128), ("b_v", 128),
    ("c_q", 256), ("c_k", 256), ("c_qi", 256), ("c_ki", 128),
    ("d_q", 256), ("d_k", 256),
)
_ROW_OFF = {}
_acc = 0
for _n, _w in _ROW_GROUPS:
    _ROW_OFF[_n] = (_acc, _acc + _w)
    _acc += _w
ROW_WIDTH = _acc
MISC_ROWS = 16
T_ROWS = 2 * GROUP_WIDTH + MISC_ROWS


def _dot(a, b):
    return jnp.dot(a, b, preferred_element_type=F32)


def _dot_nt(a, b):
    return lax.dot_general(a, b, (((1,), (1,)), ((), ())), preferred_element_type=F32)


def _rms(x, g):
    return x * lax.rsqrt(jnp.mean(x * x, axis=-1, keepdims=True) + EPS) * g


def _rope(x, cos, sin, half):
    width = x.shape[1]
    reps = width // LANES
    if reps > 1:
        cos = jnp.concatenate([cos] * reps, axis=1)
        sin = jnp.concatenate([sin] * reps, axis=1)
    lane = lax.broadcasted_iota(I32, x.shape, 1)
    first = (lane % (2 * half)) < half
    rot = jnp.where(first, pltpu.roll(x, width - half, 1), pltpu.roll(x, half, 1))
    return x * cos + rot * sin


def _half_mask(q, half):
    lane = lax.broadcasted_iota(I32, q.shape, 1)
    return jnp.where((lane >= HEAD_DIM) == bool(half), q, jnp.zeros_like(q))


def _softmax_step(s, m, l8, mask):
    ck, tq = s.shape
    if mask is not None:
        s = jnp.where(mask, s, NEG)
    m_new = jnp.maximum(m, jnp.max(s, axis=0, keepdims=True))
    alpha = jnp.exp2(m - m_new)
    p = jnp.exp2(s - m_new)
    l8 = alpha * l8 + jnp.sum(p.reshape(ck // SUBLANES, SUBLANES, tq), axis=0)
    return p.astype(BF16), m_new, l8, alpha


def _inproj_kernel(x_ref, g1_ref, w_ref, wt_ref, qn_ref, kvn_ref, wuq_ref, wuk_ref, wuvt_ref,
                   c64_ref, s64_ref, c32_ref, s32_ref, cm_ref, sm_ref,
                   qa_ref, ka_ref, qb_ref, kb_ref, vb_ref,
                   qc_ref, kc_ref, qi_ref, ki_ref, qd_ref, kd_ref,
                   vat_ref, vct_ref, vdt_ref, misct_ref):
    h = _rms(x_ref[...], g1_ref[...]).astype(BF16)

    def proj(name):
        lo, hi = _ROW_OFF[name]
        return _dot(h, w_ref[:, lo:hi])

    c64, s64 = c64_ref[...], s64_ref[...]
    c32, s32 = c32_ref[...], s32_ref[...]
    cm, sm = cm_ref[...], sm_ref[...]
    scale_a = (MLA_NOPE + MLA_ROPE) ** -0.5 * LOG2E
    scale = HEAD_DIM ** -0.5 * LOG2E

    cq = _rms(proj("a_cq"), qn_ref[...]).astype(BF16)
    qa = _rope(_dot(cq, wuq_ref[...]), cm, sm, MLA_ROPE // 2) * scale_a
    qa_ref[...] = qa.astype(BF16)
    ckv = _rms(proj("a_ckv"), kvn_ref[...]).astype(BF16)
    kr = _rope(proj("a_kr"), cm, sm, MLA_ROPE // 2)
    ka_ref[...] = (_dot(ckv, wuk_ref[...]) + jnp.concatenate([kr] * N_HEADS, axis=1)).astype(BF16)
    vat_ref[0] = _dot_nt(wuvt_ref[...], ckv).astype(BF16)

    qb_ref[...] = (_rope(proj("b_q"), c64, s64, HEAD_DIM // 2) * scale).astype(BF16)
    kb_ref[...] = _rope(proj("b_k"), c64, s64, HEAD_DIM // 2).astype(BF16)
    vb_ref[...] = proj("b_v").astype(BF16)

    qc_ref[...] = (_rope(proj("c_q"), c64, s64, HEAD_DIM // 2) * scale).astype(BF16)
    kc_ref[...] = _rope(proj("c_k"), c64, s64, HEAD_DIM // 2).astype(BF16)
    qi_ref[...] = _rope(proj("c_qi"), c32, s32, IDX_DIM // 2).astype(BF16)
    ki_ref[...] = _rope(proj("c_ki"), c32, s32, IDX_DIM // 2).astype(BF16)

    qd_ref[...] = (proj("d_q") * scale).astype(BF16)
    kd_ref[...] = proj("d_k").astype(BF16)

    t = _dot_nt(wt_ref[...], h)
    vct_ref[0] = t[:GROUP_WIDTH].astype(BF16)
    vdt_ref[0] = t[GROUP_WIDTH:2 * GROUP_WIDTH].astype(BF16)
    misct_ref[0] = t[2 * GROUP_WIDTH:]


def _inproj(x2d, g1, w_row, w_t, qn, kvn, wuq, wuk, wuvt, tables, *, batch, seq, tm):
    n = x2d.shape[0]
    d = x2d.shape[1]
    spb = seq // tm
    row = lambda i: (i, 0)
    const = lambda i: (0, 0)
    tab = lambda i: (i % spb, 0)
    tr = lambda i: (i // spb, 0, i % spb)
    once = pl.Buffered(1)

    def out2(width):
        return jax.ShapeDtypeStruct((n, width), BF16), pl.BlockSpec((tm, width), row)

    def out_t(rows, dtype):
        return jax.ShapeDtypeStruct((batch, rows, seq), dtype), pl.BlockSpec((1, rows, tm), tr)

    outs = [out2(512), out2(512),
            out2(256), out2(128), out2(128),
            out2(256), out2(256), out2(256), out2(128),
            out2(256), out2(256),
            out_t(GROUP_WIDTH, BF16), out_t(GROUP_WIDTH, BF16), out_t(GROUP_WIDTH, BF16),
            out_t(MISC_ROWS, F32)]
    in_specs = [
        pl.BlockSpec((tm, d), row),
        pl.BlockSpec((1, d), const),
        pl.BlockSpec((d, ROW_WIDTH), const, pipeline_mode=once),
        pl.BlockSpec((T_ROWS, d), const, pipeline_mode=once),
        pl.BlockSpec((1, MLA_Q_RANK), const),
        pl.BlockSpec((1, MLA_KV_RANK), const),
        pl.BlockSpec((MLA_Q_RANK, N_HEADS * LANES), const, pipeline_mode=once),
        pl.BlockSpec((MLA_KV_RANK, N_HEADS * LANES), const, pipeline_mode=once),
        pl.BlockSpec((GROUP_WIDTH, MLA_KV_RANK), const, pipeline_mode=once),
    ] + [pl.BlockSpec((tm, LANES), tab)] * 6
    return pl.pallas_call(
        _inproj_kernel,
        grid=(n // tm,),
        in_specs=in_specs,
        out_specs=[o[1] for o in outs],
        out_shape=[o[0] for o in outs],
        compiler_params=pltpu.CompilerParams(
            dimension_semantics=("parallel",), vmem_limit_bytes=VMEM_LIMIT),
        name="inproj",
    )(x2d, g1, w_row, w_t, qn, kvn, wuq, wuk, wuvt, *tables)


def _cumsum_kernel(f_ref, b_ref, o_ref, *, segs):
    x = f_ref[...] + b_ref[...]
    ls = jnp.minimum(x, 0.0) - jnp.log1p(jnp.exp(-jnp.abs(x)))
    rows = x.shape[0]
    r = lax.broadcasted_iota(I32, (LANES, LANES), 0)
    c = lax.broadcasted_iota(I32, (LANES, LANES), 1)
    upper = (r <= c).astype(F32)
    within = jnp.dot(ls, upper, preferred_element_type=F32, precision=lax.Precision.HIGHEST)
    tot = jnp.broadcast_to(within[:, LANES - 1:LANES], (rows, LANES))
    rr = lax.broadcasted_iota(I32, (rows, rows), 0)
    cc = lax.broadcasted_iota(I32, (rows, rows), 1)
    before = ((cc < rr) & (cc // segs == rr // segs)).astype(F32)
    off = jnp.dot(before, tot, preferred_element_type=F32, precision=lax.Precision.HIGHEST)
    o_ref[...] = (within + off) * LOG2E


def _forget_cumsum(f_t, bias, *, batch, seq):
    segs = seq // LANES
    rows = N_HEADS * segs
    f2 = f_t.reshape(batch * rows, LANES)
    b2 = jnp.broadcast_to(jnp.repeat(bias, segs)[:, None], (rows, LANES))
    out = pl.pallas_call(
        functools.partial(_cumsum_kernel, segs=segs),
        grid=(batch,),
        in_specs=[pl.BlockSpec((rows, LANES), lambda b: (b, 0)),
                  pl.BlockSpec((rows, LANES), lambda b: (0, 0))],
        out_specs=pl.BlockSpec((rows, LANES), lambda b: (b, 0)),
        out_shape=jax.ShapeDtypeStruct((batch * rows, LANES), F32),
        compiler_params=pltpu.CompilerParams(dimension_semantics=("parallel",)),
        name="forget_cumsum",
    )(f2, b2)
    return out.reshape(batch, N_HEADS, seq)


def _flash_kernel(*refs, heads, tq, ck, has_bias):
    if has_bias:
        q_ref, k_ref, vt_ref, b_ref, o_ref, acc_ref, sa_ref, sb_ref = refs
    else:
        q_ref, k_ref, vt_ref, o_ref, acc_ref, sa_ref, sb_ref = refs
        b_ref = None
    assert tq == ck
    nh = len(heads)
    i = pl.program_id(1)
    q_start = i * tq
    n_full = q_start // ck
    kpos_l = lax.broadcasted_iota(I32, (ck, tq), 0)
    qpos = q_start + lax.broadcasted_iota(I32, (ck, tq), 1)
    qs = []
    for ql, qhalf, _, _ in heads:
        q = q_ref[0, :, ql:ql + LANES]
        qs.append(q if qhalf is None else _half_mask(q, qhalf))
    acc_ref[...] = jnp.zeros(acc_ref.shape, F32)

    def scores(c, s_ref):
        off = pl.multiple_of(c * ck, ck)
        for h, (_, _, kl, _) in enumerate(heads):
            s = _dot_nt(k_ref[0, pl.ds(off, ck), kl:kl + LANES], qs[h])
            if has_bias:
                s = s - b_ref[0, pl.ds(off, ck), h:h + 1]
            s_ref[h] = s

    def consume(c, s_ref, carry, masked):
        ms, ls = carry
        off = pl.multiple_of(c * ck, ck)
        mask = (off + kpos_l <= qpos) if masked else None
        new_ms, new_ls = [], []
        for h, (_, _, _, pair) in enumerate(heads):
            p, m_new, l8, alpha = _softmax_step(s_ref[h], ms[h], ls[h], mask)
            vt = vt_ref[0, pair * LANES:(pair + 1) * LANES, pl.ds(off, ck)]
            acc_ref[h] = alpha * acc_ref[h] + _dot(vt, p)
            new_ms.append(m_new)
            new_ls.append(l8)
        return tuple(new_ms), tuple(new_ls)

    def pair_body(p, carry):
        c = 2 * p
        scores(c + 1, sb_ref)
        carry = consume(c, sa_ref, carry, False)
        scores(c + 2, sa_ref)
        return consume(c + 1, sb_ref, carry, False)

    carry = (tuple(jnp.full((1, tq), NEG, F32) for _ in range(nh)),
             tuple(jnp.zeros((SUBLANES, tq), F32) for _ in range(nh)))
    scores(0, sa_ref)
    carry = lax.fori_loop(0, n_full // 2, pair_body, carry)
    odd = n_full % 2 == 1

    def tail_odd(carry):
        scores(n_full, sb_ref)
        carry = consume(n_full - 1, sa_ref, carry, False)
        return consume(n_full, sb_ref, carry, True)

    carry = lax.cond(odd, tail_odd, lambda cr: consume(n_full, sa_ref, cr, True), carry)
    _, ls = carry
    sub = lax.broadcasted_iota(I32, (LANES, tq), 0)
    for pair in range(nh // 2):
        o0 = acc_ref[2 * pair] / jnp.sum(ls[2 * pair], axis=0, keepdims=True)
        o1 = acc_ref[2 * pair + 1] / jnp.sum(ls[2 * pair + 1], axis=0, keepdims=True)
        o_t = jnp.where(sub < HEAD_DIM, o0, o1)
        o_ref[0, :, pair * LANES:(pair + 1) * LANES] = o_t.T.astype(o_ref.dtype)


def _flash(q, k, v_t, bias, heads, *, tq, ck):
    batch, seq, wq = q.shape
    wk = k.shape[2]
    has_bias = bias is not None
    in_specs = [pl.BlockSpec((1, tq, wq), lambda b, i: (b, i, 0)),
                pl.BlockSpec((1, seq, wk), lambda b, i: (b, 0, 0)),
                pl.BlockSpec((1, GROUP_WIDTH, seq), lambda b, i: (b, 0, 0))]
    args = [q, k, v_t]
    if has_bias:
        in_specs.append(pl.BlockSpec((1, seq, N_HEADS), lambda b, i: (b, 0, 0)))
        args.append(bias)
    return pl.pallas_call(
        functools.partial(_flash_kernel, heads=heads, tq=tq, ck=ck, has_bias=has_bias),
        grid=(batch, seq // tq),
        in_specs=in_specs,
        out_specs=pl.BlockSpec((1, tq, GROUP_WIDTH), lambda b, i: (b, i, 0)),
        out_shape=jax.ShapeDtypeStruct((batch, seq, GROUP_WIDTH), BF16),
        scratch_shapes=[pltpu.VMEM((len(heads), LANES, tq), F32),
                        pltpu.VMEM((len(heads), ck, tq), F32),
                        pltpu.VMEM((len(heads), ck, tq), F32)],
        compiler_params=pltpu.CompilerParams(
            dimension_semantics=("parallel", "arbitrary"), vmem_limit_bytes=VMEM_LIMIT),
        name="flash_bias" if has_bias else "flash",
    )(*args)


def _swa_kernel(sink_ref, q_ref, k_ref, v_ref, o_ref, *, tq, window):
    i = pl.program_id(1)
    q_start = i * tq
    span = tq + window
    start = pl.multiple_of(jnp.maximum(q_start - window, 0), LANES)
    k = k_ref[0, pl.ds(start, span), :]
    v = v_ref[0, pl.ds(start, span), :]
    lane = lax.broadcasted_iota(I32, (tq, LANES), 1)
    qpos = q_start + lax.broadcasted_iota(I32, (tq, span), 0)
    kpos = start + lax.broadcasted_iota(I32, (tq, span), 1)
    valid = (kpos <= qpos) & (qpos - kpos < window)
    for pair in range(2):
        qp = q_ref[0, :, pair * LANES:(pair + 1) * LANES]
        outs = []
        for half in range(2):
            head = pair + 2 * half
            sink = sink_ref[head] * LOG2E
            s = jnp.where(valid, _dot_nt(_half_mask(qp, half), k), NEG)
            m = jnp.maximum(jnp.max(s, axis=1, keepdims=True), sink)
            p = jnp.exp2(s - m)
            denom = jnp.sum(p, axis=1, keepdims=True) + jnp.exp2(sink - m)
            outs.append(_dot(p.astype(BF16), v) / denom)
        o_ref[0, :, pair * LANES:(pair + 1) * LANES] = jnp.where(
            lane < HEAD_DIM, outs[0], outs[1]).astype(o_ref.dtype)


def _swa(q, k, v, sinks, *, tq):
    batch, seq, _ = q.shape
    return pl.pallas_call(
        functools.partial(_swa_kernel, tq=tq, window=SWA_WINDOW),
        grid=(batch, seq // tq),
        in_specs=[pl.BlockSpec(memory_space=pltpu.SMEM),
                  pl.BlockSpec((1, tq, GROUP_WIDTH), lambda b, i: (b, i, 0)),
                  pl.BlockSpec((1, seq, LANES), lambda b, i: (b, 0, 0)),
                  pl.BlockSpec((1, seq, LANES), lambda b, i: (b, 0, 0))],
        out_specs=pl.BlockSpec((1, tq, GROUP_WIDTH), lambda b, i: (b, i, 0)),
        out_shape=jax.ShapeDtypeStruct((batch, seq, GROUP_WIDTH), BF16),
        compiler_params=pltpu.CompilerParams(
            dimension_semantics=("parallel", "arbitrary"), vmem_limit_bytes=VMEM_LIMIT),
        name="swa",
    )(sinks, q, k, v)


def _dsa_kernel(qi_ref, w_ref, ki_ref, qc_ref, kc_ref, vct_ref, o_ref,
                keys_ref, acc_ref, sa_ref, sb_ref, *, tq, ck, topk, seq):
    assert tq == ck
    i = pl.program_id(1)
    q_start = i * tq
    n_chunks = (q_start + tq) // ck
    n_full = q_start // ck
    lane = lax.broadcasted_iota(I32, (tq, LANES), 1)
    kpos_l = lax.broadcasted_iota(I32, (ck, tq), 0)
    qpos = q_start + lax.broadcasted_iota(I32, (ck, tq), 1)
    qpos_row = q_start + lax.broadcasted_iota(I32, (1, tq), 1)
    msb = jnp.int32(INT_MIN)

    qm = []
    for h in range(IDX_HEADS):
        g, r = divmod(h, LANES // IDX_DIM)
        qg = qi_ref[0, :, g * LANES:(g + 1) * LANES]
        qm.append(jnp.where(lane // IDX_DIM == r, qg, jnp.zeros_like(qg)))
    w = w_ref[0]

    def flip(v):
        return v ^ ((v >> 31) & jnp.int32(0x7FFFFFFF))

    def group_reduce(fn, v):
        quarter = v.shape[0] // 4
        parts = [fn(v[j * quarter:(j + 1) * quarter].reshape(quarter // SUBLANES, SUBLANES, tq), axis=0)
                 for j in range(4)]
        return fn(jnp.stack([fn(jnp.stack(parts[:2]), axis=0), fn(jnp.stack(parts[2:]), axis=0)]),
                  axis=0)

    half_heads = IDX_HEADS // 2

    def index_dots(c, first, s_ref):
        ki = ki_ref[0, pl.ds(pl.multiple_of(c * ck, ck), ck), :]
        for j in range(half_heads):
            s_ref[j] = _dot_nt(ki, qm[first + j])

    def weighted_relu(first, s_ref):
        acc = None
        for j in range(half_heads):
            term = jnp.maximum(s_ref[j], 0.0) * w[first + j:first + j + 1, :]
            acc = term if acc is None else acc + term
        return acc

    def score_chunk(c, carry, last):
        kmax8, kmin8 = carry
        off = pl.multiple_of(c * ck, ck)
        index_dots(c, half_heads, sb_ref)
        acc = weighted_relu(0, sa_ref)
        if not last:
            index_dots(c + 1, 0, sa_ref)
        acc = acc + weighted_relu(half_heads, sb_ref)
        key = flip(lax.bitcast_convert_type(acc, I32))
        key_lo = key
        if last:
            causal = off + kpos_l <= qpos
            key_lo = jnp.where(causal, key, jnp.int32(2 ** 31 - 1))
            key = jnp.where(causal, key, msb)
        keys_ref[pl.ds(off, ck), :] = key
        return (jnp.maximum(kmax8, group_reduce(jnp.max, key)),
                jnp.minimum(kmin8, group_reduce(jnp.min, key_lo)))

    ext = (jnp.full((SUBLANES, tq), INT_MIN, I32), jnp.full((SUBLANES, tq), 2 ** 31 - 1, I32))
    index_dots(0, 0, sa_ref)
    ext = lax.fori_loop(0, n_full, lambda c, cr: score_chunk(c, cr, False), ext)
    ext = score_chunk(n_full, ext, True)
    kmax = jnp.max(ext[0], axis=0, keepdims=True)
    kmin = jnp.min(ext[1], axis=0, keepdims=True)

    cb = 2 * ck
    n_blocks = (n_chunks + 1) // 2
    kpos_b = lax.broadcasted_iota(I32, (cb, tq), 0)

    @pl.when(n_chunks % 2 == 1)
    def _():
        keys_ref[pl.ds(pl.multiple_of(n_chunks * ck, ck), ck), :] = jnp.full((ck, tq), INT_MIN, I32)

    def key_blocks(fn, init):
        def body(b, carry):
            off = pl.multiple_of(b * cb, cb)
            return fn(keys_ref[pl.ds(off, cb), :], off, carry)
        return lax.fori_loop(0, n_blocks, body, init)

    def count(pred):
        part = key_blocks(
            lambda kk, off, part: part + group_reduce(jnp.sum, jnp.where(pred(kk, off), 1, 0)),
            jnp.zeros((SUBLANES, tq), I32))
        return jnp.sum(part, axis=0, keepdims=True)

    need = qpos_row >= topk

    def settle(state, cand_k, c, lo_k, hi_k):
        done, tie, t_res = state
        hit = c == topk
        fin = (done == 0) & (hit | (hi_k - lo_k == 1))
        t_res = jnp.where(fin, jnp.where(hit, cand_k, lo_k), t_res)
        tie = jnp.where(fin & ~hit, 1, tie)
        return jnp.where(fin, 1, done), tie, t_res

    def bisect_step(state):
        it, flags, lo_k, hi_k, c_hi = state
        cand_k = lo_k + lax.shift_right_logical(hi_k - lo_k, 1)
        c = count(lambda kk, off: kk >= cand_k)
        ge = c >= topk
        lo_k = jnp.where(ge, cand_k, lo_k)
        hi_k = jnp.where(ge, hi_k, cand_k)
        return it + 1, settle(flags, cand_k, c, lo_k, hi_k), lo_k, hi_k, jnp.where(ge, c_hi, c)

    def snap_step(state):
        it, flags, lo_k, hi_k, c_hi = state

        top8 = key_blocks(
            lambda kk, off, part: jnp.maximum(
                part, group_reduce(jnp.max, jnp.where(kk < hi_k, kk, msb))),
            jnp.full((SUBLANES, tq), INT_MIN, I32))
        cand_k = jnp.maximum(jnp.max(top8, axis=0, keepdims=True), lo_k)
        c = count(lambda kk, off: kk >= cand_k)
        ge = c >= topk
        lo_k = jnp.where(ge, cand_k, lo_k)
        hi_k = jnp.where(ge, cand_k + 1, cand_k)
        return it + 1, settle(flags, cand_k, c, lo_k, hi_k), lo_k, hi_k, jnp.where(ge, c_hi, c)

    def unsettled(state):
        return jnp.min(state[1][0]) == 0

    lo0, hi0 = kmin, kmax + 1
    flat0 = need & (hi0 - lo0 == 1)
    flags = (jnp.where(need & ~flat0, 0, 1), jnp.where(flat0, 1, 0), jnp.where(need, lo0, msb + 1))
    state = (jnp.int32(0), flags, lo0, hi0, jnp.zeros((1, tq), I32))
    state = lax.fori_loop(0, BISECT_STEPS, lambda _, st: bisect_step(st), state)
    state = lax.while_loop(unsettled, lambda st: bisect_step(snap_step(st)), state)
    _, (_, tie_flag, t_s), _, _, n_gt = state
    tie = tie_flag > 0

    @pl.when(jnp.max(jnp.where(tie, 1, 0)) > 0)
    def _():
        want = (topk - n_gt).astype(F32)
        tri = (lax.broadcasted_iota(I32, (ck, ck), 0) > lax.broadcasted_iota(I32, (ck, ck), 1))
        tri = jnp.where(tri, 1.0, 0.0).astype(BF16)

        def demote(c, before):
            off = pl.multiple_of(c * ck, ck)
            kk = keys_ref[pl.ds(off, ck), :]
            tied = tie & (kk == t_s)
            ind = jnp.where(tied, 1.0, 0.0)
            rank = before + _dot(tri, ind.astype(BF16))
            keys_ref[pl.ds(off, ck), :] = jnp.where(tied & (rank >= want), msb, kk)
            return before + jnp.sum(ind, axis=0, keepdims=True)

        lax.fori_loop(0, n_chunks, demote, jnp.zeros((1, tq), F32))

    t_fin = jnp.maximum(t_s, msb + 1)
    qs = []
    for h in range(N_HEADS):
        pair, half = divmod(h, 2)
        qs.append(_half_mask(qc_ref[0, :, pair * LANES:(pair + 1) * LANES], half))
    acc_ref[...] = jnp.zeros(acc_ref.shape, F32)

    def scores(c, s_ref):
        off = pl.multiple_of(c * ck, ck)
        for h in range(N_HEADS):
            k = kc_ref[0, pl.ds(off, ck), (h // 2) * LANES:(h // 2 + 1) * LANES]
            s_ref[h] = _dot_nt(k, qs[h])

    def consume(c, s_ref, carry):
        ms, ls = carry
        off = pl.multiple_of(c * ck, ck)
        sel = keys_ref[pl.ds(off, ck), :] >= t_fin
        new_ms, new_ls = [], []
        for h in range(N_HEADS):
            pair = h // 2
            p, m_new, l8, alpha = _softmax_step(s_ref[h], ms[h], ls[h], sel)
            vt = vct_ref[0, pair * LANES:(pair + 1) * LANES, pl.ds(off, ck)]
            acc_ref[h] = alpha * acc_ref[h] + _dot(vt, p)
            new_ms.append(m_new)
            new_ls.append(l8)
        return tuple(new_ms), tuple(new_ls)

    def pair_body(p, carry):
        c = 2 * p
        scores(c + 1, sb_ref)
        carry = consume(c, sa_ref, carry)
        scores(c + 2, sa_ref)
        return consume(c + 1, sb_ref, carry)

    carry = (tuple(jnp.full((1, tq), NEG, F32) for _ in range(N_HEADS)),
             tuple(jnp.zeros((SUBLANES, tq), F32) for _ in range(N_HEADS)))
    scores(0, sa_ref)
    carry = lax.fori_loop(0, n_full // 2, pair_body, carry)

    def tail_odd(carry):
        scores(n_full, sb_ref)
        return consume(n_full, sb_ref, consume(n_full - 1, sa_ref, carry))

    _, ls = lax.cond(n_full % 2 == 1, tail_odd, lambda cr: consume(n_full, sa_ref, cr), carry)
    sub = lax.broadcasted_iota(I32, (LANES, tq), 0)
    for pair in range(2):
        o0 = acc_ref[2 * pair] / jnp.sum(ls[2 * pair], axis=0, keepdims=True)
        o1 = acc_ref[2 * pair + 1] / jnp.sum(ls[2 * pair + 1], axis=0, keepdims=True)
        o_t = jnp.where(sub < HEAD_DIM, o0, o1)
        o_ref[0, :, pair * LANES:(pair + 1) * LANES] = o_t.T.astype(o_ref.dtype)


def _dsa(qi, misc_t, ki, qc, kc, vc_t, *, tq, ck, topk):
    batch, seq, _ = qc.shape
    return pl.pallas_call(
        functools.partial(_dsa_kernel, tq=tq, ck=ck, topk=topk, seq=seq),
        grid=(batch, seq // tq),
        in_specs=[pl.BlockSpec((1, tq, IDX_HEADS * IDX_DIM), lambda b, i: (b, i, 0)),
                  pl.BlockSpec((1, MISC_ROWS, tq), lambda b, i: (b, 0, i)),
                  pl.BlockSpec((1, seq, LANES), lambda b, i: (b, 0, 0)),
                  pl.BlockSpec((1, tq, GROUP_WIDTH), lambda b, i: (b, i, 0)),
                  pl.BlockSpec((1, seq, GROUP_WIDTH), lambda b, i: (b, 0, 0)),
                  pl.BlockSpec((1, GROUP_WIDTH, seq), lambda b, i: (b, 0, 0))],
        out_specs=pl.BlockSpec((1, tq, GROUP_WIDTH), lambda b, i: (b, i, 0)),
        out_shape=jax.ShapeDtypeStruct((batch, seq, GROUP_WIDTH), BF16),
        scratch_shapes=[pltpu.VMEM((seq, tq), I32),
                        pltpu.VMEM((N_HEADS, LANES, tq), F32),
                        pltpu.VMEM((N_HEADS, ck, tq), F32),
                        pltpu.VMEM((N_HEADS, ck, tq), F32)],
        compiler_params=pltpu.CompilerParams(
            dimension_semantics=("parallel", "arbitrary"), vmem_limit_bytes=VMEM_LIMIT),
        name="dsa",
    )(qi, misc_t, ki, qc, kc, vc_t)


def _mlp_kernel(oa_ref, ob_ref, oc_ref, od_ref, x_ref, wo_ref, g2_ref, wup_ref, wdn_ref,
                gf_ref, out_ref, *, ff_chunk, final):
    gw = GROUP_WIDTH
    mix = None
    for g, o_ref in enumerate((oa_ref, ob_ref, oc_ref, od_ref)):
        y = _dot(o_ref[...], wo_ref[g * gw:(g + 1) * gw, :])
        mix = y if mix is None else mix + y
    x1 = x_ref[...] + mix
    h2 = _rms(x1, g2_ref[...]).astype(BF16)
    ffn = None
    for c in range(wup_ref.shape[1] // ff_chunk):
        u = _dot(h2, wup_ref[:, c * ff_chunk:(c + 1) * ff_chunk])
        a = jnp.square(jnp.maximum(u, 0.0)).astype(BF16)
        y = _dot(a, wdn_ref[c * ff_chunk:(c + 1) * ff_chunk, :])
        ffn = y if ffn is None else ffn + y
    acc = x1 + ffn
    if final:
        acc = _rms(acc, gf_ref[...])
    out_ref[...] = acc


def _mlp(oa, ob, oc, od, x2d, wo, g2, wup, wdn, gf, *, tm, final):
    n, d = x2d.shape
    dff = wup.shape[1]
    row = lambda i: (i, 0)
    const = lambda i: (0, 0)
    once = pl.Buffered(1)
    o_spec = pl.BlockSpec((tm, GROUP_WIDTH), row)
    return pl.pallas_call(
        functools.partial(_mlp_kernel, ff_chunk=1024, final=final),
        grid=(n // tm,),
        in_specs=[o_spec, o_spec, o_spec, o_spec,
                  pl.BlockSpec((tm, d), row),
                  pl.BlockSpec((d, d), const, pipeline_mode=once),
                  pl.BlockSpec((1, d), const),
                  pl.BlockSpec((d, dff), const, pipeline_mode=once),
                  pl.BlockSpec((dff, d), const, pipeline_mode=once),
                  pl.BlockSpec((1, d), const)],
        out_specs=pl.BlockSpec((tm, d), row),
        out_shape=jax.ShapeDtypeStruct((n, d), F32),
        compiler_params=pltpu.CompilerParams(
            dimension_semantics=("parallel",), vmem_limit_bytes=VMEM_LIMIT),
        name="outproj_mlp",
    )(oa, ob, oc, od, x2d, wo, g2, wup, wdn, gf)


def _rope_tables(seq):
    pos = jnp.arange(seq, dtype=F32)[:, None]
    lane = jnp.arange(LANES)

    def table(dim, active):
        half = dim // 2
        inv_freq = 1.0 / (ROPE_THETA ** (jnp.arange(0, half, dtype=F32) * 2.0 / dim))
        ang = pos * inv_freq[None, :][:, lane % half]
        sign = jnp.where((lane % dim) < half, -1.0, 1.0)
        cos = jnp.where(active, jnp.cos(ang), 1.0)
        sin = jnp.where(active, jnp.sin(ang) * sign, 0.0)
        return cos.astype(F32), sin.astype(F32)

    c64, s64 = table(HEAD_DIM, jnp.ones((LANES,), bool))
    c32, s32 = table(IDX_DIM, jnp.ones((LANES,), bool))
    cm, sm = table(MLA_ROPE, (lane >= MLA_NOPE) & (lane < MLA_NOPE + MLA_ROPE))
    return c64, s64, c32, s32, cm, sm


def _pack_weights(w_in, mla_w_uq, mla_w_ukv, w_out):
    depth, d, _ = w_in.shape
    splits = (MLA_Q_RANK, MLA_KV_RANK, MLA_ROPE,
              GROUP_WIDTH, SWA_KV_HEADS * HEAD_DIM, SWA_KV_HEADS * HEAD_DIM,
              GROUP_WIDTH, GROUP_WIDTH, GROUP_WIDTH, IDX_HEADS * IDX_DIM, IDX_DIM, IDX_HEADS,
              GROUP_WIDTH, GROUP_WIDTH, GROUP_WIDTH, N_HEADS)
    offs = [0]
    for s in splits:
        offs.append(offs[-1] + s)
    (a_cq, a_ckv, a_kr, b_q, b_k, b_v, c_q, c_k, c_v, c_qi, c_ki, c_w,
     d_q, d_k, d_v, d_f) = [w_in[:, :, offs[j]:offs[j + 1]] for j in range(len(splits))]
    zeros = lambda n: jnp.zeros((depth, d, n), w_in.dtype)
    swap = jnp.array([0, 2, 1, 3])
    a_kr = jnp.concatenate([zeros(MLA_NOPE), a_kr, zeros(LANES - MLA_NOPE - MLA_ROPE)], -1)
    b_q = b_q.reshape(depth, d, N_HEADS, HEAD_DIM)[:, :, swap].reshape(depth, d, GROUP_WIDTH)
    c_ki = jnp.concatenate([c_ki] * (LANES // IDX_DIM), -1)
    w_row = jnp.concatenate(
        [a_cq, a_ckv, a_kr, b_q, b_k, b_v, c_q, c_k, c_qi, c_ki, d_q, d_k], -1).astype(BF16)
    w_t = jnp.concatenate([c_v, d_v, c_w, d_f, zeros(MISC_ROWS - IDX_HEADS - N_HEADS)], -1)
    w_t = jnp.swapaxes(w_t, 1, 2).astype(BF16)

    pad = LANES - MLA_NOPE - MLA_ROPE
    wuq = mla_w_uq.reshape(depth, MLA_Q_RANK, N_HEADS, MLA_NOPE + MLA_ROPE)
    wuq = jnp.pad(wuq, ((0, 0), (0, 0), (0, 0), (0, pad))).reshape(depth, MLA_Q_RANK, N_HEADS * LANES)
    wukv = mla_w_ukv.reshape(depth, MLA_KV_RANK, N_HEADS, MLA_NOPE + HEAD_DIM)
    wuk = jnp.pad(wukv[..., :MLA_NOPE], ((0, 0), (0, 0), (0, 0), (0, LANES - MLA_NOPE)))
    wuk = wuk.reshape(depth, MLA_KV_RANK, N_HEADS * LANES)
    wuvt = jnp.swapaxes(wukv[..., MLA_NOPE:].reshape(depth, MLA_KV_RANK, GROUP_WIDTH), 1, 2)

    wo_b = w_out[:, GROUP_WIDTH:2 * GROUP_WIDTH].reshape(depth, N_HEADS, HEAD_DIM, -1)[:, swap]
    wo = jnp.concatenate([w_out[:, :GROUP_WIDTH], wo_b.reshape(depth, GROUP_WIDTH, -1),
                          w_out[:, 2 * GROUP_WIDTH:]], 1)
    return w_row, w_t, wuq.astype(BF16), wuk.astype(BF16), wuvt.astype(BF16), wo.astype(BF16)


_MLA_HEADS = tuple((h * LANES, None, h * LANES, h // 2) for h in range(N_HEADS))
_FOX_HEADS = tuple(((h // 2) * LANES, h % 2, (h // 2) * LANES, h // 2) for h in range(N_HEADS))


def kernel(x, norm1, w_in, mla_q_norm, mla_kv_norm, mla_w_uq, mla_w_ukv, swa_sinks, fox_b_f,
           w_out, norm2, w_up, w_down, final_norm):
    batch, seq, d = x.shape
    depth = w_in.shape[0]
    n = batch * seq
    top_k = min(DSA_TOPK, seq // 4)
    tm = min(512, seq)
    tq_flash = min(256, seq)
    ck_flash = min(256, seq)
    tq_swa = min(256, seq)
    tq_dsa = min(256, seq)
    ck_dsa = min(256, seq)

    tables = _rope_tables(seq)
    w_row, w_t, wuq, wuk, wuvt, wo = _pack_weights(w_in, mla_w_uq, mla_w_ukv, w_out)
    wup = w_up.astype(BF16)
    wdn = w_down.astype(BF16)

    x2d = x.reshape(n, d)
    for l in range(depth):
        (qa, ka, qb, kb, vb, qc, kc, qi, ki, qd, kd, va_t, vc_t, vd_t, misc_t) = _inproj(
            x2d, norm1[l][None], w_row[l], w_t[l], mla_q_norm[l][None], mla_kv_norm[l][None],
            wuq[l], wuk[l], wuvt[l], tables, batch=batch, seq=seq, tm=tm)
        r3 = lambda a: a.reshape(batch, seq, a.shape[-1])
        cum = _forget_cumsum(misc_t[:, IDX_HEADS:IDX_HEADS + N_HEADS], fox_b_f[l],
                             batch=batch, seq=seq)
        cum_col = jnp.swapaxes(cum, 1, 2)
        o_a = _flash(r3(qa), r3(ka), va_t, None, _MLA_HEADS, tq=tq_flash, ck=ck_flash)
        o_b = _swa(r3(qb), r3(kb), r3(vb), swa_sinks[l], tq=tq_swa)
        o_c = _dsa(r3(qi), misc_t, r3(ki), r3(qc), r3(kc), vc_t, tq=tq_dsa, ck=ck_dsa, topk=top_k)
        o_d = _flash(r3(qd), r3(kd), vd_t, cum_col, _FOX_HEADS, tq=tq_flash, ck=ck_flash)
        x2d = _mlp(o_a.reshape(n, -1), o_b.reshape(n, -1), o_c.reshape(n, -1), o_d.reshape(n, -1),
                   x2d, wo[l], norm2[l][None], wup[l], wdn[l], final_norm[None],
                   tm=tm, final=(l == depth - 1))
    return x2d.reshape(batch, seq, d)
```

```python
import functools
import math

import jax
import jax.numpy as jnp
from jax import lax
from jax.experimental import pallas as pl
from jax.experimental.pallas import tpu as pltpu

F32 = jnp.float32
BF16 = jnp.bfloat16
I32 = jnp.int32

HEAD_DIM = 64
N_HEADS = 4
GROUP_WIDTH = N_HEADS * HEAD_DIM
MLA_Q_RANK = 256
MLA_KV_RANK = 128
MLA_NOPE = 64
MLA_ROPE = 32
SWA_KV_HEADS = 2
SWA_WINDOW = 128
IDX_HEADS = 8
IDX_DIM = 32
DSA_TOPK = 256
ROPE_THETA = 10000.0
EPS = 1e-6

LANES = 128
SUBLANES = 8
VMEM_LIMIT = 56 * 1024 * 1024
NEG = -1e30
INT_MIN = -(2 ** 31)
LOG2E = math.log2(math.e)
BISECT_STEPS = 18

_ROW_GROUPS = (
    ("a_cq", 256), ("a_ckv", 128), ("a_kr", 128),
    ("b_q", 256), ("b_k", 128),
    ("c_q", 256), ("c_k", 256), ("c_qi", 256), ("c_ki", 128),
    ("d_q", 256), ("d_k", 256),
)
_ROW_OFF = {}
_acc = 0
for _n, _w in _ROW_GROUPS:
    _ROW_OFF[_n] = (_acc, _acc + _w)
    _acc += _w
ROW_WIDTH = _acc
MISC_ROWS = 16
SWA_KV_WIDTH = SWA_KV_HEADS * HEAD_DIM
T_ROWS = 2 * GROUP_WIDTH + SWA_KV_WIDTH + MISC_ROWS


def _dot(a, b):
    return jnp.dot(a, b, preferred_element_type=F32)


def _dot_nt(a, b):
    return lax.dot_general(a, b, (((1,), (1,)), ((), ())), preferred_element_type=F32)


def _rms(x, g):
    return x * lax.rsqrt(jnp.mean(x * x, axis=-1, keepdims=True) + EPS) * g


def _rope(x, cos, sin, half):
    width = x.shape[1]
    reps = width // LANES
    if reps > 1:
        cos = jnp.concatenate([cos] * reps, axis=1)
        sin = jnp.concatenate([sin] * reps, axis=1)
    lane = lax.broadcasted_iota(I32, x.shape, 1)
    first = (lane % (2 * half)) < half
    rot = jnp.where(first, pltpu.roll(x, width - half, 1), pltpu.roll(x, half, 1))
    return x * cos + rot * sin


def _half_mask(q, half):
    lane = lax.broadcasted_iota(I32, q.shape, 1)
    return jnp.where((lane >= HEAD_DIM) == bool(half), q, jnp.zeros_like(q))


def _softmax_step(s, m, l8, mask):
    ck, tq = s.shape
    if mask is not None:
        s = jnp.where(mask, s, NEG)
    m_new = jnp.maximum(m, jnp.max(s, axis=0, keepdims=True))
    alpha = jnp.exp2(m - m_new)
    p = jnp.exp2(s - m_new)
    l8 = alpha * l8 + jnp.sum(p.reshape(ck // SUBLANES, SUBLANES, tq), axis=0)
    return p.astype(BF16), m_new, l8, alpha


def _inproj_kernel(x_ref, g1_ref, w_ref, wt_ref, qn_ref, kvn_ref, wuq_ref, wuk_ref, wuvt_ref,
                   c64_ref, s64_ref, c32_ref, s32_ref, cm_ref, sm_ref,
                   qa_ref, ka_ref, qb_ref, kb_ref,
                   qc_ref, kc_ref, qi_ref, ki_ref, qd_ref, kd_ref,
                   vat_ref, vct_ref, vdt_ref, vbt_ref, misct_ref):
    h = _rms(x_ref[...], g1_ref[...]).astype(BF16)

    def proj(name):
        lo, hi = _ROW_OFF[name]
        return _dot(h, w_ref[:, lo:hi])

    c64, s64 = c64_ref[...], s64_ref[...]
    c32, s32 = c32_ref[...], s32_ref[...]
    cm, sm = cm_ref[...], sm_ref[...]
    scale_a = (MLA_NOPE + MLA_ROPE) ** -0.5 * LOG2E
    scale = HEAD_DIM ** -0.5 * LOG2E

    cq = _rms(proj("a_cq"), qn_ref[...]).astype(BF16)
    qa = _rope(_dot(cq, wuq_ref[...]), cm, sm, MLA_ROPE // 2) * scale_a
    qa_ref[...] = qa.astype(BF16)
    ckv = _rms(proj("a_ckv"), kvn_ref[...]).astype(BF16)
    kr = _rope(proj("a_kr"), cm, sm, MLA_ROPE // 2)
    ka_ref[...] = (_dot(ckv, wuk_ref[...]) + jnp.concatenate([kr] * N_HEADS, axis=1)).astype(BF16)
    vat_ref[0] = _dot_nt(wuvt_ref[...], ckv).astype(BF16)

    qb_ref[...] = (_rope(proj("b_q"), c64, s64, HEAD_DIM // 2) * scale).astype(BF16)
    kb_ref[...] = _rope(proj("b_k"), c64, s64, HEAD_DIM // 2).astype(BF16)

    qc_ref[...] = (_rope(proj("c_q"), c64, s64, HEAD_DIM // 2) * scale).astype(BF16)
    kc_ref[...] = _rope(proj("c_k"), c64, s64, HEAD_DIM // 2).astype(BF16)
    qi_ref[...] = _rope(proj("c_qi"), c32, s32, IDX_DIM // 2).astype(BF16)
    ki_ref[...] = _rope(proj("c_ki"), c32, s32, IDX_DIM // 2).astype(BF16)

    qd_ref[...] = (proj("d_q") * scale).astype(BF16)
    kd_ref[...] = proj("d_k").astype(BF16)

    t = _dot_nt(wt_ref[...], h)
    vct_ref[0] = t[:GROUP_WIDTH].astype(BF16)
    vdt_ref[0] = t[GROUP_WIDTH:2 * GROUP_WIDTH].astype(BF16)
    vbt_ref[0] = t[2 * GROUP_WIDTH:2 * GROUP_WIDTH + SWA_KV_WIDTH].astype(BF16)
    misct_ref[0] = t[2 * GROUP_WIDTH + SWA_KV_WIDTH:]


def _inproj(x2d, g1, w_row, w_t, qn, kvn, wuq, wuk, wuvt, tables, *, batch, seq, tm):
    n = x2d.shape[0]
    d = x2d.shape[1]
    spb = seq // tm
    row = lambda i: (i, 0)
    const = lambda i: (0, 0)
    tab = lambda i: (i % spb, 0)
    tr = lambda i: (i // spb, 0, i % spb)
    once = pl.Buffered(1)

    def out2(width):
        return jax.ShapeDtypeStruct((n, width), BF16), pl.BlockSpec((tm, width), row)

    def out_t(rows, dtype):
        return jax.ShapeDtypeStruct((batch, rows, seq), dtype), pl.BlockSpec((1, rows, tm), tr)

    outs = [out2(512), out2(512),
            out2(256), out2(128),
            out2(256), out2(256), out2(256), out2(128),
            out2(256), out2(256),
            out_t(GROUP_WIDTH, BF16), out_t(GROUP_WIDTH, BF16), out_t(GROUP_WIDTH, BF16),
            out_t(SWA_KV_WIDTH, BF16), out_t(MISC_ROWS, F32)]
    in_specs = [
        pl.BlockSpec((tm, d), row),
        pl.BlockSpec((1, d), const),
        pl.BlockSpec((d, ROW_WIDTH), const, pipeline_mode=once),
        pl.BlockSpec((T_ROWS, d), const, pipeline_mode=once),
        pl.BlockSpec((1, MLA_Q_RANK), const),
        pl.BlockSpec((1, MLA_KV_RANK), const),
        pl.BlockSpec((MLA_Q_RANK, N_HEADS * LANES), const, pipeline_mode=once),
        pl.BlockSpec((MLA_KV_RANK, N_HEADS * LANES), const, pipeline_mode=once),
        pl.BlockSpec((GROUP_WIDTH, MLA_KV_RANK), const, pipeline_mode=once),
    ] + [pl.BlockSpec((tm, LANES), tab)] * 6
    return pl.pallas_call(
        _inproj_kernel,
        grid=(n // tm,),
        in_specs=in_specs,
        out_specs=[o[1] for o in outs],
        out_shape=[o[0] for o in outs],
        compiler_params=pltpu.CompilerParams(
            dimension_semantics=("parallel",), vmem_limit_bytes=VMEM_LIMIT),
        name="inproj",
    )(x2d, g1, w_row, w_t, qn, kvn, wuq, wuk, wuvt, *tables)


def _cumsum_kernel(f_ref, b_ref, o_ref, *, segs):
    x = f_ref[...] + b_ref[...]
    ls = jnp.minimum(x, 0.0) - jnp.log1p(jnp.exp(-jnp.abs(x)))
    rows = x.shape[0]
    r = lax.broadcasted_iota(I32, (LANES, LANES), 0)
    c = lax.broadcasted_iota(I32, (LANES, LANES), 1)
    upper = (r <= c).astype(F32)
    within = jnp.dot(ls, upper, preferred_element_type=F32, precision=lax.Precision.HIGHEST)
    tot = jnp.broadcast_to(within[:, LANES - 1:LANES], (rows, LANES))
    rr = lax.broadcasted_iota(I32, (rows, rows), 0)
    cc = lax.broadcasted_iota(I32, (rows, rows), 1)
    before = ((cc < rr) & (cc // segs == rr // segs)).astype(F32)
    off = jnp.dot(before, tot, preferred_element_type=F32, precision=lax.Precision.HIGHEST)
    o_ref[...] = (within + off) * LOG2E


def _forget_cumsum(f_t, bias, *, batch, seq):
    segs = seq // LANES
    rows = N_HEADS * segs
    f2 = f_t.reshape(batch * rows, LANES)
    b2 = jnp.broadcast_to(jnp.repeat(bias, segs)[:, None], (rows, LANES))
    out = pl.pallas_call(
        functools.partial(_cumsum_kernel, segs=segs),
        grid=(batch,),
        in_specs=[pl.BlockSpec((rows, LANES), lambda b: (b, 0)),
                  pl.BlockSpec((rows, LANES), lambda b: (0, 0))],
        out_specs=pl.BlockSpec((rows, LANES), lambda b: (b, 0)),
        out_shape=jax.ShapeDtypeStruct((batch * rows, LANES), F32),
        compiler_params=pltpu.CompilerParams(dimension_semantics=("parallel",)),
        name="forget_cumsum",
    )(f2, b2)
    return out.reshape(batch, N_HEADS, seq)


def _flash_kernel(*refs, heads, tq, ck, has_bias):
    if has_bias:
        q_ref, k_ref, vt_ref, b_ref, o_ref, acc_ref, sa_ref, sb_ref = refs
    else:
        q_ref, k_ref, vt_ref, o_ref, acc_ref, sa_ref, sb_ref = refs
        b_ref = None
    assert tq == ck
    nh = len(heads)
    i = pl.program_id(1)
    q_start = i * tq
    n_full = q_start // ck
    kpos_l = lax.broadcasted_iota(I32, (ck, tq), 0)
    qpos = q_start + lax.broadcasted_iota(I32, (ck, tq), 1)
    qs = []
    for ql, qhalf, _ in heads:
        q = q_ref[0, :, ql:ql + LANES]
        qs.append(q if qhalf is None else _half_mask(q, qhalf))
    acc_ref[...] = jnp.zeros(acc_ref.shape, F32)

    def scores(c, s_ref):
        off = pl.multiple_of(c * ck, ck)
        for h, (_, _, kl) in enumerate(heads):
            s = _dot_nt(k_ref[0, pl.ds(off, ck), kl:kl + LANES], qs[h])
            if has_bias:
                s = s - b_ref[0, pl.ds(off, ck), h:h + 1]
            s_ref[h] = s

    def consume(c, s_ref, carry, masked):
        ms, ls = carry
        off = pl.multiple_of(c * ck, ck)
        mask = (off + kpos_l <= qpos) if masked else None
        new_ms, new_ls = [], []
        for h in range(nh):
            p, m_new, l8, alpha = _softmax_step(s_ref[h], ms[h], ls[h], mask)
            vt = vt_ref[0, h * HEAD_DIM:(h + 1) * HEAD_DIM, pl.ds(off, ck)]
            acc_ref[h] = alpha * acc_ref[h] + _dot(vt, p)
            new_ms.append(m_new)
            new_ls.append(l8)
        return tuple(new_ms), tuple(new_ls)

    def pair_body(p, carry):
        c = 2 * p
        scores(c + 1, sb_ref)
        carry = consume(c, sa_ref, carry, False)
        scores(c + 2, sa_ref)
        return consume(c + 1, sb_ref, carry, False)

    carry = (tuple(jnp.full((1, tq), NEG, F32) for _ in range(nh)),
             tuple(jnp.zeros((SUBLANES, tq), F32) for _ in range(nh)))
    scores(0, sa_ref)
    carry = lax.fori_loop(0, n_full // 2, pair_body, carry)
    odd = n_full % 2 == 1

    def tail_odd(carry):
        scores(n_full, sb_ref)
        carry = consume(n_full - 1, sa_ref, carry, False)
        return consume(n_full, sb_ref, carry, True)

    carry = lax.cond(odd, tail_odd, lambda cr: consume(n_full, sa_ref, cr, True), carry)
    _, ls = carry
    for pair in range(nh // 2):
        o0 = acc_ref[2 * pair] / jnp.sum(ls[2 * pair], axis=0, keepdims=True)
        o1 = acc_ref[2 * pair + 1] / jnp.sum(ls[2 * pair + 1], axis=0, keepdims=True)
        o_t = jnp.concatenate([o0, o1], axis=0)
        o_ref[0, :, pair * LANES:(pair + 1) * LANES] = o_t.T.astype(o_ref.dtype)


def _flash(q, k, v_t, bias, heads, *, tq, ck):
    batch, seq, wq = q.shape
    wk = k.shape[2]
    has_bias = bias is not None
    in_specs = [pl.BlockSpec((1, tq, wq), lambda b, i: (b, i, 0)),
                pl.BlockSpec((1, seq, wk), lambda b, i: (b, 0, 0)),
                pl.BlockSpec((1, GROUP_WIDTH, seq), lambda b, i: (b, 0, 0))]
    args = [q, k, v_t]
    if has_bias:
        in_specs.append(pl.BlockSpec((1, seq, N_HEADS), lambda b, i: (b, 0, 0)))
        args.append(bias)
    return pl.pallas_call(
        functools.partial(_flash_kernel, heads=heads, tq=tq, ck=ck, has_bias=has_bias),
        grid=(batch, seq // tq),
        in_specs=in_specs,
        out_specs=pl.BlockSpec((1, tq, GROUP_WIDTH), lambda b, i: (b, i, 0)),
        out_shape=jax.ShapeDtypeStruct((batch, seq, GROUP_WIDTH), BF16),
        scratch_shapes=[pltpu.VMEM((len(heads), HEAD_DIM, tq), F32),
                        pltpu.VMEM((len(heads), ck, tq), F32),
                        pltpu.VMEM((len(heads), ck, tq), F32)],
        compiler_params=pltpu.CompilerParams(
            dimension_semantics=("parallel", "arbitrary"), vmem_limit_bytes=VMEM_LIMIT),
        name="flash_bias" if has_bias else "flash",
    )(*args)


def _swa_kernel(sink_ref, q_ref, k_ref, vt_ref, o_ref, *, tq, window):
    i = pl.program_id(1)
    q_start = i * tq
    span = tq + window
    start = pl.multiple_of(jnp.maximum(q_start - window, 0), LANES)
    k = k_ref[0, pl.ds(start, span), :]
    kpos = start + lax.broadcasted_iota(I32, (span, tq), 0)
    qpos = q_start + lax.broadcasted_iota(I32, (span, tq), 1)
    valid = (kpos <= qpos) & (qpos - kpos < window)
    for pair in range(2):
        qp = q_ref[0, :, pair * LANES:(pair + 1) * LANES]
        outs = []
        for half in range(2):
            head = pair + 2 * half
            sink = sink_ref[head] * LOG2E
            s = jnp.where(valid, _dot_nt(k, _half_mask(qp, half)), NEG)
            m = jnp.maximum(jnp.max(s, axis=0, keepdims=True), sink)
            p = jnp.exp2(s - m)
            denom = jnp.sum(p, axis=0, keepdims=True) + jnp.exp2(sink - m)
            vt = vt_ref[0, half * HEAD_DIM:(half + 1) * HEAD_DIM, pl.ds(start, span)]
            outs.append(_dot(vt, p.astype(BF16)) / denom)
        o_t = jnp.concatenate(outs, axis=0)
        o_ref[0, :, pair * LANES:(pair + 1) * LANES] = o_t.T.astype(o_ref.dtype)


def _swa(q, k, v_t, sinks, *, tq):
    batch, seq, _ = q.shape
    return pl.pallas_call(
        functools.partial(_swa_kernel, tq=tq, window=SWA_WINDOW),
        grid=(batch, seq // tq),
        in_specs=[pl.BlockSpec(memory_space=pltpu.SMEM),
                  pl.BlockSpec((1, tq, GROUP_WIDTH), lambda b, i: (b, i, 0)),
                  pl.BlockSpec((1, seq, LANES), lambda b, i: (b, 0, 0)),
                  pl.BlockSpec((1, SWA_KV_HEADS * HEAD_DIM, seq), lambda b, i: (b, 0, 0))],
        out_specs=pl.BlockSpec((1, tq, GROUP_WIDTH), lambda b, i: (b, i, 0)),
        out_shape=jax.ShapeDtypeStruct((batch, seq, GROUP_WIDTH), BF16),
        compiler_params=pltpu.CompilerParams(
            dimension_semantics=("parallel", "arbitrary"), vmem_limit_bytes=VMEM_LIMIT),
        name="swa",
    )(sinks, q, k, v_t)


def _dsa_kernel(qi_ref, w_ref, ki_ref, qc_ref, kc_ref, vct_ref, o_ref,
                keys_ref, acc_ref, sa_ref, sb_ref, *, tq, ck, topk, seq):
    assert tq == ck
    i = pl.program_id(1)
    q_start = i * tq
    n_chunks = (q_start + tq) // ck
    n_full = q_start // ck
    lane = lax.broadcasted_iota(I32, (tq, LANES), 1)
    kpos_l = lax.broadcasted_iota(I32, (ck, tq), 0)
    qpos = q_start + lax.broadcasted_iota(I32, (ck, tq), 1)
    qpos_row = q_start + lax.broadcasted_iota(I32, (1, tq), 1)
    msb = jnp.int32(INT_MIN)

    qm = []
    for h in range(IDX_HEADS):
        g, r = divmod(h, LANES // IDX_DIM)
        qg = qi_ref[0, :, g * LANES:(g + 1) * LANES]
        qm.append(jnp.where(lane // IDX_DIM == r, qg, jnp.zeros_like(qg)))
    w = w_ref[0]

    def flip(v):
        return v ^ ((v >> 31) & jnp.int32(0x7FFFFFFF))

    def group_reduce(fn, v):
        quarter = v.shape[0] // 4
        parts = [fn(v[j * quarter:(j + 1) * quarter].reshape(quarter // SUBLANES, SUBLANES, tq), axis=0)
                 for j in range(4)]
        return fn(jnp.stack([fn(jnp.stack(parts[:2]), axis=0), fn(jnp.stack(parts[2:]), axis=0)]),
                  axis=0)

    half_heads = IDX_HEADS // 2

    def index_dots(c, first, s_ref):
        ki = ki_ref[0, pl.ds(pl.multiple_of(c * ck, ck), ck), :]
        for j in range(half_heads):
            s_ref[j] = _dot_nt(ki, qm[first + j])

    def weighted_relu(first, s_ref):
        acc = None
        for j in range(half_heads):
            term = jnp.maximum(s_ref[j], 0.0) * w[first + j:first + j + 1, :]
            acc = term if acc is None else acc + term
        return acc

    def score_chunk(c, carry, last):
        kmax8, kmin8 = carry
        off = pl.multiple_of(c * ck, ck)
        index_dots(c, half_heads, sb_ref)
        acc = weighted_relu(0, sa_ref)
        if not last:
            index_dots(c + 1, 0, sa_ref)
        acc = acc + weighted_relu(half_heads, sb_ref)
        key = flip(lax.bitcast_convert_type(acc, I32))
        key_lo = key
        if last:
            causal = off + kpos_l <= qpos
            key_lo = jnp.where(causal, key, jnp.int32(2 ** 31 - 1))
            key = jnp.where(causal, key, msb)
        keys_ref[pl.ds(off, ck), :] = key
        return (jnp.maximum(kmax8, group_reduce(jnp.max, key)),
                jnp.minimum(kmin8, group_reduce(jnp.min, key_lo)))

    ext = (jnp.full((SUBLANES, tq), INT_MIN, I32), jnp.full((SUBLANES, tq), 2 ** 31 - 1, I32))
    index_dots(0, 0, sa_ref)
    ext = lax.fori_loop(0, n_full, lambda c, cr: score_chunk(c, cr, False), ext)
    ext = score_chunk(n_full, ext, True)
    kmax = jnp.max(ext[0], axis=0, keepdims=True)
    kmin = jnp.min(ext[1], axis=0, keepdims=True)

    cb = 2 * ck
    n_blocks = (n_chunks + 1) // 2

    @pl.when(n_chunks % 2 == 1)
    def _():
        keys_ref[pl.ds(pl.multiple_of(n_chunks * ck, ck), ck), :] = jnp.full((ck, tq), INT_MIN, I32)

    def key_blocks(fn, init):
        def body(b, carry):
            off = pl.multiple_of(b * cb, cb)
            return fn(keys_ref[pl.ds(off, cb), :], off, carry)
        return lax.fori_loop(0, n_blocks, body, init)

    def count(pred):
        part = key_blocks(
            lambda kk, off, part: part + group_reduce(jnp.sum, jnp.where(pred(kk, off), 1, 0)),
            jnp.zeros((SUBLANES, tq), I32))
        return jnp.sum(part, axis=0, keepdims=True)

    need = qpos_row >= topk

    def settle(state, cand_k, c, lo_k, hi_k):
        done, tie, t_res = state
        hit = c == topk
        fin = (done == 0) & (hit | (hi_k - lo_k == 1))
        t_res = jnp.where(fin, jnp.where(hit, cand_k, lo_k), t_res)
        tie = jnp.where(fin & ~hit, 1, tie)
        return jnp.where(fin, 1, done), tie, t_res

    def bisect_step(state):
        it, flags, lo_k, hi_k, c_hi = state
        cand_k = lo_k + lax.shift_right_logical(hi_k - lo_k, 1)
        c = count(lambda kk, off: kk >= cand_k)
        ge = c >= topk
        lo_k = jnp.where(ge, cand_k, lo_k)
        hi_k = jnp.where(ge, hi_k, cand_k)
        return it + 1, settle(flags, cand_k, c, lo_k, hi_k), lo_k, hi_k, jnp.where(ge, c_hi, c)

    def snap_step(state):
        it, flags, lo_k, hi_k, c_hi = state

        top8 = key_blocks(
            lambda kk, off, part: jnp.maximum(
                part, group_reduce(jnp.max, jnp.where(kk < hi_k, kk, msb))),
            jnp.full((SUBLANES, tq), INT_MIN, I32))
        cand_k = jnp.maximum(jnp.max(top8, axis=0, keepdims=True), lo_k)
        c = count(lambda kk, off: kk >= cand_k)
        ge = c >= topk
        lo_k = jnp.where(ge, cand_k, lo_k)
        hi_k = jnp.where(ge, cand_k + 1, cand_k)
        return it + 1, settle(flags, cand_k, c, lo_k, hi_k), lo_k, hi_k, jnp.where(ge, c_hi, c)

    def unsettled(state):
        return jnp.min(state[1][0]) == 0

    lo0, hi0 = kmin, kmax + 1
    flat0 = need & (hi0 - lo0 == 1)
    flags = (jnp.where(need & ~flat0, 0, 1), jnp.where(flat0, 1, 0), jnp.where(need, lo0, msb + 1))
    state = (jnp.int32(0), flags, lo0, hi0, jnp.zeros((1, tq), I32))
    state = lax.fori_loop(0, BISECT_STEPS, lambda _, st: bisect_step(st), state)
    state = lax.while_loop(unsettled, lambda st: bisect_step(snap_step(st)), state)
    _, (_, tie_flag, t_s), _, _, n_gt = state
    tie = tie_flag > 0

    @pl.when(jnp.max(jnp.where(tie, 1, 0)) > 0)
    def _():
        want = (topk - n_gt).astype(F32)
        tri = (lax.broadcasted_iota(I32, (ck, ck), 0) > lax.broadcasted_iota(I32, (ck, ck), 1))
        tri = jnp.where(tri, 1.0, 0.0).astype(BF16)

        def demote(c, before):
            off = pl.multiple_of(c * ck, ck)
            kk = keys_ref[pl.ds(off, ck), :]
            tied = tie & (kk == t_s)
            ind = jnp.where(tied, 1.0, 0.0)
            rank = before + _dot(tri, ind.astype(BF16))
            keys_ref[pl.ds(off, ck), :] = jnp.where(tied & (rank >= want), msb, kk)
            return before + jnp.sum(ind, axis=0, keepdims=True)

        lax.fori_loop(0, n_chunks, demote, jnp.zeros((1, tq), F32))

    t_fin = jnp.maximum(t_s, msb + 1)
    qs = []
    for h in range(N_HEADS):
        pair, half = divmod(h, 2)
        qs.append(_half_mask(qc_ref[0, :, pair * LANES:(pair + 1) * LANES], half))
    acc_ref[...] = jnp.zeros(acc_ref.shape, F32)

    def scores(c, s_ref):
        off = pl.multiple_of(c * ck, ck)
        for h in range(N_HEADS):
            k = kc_ref[0, pl.ds(off, ck), (h // 2) * LANES:(h // 2 + 1) * LANES]
            s_ref[h] = _dot_nt(k, qs[h])

    def consume(c, s_ref, carry):
        ms, ls = carry
        off = pl.multiple_of(c * ck, ck)
        sel = keys_ref[pl.ds(off, ck), :] >= t_fin
        new_ms, new_ls = [], []
        for h in range(N_HEADS):
            pair = h // 2
            p, m_new, l8, alpha = _softmax_step(s_ref[h], ms[h], ls[h], sel)
            vt = vct_ref[0, h * HEAD_DIM:(h + 1) * HEAD_DIM, pl.ds(off, ck)]
            acc_ref[h] = alpha * acc_ref[h] + _dot(vt, p)
            new_ms.append(m_new)
            new_ls.append(l8)
        return tuple(new_ms), tuple(new_ls)

    def pair_body(p, carry):
        c = 2 * p
        scores(c + 1, sb_ref)
        carry = consume(c, sa_ref, carry)
        scores(c + 2, sa_ref)
        return consume(c + 1, sb_ref, carry)

    carry = (tuple(jnp.full((1, tq), NEG, F32) for _ in range(N_HEADS)),
             tuple(jnp.zeros((SUBLANES, tq), F32) for _ in range(N_HEADS)))
    scores(0, sa_ref)
    carry = lax.fori_loop(0, n_full // 2, pair_body, carry)

    def tail_odd(carry):
        scores(n_full, sb_ref)
        return consume(n_full, sb_ref, consume(n_full - 1, sa_ref, carry))

    _, ls = lax.cond(n_full % 2 == 1, tail_odd, lambda cr: consume(n_full, sa_ref, cr), carry)
    for pair in range(2):
        o0 = acc_ref[2 * pair] / jnp.sum(ls[2 * pair], axis=0, keepdims=True)
        o1 = acc_ref[2 * pair + 1] / jnp.sum(ls[2 * pair + 1], axis=0, keepdims=True)
        o_t = jnp.concatenate([o0, o1], axis=0)
        o_ref[0, :, pair * LANES:(pair + 1) * LANES] = o_t.T.astype(o_ref.dtype)


def _dsa(qi, misc_t, ki, qc, kc, vc_t, *, tq, ck, topk):
    batch, seq, _ = qc.shape
    return pl.pallas_call(
        functools.partial(_dsa_kernel, tq=tq, ck=ck, topk=topk, seq=seq),
        grid=(batch, seq // tq),
        in_specs=[pl.BlockSpec((1, tq, IDX_HEADS * IDX_DIM), lambda b, i: (b, i, 0)),
                  pl.BlockSpec((1, MISC_ROWS, tq), lambda b, i: (b, 0, i)),
                  pl.BlockSpec((1, seq, LANES), lambda b, i: (b, 0, 0)),
                  pl.BlockSpec((1, tq, GROUP_WIDTH), lambda b, i: (b, i, 0)),
                  pl.BlockSpec((1, seq, GROUP_WIDTH), lambda b, i: (b, 0, 0)),
                  pl.BlockSpec((1, GROUP_WIDTH, seq), lambda b, i: (b, 0, 0))],
        out_specs=pl.BlockSpec((1, tq, GROUP_WIDTH), lambda b, i: (b, i, 0)),
        out_shape=jax.ShapeDtypeStruct((batch, seq, GROUP_WIDTH), BF16),
        scratch_shapes=[pltpu.VMEM((seq, tq), I32),
                        pltpu.VMEM((N_HEADS, HEAD_DIM, tq), F32),
                        pltpu.VMEM((N_HEADS, ck, tq), F32),
                        pltpu.VMEM((N_HEADS, ck, tq), F32)],
        compiler_params=pltpu.CompilerParams(
            dimension_semantics=("parallel", "arbitrary"), vmem_limit_bytes=VMEM_LIMIT),
        name="dsa",
    )(qi, misc_t, ki, qc, kc, vc_t)


def _mlp_kernel(oa_ref, ob_ref, oc_ref, od_ref, x_ref, wo_ref, g2_ref, wup_ref, wdn_ref,
                gf_ref, out_ref, *, ff_chunk, final):
    gw = GROUP_WIDTH
    mix = None
    for g, o_ref in enumerate((oa_ref, ob_ref, oc_ref, od_ref)):
        y = _dot(o_ref[...], wo_ref[g * gw:(g + 1) * gw, :])
        mix = y if mix is None else mix + y
    x1 = x_ref[...] + mix
    h2 = _rms(x1, g2_ref[...]).astype(BF16)
    ffn = None
    for c in range(wup_ref.shape[1] // ff_chunk):
        u = _dot(h2, wup_ref[:, c * ff_chunk:(c + 1) * ff_chunk])
        a = jnp.square(jnp.maximum(u, 0.0)).astype(BF16)
        y = _dot(a, wdn_ref[c * ff_chunk:(c + 1) * ff_chunk, :])
        ffn = y if ffn is None else ffn + y
    acc = x1 + ffn
    if final:
        acc = _rms(acc, gf_ref[...])
    out_ref[...] = acc


def _mlp(oa, ob, oc, od, x2d, wo, g2, wup, wdn, gf, *, tm, final):
    n, d = x2d.shape
    dff = wup.shape[1]
    row = lambda i: (i, 0)
    const = lambda i: (0, 0)
    once = pl.Buffered(1)
    o_spec = pl.BlockSpec((tm, GROUP_WIDTH), row)
    return pl.pallas_call(
        functools.partial(_mlp_kernel, ff_chunk=1024, final=final),
        grid=(n // tm,),
        in_specs=[o_spec, o_spec, o_spec, o_spec,
                  pl.BlockSpec((tm, d), row),
                  pl.BlockSpec((d, d), const, pipeline_mode=once),
                  pl.BlockSpec((1, d), const),
                  pl.BlockSpec((d, dff), const, pipeline_mode=once),
                  pl.BlockSpec((dff, d), const, pipeline_mode=once),
                  pl.BlockSpec((1, d), const)],
        out_specs=pl.BlockSpec((tm, d), row),
        out_shape=jax.ShapeDtypeStruct((n, d), F32),
        compiler_params=pltpu.CompilerParams(
            dimension_semantics=("parallel",), vmem_limit_bytes=VMEM_LIMIT),
        name="outproj_mlp",
    )(oa, ob, oc, od, x2d, wo, g2, wup, wdn, gf)


def _rope_tables(seq):
    pos = jnp.arange(seq, dtype=F32)[:, None]
    lane = jnp.arange(LANES)

    def table(dim, active):
        half = dim // 2
        inv_freq = 1.0 / (ROPE_THETA ** (jnp.arange(0, half, dtype=F32) * 2.0 / dim))
        ang = pos * inv_freq[None, :][:, lane % half]
        sign = jnp.where((lane % dim) < half, -1.0, 1.0)
        cos = jnp.where(active, jnp.cos(ang), 1.0)
        sin = jnp.where(active, jnp.sin(ang) * sign, 0.0)
        return cos.astype(F32), sin.astype(F32)

    c64, s64 = table(HEAD_DIM, jnp.ones((LANES,), bool))
    c32, s32 = table(IDX_DIM, jnp.ones((LANES,), bool))
    cm, sm = table(MLA_ROPE, (lane >= MLA_NOPE) & (lane < MLA_NOPE + MLA_ROPE))
    return c64, s64, c32, s32, cm, sm


def _pack_weights(w_in, mla_w_uq, mla_w_ukv, w_out):
    depth, d, _ = w_in.shape
    splits = (MLA_Q_RANK, MLA_KV_RANK, MLA_ROPE,
              GROUP_WIDTH, SWA_KV_HEADS * HEAD_DIM, SWA_KV_HEADS * HEAD_DIM,
              GROUP_WIDTH, GROUP_WIDTH, GROUP_WIDTH, IDX_HEADS * IDX_DIM, IDX_DIM, IDX_HEADS,
              GROUP_WIDTH, GROUP_WIDTH, GROUP_WIDTH, N_HEADS)
    offs = [0]
    for s in splits:
        offs.append(offs[-1] + s)
    (a_cq, a_ckv, a_kr, b_q, b_k, b_v, c_q, c_k, c_v, c_qi, c_ki, c_w,
     d_q, d_k, d_v, d_f) = [w_in[:, :, offs[j]:offs[j + 1]] for j in range(len(splits))]
    zeros = lambda n: jnp.zeros((depth, d, n), w_in.dtype)
    swap = jnp.array([0, 2, 1, 3])
    a_kr = jnp.concatenate([zeros(MLA_NOPE), a_kr, zeros(LANES - MLA_NOPE - MLA_ROPE)], -1)
    b_q = b_q.reshape(depth, d, N_HEADS, HEAD_DIM)[:, :, swap].reshape(depth, d, GROUP_WIDTH)
    c_ki = jnp.concatenate([c_ki] * (LANES // IDX_DIM), -1)
    w_row = jnp.concatenate(
        [a_cq, a_ckv, a_kr, b_q, b_k, c_q, c_k, c_qi, c_ki, d_q, d_k], -1).astype(BF16)
    w_t = jnp.concatenate([c_v, d_v, b_v, c_w, d_f, zeros(MISC_ROWS - IDX_HEADS - N_HEADS)], -1)
    w_t = jnp.swapaxes(w_t, 1, 2).astype(BF16)

    pad = LANES - MLA_NOPE - MLA_ROPE
    wuq = mla_w_uq.reshape(depth, MLA_Q_RANK, N_HEADS, MLA_NOPE + MLA_ROPE)
    wuq = jnp.pad(wuq, ((0, 0), (0, 0), (0, 0), (0, pad))).reshape(depth, MLA_Q_RANK, N_HEADS * LANES)
    wukv = mla_w_ukv.reshape(depth, MLA_KV_RANK, N_HEADS, MLA_NOPE + HEAD_DIM)
    wuk = jnp.pad(wukv[..., :MLA_NOPE], ((0, 0), (0, 0), (0, 0), (0, LANES - MLA_NOPE)))
    wuk = wuk.reshape(depth, MLA_KV_RANK, N_HEADS * LANES)
    wuvt = jnp.swapaxes(wukv[..., MLA_NOPE:].reshape(depth, MLA_KV_RANK, GROUP_WIDTH), 1, 2)

    wo_b = w_out[:, GROUP_WIDTH:2 * GROUP_WIDTH].reshape(depth, N_HEADS, HEAD_DIM, -1)[:, swap]
    wo = jnp.concatenate([w_out[:, :GROUP_WIDTH], wo_b.reshape(depth, GROUP_WIDTH, -1),
                          w_out[:, 2 * GROUP_WIDTH:]], 1)
    return w_row, w_t, wuq.astype(BF16), wuk.astype(BF16), wuvt.astype(BF16), wo.astype(BF16)


_MLA_HEADS = tuple((h * LANES, None, h * LANES) for h in range(N_HEADS))
_FOX_HEADS = tuple(((h // 2) * LANES, h % 2, (h // 2) * LANES) for h in range(N_HEADS))


def kernel(x, norm1, w_in, mla_q_norm, mla_kv_norm, mla_w_uq, mla_w_ukv, swa_sinks, fox_b_f,
           w_out, norm2, w_up, w_down, final_norm):
    batch, seq, d = x.shape
    depth = w_in.shape[0]
    n = batch * seq
    top_k = min(DSA_TOPK, seq // 4)
    tm = min(512, seq)
    tq_flash = min(256, seq)
    ck_flash = min(256, seq)
    tq_swa = min(256, seq)
    tq_dsa = min(256, seq)
    ck_dsa = min(256, seq)

    tables = _rope_tables(seq)
    w_row, w_t, wuq, wuk, wuvt, wo = _pack_weights(w_in, mla_w_uq, mla_w_ukv, w_out)
    wup = w_up.astype(BF16)
    wdn = w_down.astype(BF16)

    x2d = x.reshape(n, d)
    for l in range(depth):
        (qa, ka, qb, kb, qc, kc, qi, ki, qd, kd, va_t, vc_t, vd_t, vb_t, misc_t) = _inproj(
            x2d, norm1[l][None], w_row[l], w_t[l], mla_q_norm[l][None], mla_kv_norm[l][None],
            wuq[l], wuk[l], wuvt[l], tables, batch=batch, seq=seq, tm=tm)
        r3 = lambda a: a.reshape(batch, seq, a.shape[-1])
        cum = _forget_cumsum(misc_t[:, IDX_HEADS:IDX_HEADS + N_HEADS], fox_b_f[l],
                             batch=batch, seq=seq)
        cum_col = jnp.swapaxes(cum, 1, 2)
        o_a = _flash(r3(qa), r3(ka), va_t, None, _MLA_HEADS, tq=tq_flash, ck=ck_flash)
        o_b = _swa(r3(qb), r3(kb), vb_t, swa_sinks[l], tq=tq_swa)
        o_c = _dsa(r3(qi), misc_t, r3(ki), r3(qc), r3(kc), vc_t, tq=tq_dsa, ck=ck_dsa, topk=top_k)
        o_d = _flash(r3(qd), r3(kd), vd_t, cum_col, _FOX_HEADS, tq=tq_flash, ck=ck_flash)
        x2d = _mlp(o_a.reshape(n, -1), o_b.reshape(n, -1), o_c.reshape(n, -1), o_d.reshape(n, -1),
                   x2d, wo[l], norm2[l][None], wup[l], wdn[l], final_norm[None],
                   tm=tm, final=(l == depth - 1))
    return x2d.reshape(batch, seq, d)
```

```python
import functools
import math

import jax
import jax.numpy as jnp
from jax import lax
from jax.experimental import pallas as pl
from jax.experimental.pallas import tpu as pltpu

F32 = jnp.float32
BF16 = jnp.bfloat16
I32 = jnp.int32

HEAD_DIM = 64
N_HEADS = 4
GROUP_WIDTH = N_HEADS * HEAD_DIM
MLA_Q_RANK = 256
MLA_KV_RANK = 128
MLA_NOPE = 64
MLA_ROPE = 32
SWA_KV_HEADS = 2
SWA_WINDOW = 128
IDX_HEADS = 8
IDX_DIM = 32
DSA_TOPK = 256
ROPE_THETA = 10000.0
EPS = 1e-6

LANES = 128
SUBLANES = 8
VMEM_LIMIT = 56 * 1024 * 1024
NEG = -1e30
INT_MIN = -(2 ** 31)
LOG2E = math.log2(math.e)
BISECT_STEPS = 18

_ROW_GROUPS = (
    ("a_cq", 256), ("a_ckv", 128), ("a_kr", 128),
    ("b_q", 256), ("b_k", 128),
    ("c_q", 256), ("c_k", 256), ("c_qi", 256), ("c_ki", 128),
    ("d_q", 256), ("d_k", 256),
)
_ROW_OFF = {}
_acc = 0
for _n, _w in _ROW_GROUPS:
    _ROW_OFF[_n] = (_acc, _acc + _w)
    _acc += _w
ROW_WIDTH = _acc
MISC_ROWS = 16
SWA_KV_WIDTH = SWA_KV_HEADS * HEAD_DIM
T_ROWS = 2 * GROUP_WIDTH + SWA_KV_WIDTH + MISC_ROWS


def _dot(a, b):
    return jnp.dot(a, b, preferred_element_type=F32)


def _dot_nt(a, b):
    return lax.dot_general(a, b, (((1,), (1,)), ((), ())), preferred_element_type=F32)


def _rms(x, g):
    return x * lax.rsqrt(jnp.mean(x * x, axis=-1, keepdims=True) + EPS) * g


def _rope(x, cos, sin, half):
    width = x.shape[1]
    reps = width // LANES
    if reps > 1:
        cos = jnp.concatenate([cos] * reps, axis=1)
        sin = jnp.concatenate([sin] * reps, axis=1)
    lane = lax.broadcasted_iota(I32, x.shape, 1)
    first = (lane % (2 * half)) < half
    rot = jnp.where(first, pltpu.roll(x, width - half, 1), pltpu.roll(x, half, 1))
    return x * cos + rot * sin


def _half_mask(q, half):
    lane = lax.broadcasted_iota(I32, q.shape, 1)
    return jnp.where((lane >= HEAD_DIM) == bool(half), q, jnp.zeros_like(q))


def _softmax_step(s, m, l8, mask):
    ck, tq = s.shape
    if mask is not None:
        s = jnp.where(mask, s, NEG)
    m_new = jnp.maximum(m, jnp.max(s, axis=0, keepdims=True))
    alpha = jnp.exp2(m - m_new)
    p = jnp.exp2(s - m_new)
    l8 = alpha * l8 + jnp.sum(p.reshape(ck // SUBLANES, SUBLANES, tq), axis=0)
    return p.astype(BF16), m_new, l8, alpha


def _inproj_kernel(x_ref, g1_ref, w_ref, wt_ref, qn_ref, kvn_ref, wuq_ref, wuk_ref, wuvt_ref,
                   c64_ref, s64_ref, c32_ref, s32_ref, cm_ref, sm_ref,
                   qa_ref, ka_ref, qb_ref, kb_ref,
                   qc_ref, kc_ref, qi_ref, ki_ref, qd_ref, kd_ref,
                   vat_ref, vct_ref, vdt_ref, vbt_ref, misct_ref):
    h = _rms(x_ref[...], g1_ref[...]).astype(BF16)

    def proj(name):
        lo, hi = _ROW_OFF[name]
        return _dot(h, w_ref[:, lo:hi])

    c64, s64 = c64_ref[...], s64_ref[...]
    c32, s32 = c32_ref[...], s32_ref[...]
    cm, sm = cm_ref[...], sm_ref[...]
    scale_a = (MLA_NOPE + MLA_ROPE) ** -0.5 * LOG2E
    scale = HEAD_DIM ** -0.5 * LOG2E

    cq = _rms(proj("a_cq"), qn_ref[...]).astype(BF16)
    qa = _rope(_dot(cq, wuq_ref[...]), cm, sm, MLA_ROPE // 2) * scale_a
    qa_ref[...] = qa.astype(BF16)
    ckv = _rms(proj("a_ckv"), kvn_ref[...]).astype(BF16)
    kr = _rope(proj("a_kr"), cm, sm, MLA_ROPE // 2)
    ka_ref[...] = (_dot(ckv, wuk_ref[...]) + jnp.concatenate([kr] * N_HEADS, axis=1)).astype(BF16)
    vat_ref[0] = _dot_nt(wuvt_ref[...], ckv).astype(BF16)

    qb_ref[...] = (_rope(proj("b_q"), c64, s64, HEAD_DIM // 2) * scale).astype(BF16)
    kb_ref[...] = _rope(proj("b_k"), c64, s64, HEAD_DIM // 2).astype(BF16)

    qc_ref[...] = (_rope(proj("c_q"), c64, s64, HEAD_DIM // 2) * scale).astype(BF16)
    kc_ref[...] = _rope(proj("c_k"), c64, s64, HEAD_DIM // 2).astype(BF16)
    qi_ref[...] = _rope(proj("c_qi"), c32, s32, IDX_DIM // 2).astype(BF16)
    ki_ref[...] = _rope(proj("c_ki"), c32, s32, IDX_DIM // 2).astype(BF16)

    qd_ref[...] = (proj("d_q") * scale).astype(BF16)
    kd_ref[...] = proj("d_k").astype(BF16)

    t = _dot_nt(wt_ref[...], h)
    vct_ref[0] = t[:GROUP_WIDTH].astype(BF16)
    vdt_ref[0] = t[GROUP_WIDTH:2 * GROUP_WIDTH].astype(BF16)
    vbt_ref[0] = t[2 * GROUP_WIDTH:2 * GROUP_WIDTH + SWA_KV_WIDTH].astype(BF16)
    misct_ref[0] = t[2 * GROUP_WIDTH + SWA_KV_WIDTH:]


def _inproj(x2d, g1, w_row, w_t, qn, kvn, wuq, wuk, wuvt, tables, *, batch, seq, tm):
    n = x2d.shape[0]
    d = x2d.shape[1]
    spb = seq // tm
    row = lambda i: (i, 0)
    const = lambda i: (0, 0)
    tab = lambda i: (i % spb, 0)
    tr = lambda i: (i // spb, 0, i % spb)
    once = pl.Buffered(1)

    def out2(width):
        return jax.ShapeDtypeStruct((n, width), BF16), pl.BlockSpec((tm, width), row)

    def out_t(rows, dtype):
        return jax.ShapeDtypeStruct((batch, rows, seq), dtype), pl.BlockSpec((1, rows, tm), tr)

    outs = [out2(512), out2(512),
            out2(256), out2(128),
            out2(256), out2(256), out2(256), out2(128),
            out2(256), out2(256),
            out_t(GROUP_WIDTH, BF16), out_t(GROUP_WIDTH, BF16), out_t(GROUP_WIDTH, BF16),
            out_t(SWA_KV_WIDTH, BF16), out_t(MISC_ROWS, F32)]
    in_specs = [
        pl.BlockSpec((tm, d), row),
        pl.BlockSpec((1, d), const),
        pl.BlockSpec((d, ROW_WIDTH), const, pipeline_mode=once),
        pl.BlockSpec((T_ROWS, d), const, pipeline_mode=once),
        pl.BlockSpec((1, MLA_Q_RANK), const),
        pl.BlockSpec((1, MLA_KV_RANK), const),
        pl.BlockSpec((MLA_Q_RANK, N_HEADS * LANES), const, pipeline_mode=once),
        pl.BlockSpec((MLA_KV_RANK, N_HEADS * LANES), const, pipeline_mode=once),
        pl.BlockSpec((GROUP_WIDTH, MLA_KV_RANK), const, pipeline_mode=once),
    ] + [pl.BlockSpec((tm, LANES), tab)] * 6
    return pl.pallas_call(
        _inproj_kernel,
        grid=(n // tm,),
        in_specs=in_specs,
        out_specs=[o[1] for o in outs],
        out_shape=[o[0] for o in outs],
        compiler_params=pltpu.CompilerParams(
            dimension_semantics=("parallel",), vmem_limit_bytes=VMEM_LIMIT),
        name="inproj",
    )(x2d, g1, w_row, w_t, qn, kvn, wuq, wuk, wuvt, *tables)


def _cumsum_kernel(f_ref, b_ref, o_ref, *, segs):
    x = f_ref[...] + b_ref[...]
    ls = jnp.minimum(x, 0.0) - jnp.log1p(jnp.exp(-jnp.abs(x)))
    rows = x.shape[0]
    r = lax.broadcasted_iota(I32, (LANES, LANES), 0)
    c = lax.broadcasted_iota(I32, (LANES, LANES), 1)
    upper = (r <= c).astype(F32)
    within = jnp.dot(ls, upper, preferred_element_type=F32, precision=lax.Precision.HIGHEST)
    tot = jnp.broadcast_to(within[:, LANES - 1:LANES], (rows, LANES))
    rr = lax.broadcasted_iota(I32, (rows, rows), 0)
    cc = lax.broadcasted_iota(I32, (rows, rows), 1)
    before = ((cc < rr) & (cc // segs == rr // segs)).astype(F32)
    off = jnp.dot(before, tot, preferred_element_type=F32, precision=lax.Precision.HIGHEST)
    o_ref[...] = (within + off) * LOG2E


def _forget_cumsum(f_t, bias, *, batch, seq):
    segs = seq // LANES
    rows = N_HEADS * segs
    f2 = f_t.reshape(batch * rows, LANES)
    b2 = jnp.broadcast_to(jnp.repeat(bias, segs)[:, None], (rows, LANES))
    out = pl.pallas_call(
        functools.partial(_cumsum_kernel, segs=segs),
        grid=(batch,),
        in_specs=[pl.BlockSpec((rows, LANES), lambda b: (b, 0)),
                  pl.BlockSpec((rows, LANES), lambda b: (0, 0))],
        out_specs=pl.BlockSpec((rows, LANES), lambda b: (b, 0)),
        out_shape=jax.ShapeDtypeStruct((batch * rows, LANES), F32),
        compiler_params=pltpu.CompilerParams(dimension_semantics=("parallel",)),
        name="forget_cumsum",
    )(f2, b2)
    return out.reshape(batch, N_HEADS, seq)


def _flash_kernel(*refs, heads, tq, ck, has_bias):
    if has_bias:
        q_ref, k_ref, vt_ref, b_ref, o_ref, acc_ref, sa_ref, sb_ref = refs
    else:
        q_ref, k_ref, vt_ref, o_ref, acc_ref, sa_ref, sb_ref = refs
        b_ref = None
    assert tq == ck
    nh = len(heads)
    i = pl.program_id(1)
    q_start = i * tq
    n_full = q_start // ck
    kpos_l = lax.broadcasted_iota(I32, (ck, tq), 0)
    qpos = q_start + lax.broadcasted_iota(I32, (ck, tq), 1)
    qs = []
    for ql, qhalf, _ in heads:
        q = q_ref[0, :, ql:ql + LANES]
        qs.append(q if qhalf is None else _half_mask(q, qhalf))
    acc_ref[...] = jnp.zeros(acc_ref.shape, F32)

    def scores(c, s_ref):
        off = pl.multiple_of(c * ck, ck)
        for h, (_, _, kl) in enumerate(heads):
            s = _dot_nt(k_ref[0, pl.ds(off, ck), kl:kl + LANES], qs[h])
            if has_bias:
                s = s - b_ref[0, pl.ds(off, ck), h:h + 1]
            s_ref[h] = s

    def consume(c, s_ref, carry, masked):
        ms, ls = carry
        off = pl.multiple_of(c * ck, ck)
        mask = (off + kpos_l <= qpos) if masked else None
        new_ms, new_ls = [], []
        for h in range(nh):
            p, m_new, l8, alpha = _softmax_step(s_ref[h], ms[h], ls[h], mask)
            vt = vt_ref[0, h * HEAD_DIM:(h + 1) * HEAD_DIM, pl.ds(off, ck)]
            acc_ref[h] = alpha * acc_ref[h] + _dot(vt, p)
            new_ms.append(m_new)
            new_ls.append(l8)
        return tuple(new_ms), tuple(new_ls)

    def pair_body(p, carry):
        c = 2 * p
        scores(c + 1, sb_ref)
        carry = consume(c, sa_ref, carry, False)
        scores(c + 2, sa_ref)
        return consume(c + 1, sb_ref, carry, False)

    carry = (tuple(jnp.full((1, tq), NEG, F32) for _ in range(nh)),
             tuple(jnp.zeros((SUBLANES, tq), F32) for _ in range(nh)))
    scores(0, sa_ref)
    carry = lax.fori_loop(0, n_full // 2, pair_body, carry)
    odd = n_full % 2 == 1

    def tail_odd(carry):
        scores(n_full, sb_ref)
        carry = consume(n_full - 1, sa_ref, carry, False)
        return consume(n_full, sb_ref, carry, True)

    carry = lax.cond(odd, tail_odd, lambda cr: consume(n_full, sa_ref, cr, True), carry)
    _, ls = carry
    for pair in range(nh // 2):
        o0 = acc_ref[2 * pair] / jnp.sum(ls[2 * pair], axis=0, keepdims=True)
        o1 = acc_ref[2 * pair + 1] / jnp.sum(ls[2 * pair + 1], axis=0, keepdims=True)
        o_t = jnp.concatenate([o0, o1], axis=0)
        o_ref[0, :, pair * LANES:(pair + 1) * LANES] = o_t.T.astype(o_ref.dtype)


def _flash(q, k, v_t, bias, heads, *, tq, ck):
    batch, seq, wq = q.shape
    wk = k.shape[2]
    has_bias = bias is not None
    in_specs = [pl.BlockSpec((1, tq, wq), lambda b, i: (b, i, 0)),
                pl.BlockSpec((1, seq, wk), lambda b, i: (b, 0, 0)),
                pl.BlockSpec((1, GROUP_WIDTH, seq), lambda b, i: (b, 0, 0))]
    args = [q, k, v_t]
    if has_bias:
        in_specs.append(pl.BlockSpec((1, seq, N_HEADS), lambda b, i: (b, 0, 0)))
        args.append(bias)
    return pl.pallas_call(
        functools.partial(_flash_kernel, heads=heads, tq=tq, ck=ck, has_bias=has_bias),
        grid=(batch, seq // tq),
        in_specs=in_specs,
        out_specs=pl.BlockSpec((1, tq, GROUP_WIDTH), lambda b, i: (b, i, 0)),
        out_shape=jax.ShapeDtypeStruct((batch, seq, GROUP_WIDTH), BF16),
        scratch_shapes=[pltpu.VMEM((len(heads), HEAD_DIM, tq), F32),
                        pltpu.VMEM((len(heads), ck, tq), F32),
                        pltpu.VMEM((len(heads), ck, tq), F32)],
        compiler_params=pltpu.CompilerParams(
            dimension_semantics=("parallel", "arbitrary"), vmem_limit_bytes=VMEM_LIMIT),
        name="flash_bias" if has_bias else "flash",
    )(*args)


def _swa_kernel(sink_ref, q_ref, k_ref, vt_ref, o_ref, *, tq, window):
    i = pl.program_id(1)
    span = 2 * window
    for sub in range(tq // window):
        q_start = i * tq + sub * window
        start = pl.multiple_of(jnp.maximum(q_start - window, 0), window)
        k = k_ref[0, pl.ds(start, span), :]
        kpos = start + lax.broadcasted_iota(I32, (span, window), 0)
        qpos = q_start + lax.broadcasted_iota(I32, (span, window), 1)
        valid = (kpos <= qpos) & (qpos - kpos < window)
        for pair in range(2):
            qp = q_ref[0, sub * window:(sub + 1) * window, pair * LANES:(pair + 1) * LANES]
            outs = []
            for half in range(2):
                head = pair + 2 * half
                sink = sink_ref[head] * LOG2E
                s = jnp.where(valid, _dot_nt(k, _half_mask(qp, half)), NEG)
                m = jnp.maximum(jnp.max(s, axis=0, keepdims=True), sink)
                p = jnp.exp2(s - m)
                denom = jnp.sum(p, axis=0, keepdims=True) + jnp.exp2(sink - m)
                vt = vt_ref[0, half * HEAD_DIM:(half + 1) * HEAD_DIM, pl.ds(start, span)]
                outs.append(_dot(vt, p.astype(BF16)) / denom)
            o_t = jnp.concatenate(outs, axis=0)
            o_ref[0, sub * window:(sub + 1) * window, pair * LANES:(pair + 1) * LANES] = (
                o_t.T.astype(o_ref.dtype))


def _swa(q, k, v_t, sinks, *, tq):
    batch, seq, _ = q.shape
    return pl.pallas_call(
        functools.partial(_swa_kernel, tq=tq, window=SWA_WINDOW),
        grid=(batch, seq // tq),
        in_specs=[pl.BlockSpec(memory_space=pltpu.SMEM),
                  pl.BlockSpec((1, tq, GROUP_WIDTH), lambda b, i: (b, i, 0)),
                  pl.BlockSpec((1, seq, LANES), lambda b, i: (b, 0, 0)),
                  pl.BlockSpec((1, SWA_KV_HEADS * HEAD_DIM, seq), lambda b, i: (b, 0, 0))],
        out_specs=pl.BlockSpec((1, tq, GROUP_WIDTH), lambda b, i: (b, i, 0)),
        out_shape=jax.ShapeDtypeStruct((batch, seq, GROUP_WIDTH), BF16),
        compiler_params=pltpu.CompilerParams(
            dimension_semantics=("parallel", "arbitrary"), vmem_limit_bytes=VMEM_LIMIT),
        name="swa",
    )(sinks, q, k, v_t)


def _dsa_kernel(qi_ref, w_ref, ki_ref, qc_ref, kc_ref, vct_ref, o_ref,
                keys_ref, acc_ref, sa_ref, sb_ref, *, tq, ck, topk, seq):
    assert tq == ck
    i = pl.program_id(1)
    q_start = i * tq
    n_chunks = (q_start + tq) // ck
    n_full = q_start // ck
    lane = lax.broadcasted_iota(I32, (tq, LANES), 1)
    kpos_l = lax.broadcasted_iota(I32, (ck, tq), 0)
    qpos = q_start + lax.broadcasted_iota(I32, (ck, tq), 1)
    qpos_row = q_start + lax.broadcasted_iota(I32, (1, tq), 1)
    msb = jnp.int32(INT_MIN)

    qm = []
    for h in range(IDX_HEADS):
        g, r = divmod(h, LANES // IDX_DIM)
        qg = qi_ref[0, :, g * LANES:(g + 1) * LANES]
        qm.append(jnp.where(lane // IDX_DIM == r, qg, jnp.zeros_like(qg)))
    w = w_ref[0]

    def flip(v):
        return v ^ ((v >> 31) & jnp.int32(0x7FFFFFFF))

    def group_reduce(fn, v):
        quarter = v.shape[0] // 4
        parts = [fn(v[j * quarter:(j + 1) * quarter].reshape(quarter // SUBLANES, SUBLANES, tq), axis=0)
                 for j in range(4)]
        return fn(jnp.stack([fn(jnp.stack(parts[:2]), axis=0), fn(jnp.stack(parts[2:]), axis=0)]),
                  axis=0)

    half_heads = IDX_HEADS // 2

    def index_dots(c, first, s_ref):
        ki = ki_ref[0, pl.ds(pl.multiple_of(c * ck, ck), ck), :]
        for j in range(half_heads):
            s_ref[j] = _dot_nt(ki, qm[first + j])

    def weighted_relu(first, s_ref):
        acc = None
        for j in range(half_heads):
            term = jnp.maximum(s_ref[j], 0.0) * w[first + j:first + j + 1, :]
            acc = term if acc is None else acc + term
        return acc

    def score_chunk(c, carry, last):
        kmax8, kmin8 = carry
        off = pl.multiple_of(c * ck, ck)
        index_dots(c, half_heads, sb_ref)
        acc = weighted_relu(0, sa_ref)
        if not last:
            index_dots(c + 1, 0, sa_ref)
        acc = acc + weighted_relu(half_heads, sb_ref)
        key = flip(lax.bitcast_convert_type(acc, I32))
        key_lo = key
        if last:
            causal = off + kpos_l <= qpos
            key_lo = jnp.where(causal, key, jnp.int32(2 ** 31 - 1))
            key = jnp.where(causal, key, msb)
        keys_ref[pl.ds(off, ck), :] = key
        return (jnp.maximum(kmax8, group_reduce(jnp.max, key)),
                jnp.minimum(kmin8, group_reduce(jnp.min, key_lo)))

    ext = (jnp.full((SUBLANES, tq), INT_MIN, I32), jnp.full((SUBLANES, tq), 2 ** 31 - 1, I32))
    index_dots(0, 0, sa_ref)
    ext = lax.fori_loop(0, n_full, lambda c, cr: score_chunk(c, cr, False), ext)
    ext = score_chunk(n_full, ext, True)
    kmax = jnp.max(ext[0], axis=0, keepdims=True)
    kmin = jnp.min(ext[1], axis=0, keepdims=True)

    cb = 2 * ck
    n_blocks = (n_chunks + 1) // 2

    @pl.when(n_chunks % 2 == 1)
    def _():
        keys_ref[pl.ds(pl.multiple_of(n_chunks * ck, ck), ck), :] = jnp.full((ck, tq), INT_MIN, I32)

    def key_blocks(fn, init):
        def body(b, carry):
            off = pl.multiple_of(b * cb, cb)
            return fn(keys_ref[pl.ds(off, cb), :], off, carry)
        return lax.fori_loop(0, n_blocks, body, init)

    def count(pred):
        part = key_blocks(
            lambda kk, off, part: part + group_reduce(jnp.sum, jnp.where(pred(kk, off), 1, 0)),
            jnp.zeros((SUBLANES, tq), I32))
        return jnp.sum(part, axis=0, keepdims=True)

    need = qpos_row >= topk

    def settle(state, cand_k, c, lo_k, hi_k):
        done, tie, t_res = state
        hit = c == topk
        fin = (done == 0) & (hit | (hi_k - lo_k == 1))
        t_res = jnp.where(fin, jnp.where(hit, cand_k, lo_k), t_res)
        tie = jnp.where(fin & ~hit, 1, tie)
        return jnp.where(fin, 1, done), tie, t_res

    def bisect_step(state):
        it, flags, lo_k, hi_k, c_hi = state
        cand_k = lo_k + lax.shift_right_logical(hi_k - lo_k, 1)
        c = count(lambda kk, off: kk >= cand_k)
        ge = c >= topk
        lo_k = jnp.where(ge, cand_k, lo_k)
        hi_k = jnp.where(ge, hi_k, cand_k)
        return it + 1, settle(flags, cand_k, c, lo_k, hi_k), lo_k, hi_k, jnp.where(ge, c_hi, c)

    def snap_step(state):
        it, flags, lo_k, hi_k, c_hi = state

        top8 = key_blocks(
            lambda kk, off, part: jnp.maximum(
                part, group_reduce(jnp.max, jnp.where(kk < hi_k, kk, msb))),
            jnp.full((SUBLANES, tq), INT_MIN, I32))
        cand_k = jnp.maximum(jnp.max(top8, axis=0, keepdims=True), lo_k)
        c = count(lambda kk, off: kk >= cand_k)
        ge = c >= topk
        lo_k = jnp.where(ge, cand_k, lo_k)
        hi_k = jnp.where(ge, cand_k + 1, cand_k)
        return it + 1, settle(flags, cand_k, c, lo_k, hi_k), lo_k, hi_k, jnp.where(ge, c_hi, c)

    def unsettled(state):
        return jnp.min(state[1][0]) == 0

    lo0, hi0 = kmin, kmax + 1
    flat0 = need & (hi0 - lo0 == 1)
    flags = (jnp.where(need & ~flat0, 0, 1), jnp.where(flat0, 1, 0), jnp.where(need, lo0, msb + 1))
    state = (jnp.int32(0), flags, lo0, hi0, jnp.zeros((1, tq), I32))
    state = lax.fori_loop(0, BISECT_STEPS, lambda _, st: bisect_step(st), state)
    state = lax.while_loop(unsettled, lambda st: bisect_step(snap_step(st)), state)
    _, (_, tie_flag, t_s), _, _, n_gt = state
    tie = tie_flag > 0

    @pl.when(jnp.max(jnp.where(tie, 1, 0)) > 0)
    def _():
        want = (topk - n_gt).astype(F32)
        tri = (lax.broadcasted_iota(I32, (ck, ck), 0) > lax.broadcasted_iota(I32, (ck, ck), 1))
        tri = jnp.where(tri, 1.0, 0.0).astype(BF16)

        def demote(c, before):
            off = pl.multiple_of(c * ck, ck)
            kk = keys_ref[pl.ds(off, ck), :]
            tied = tie & (kk == t_s)
            ind = jnp.where(tied, 1.0, 0.0)
            rank = before + _dot(tri, ind.astype(BF16))
            keys_ref[pl.ds(off, ck), :] = jnp.where(tied & (rank >= want), msb, kk)
            return before + jnp.sum(ind, axis=0, keepdims=True)

        lax.fori_loop(0, n_chunks, demote, jnp.zeros((1, tq), F32))

    t_fin = jnp.maximum(t_s, msb + 1)
    qs = []
    for h in range(N_HEADS):
        pair, half = divmod(h, 2)
        qs.append(_half_mask(qc_ref[0, :, pair * LANES:(pair + 1) * LANES], half))
    acc_ref[...] = jnp.zeros(acc_ref.shape, F32)

    def scores(c, s_ref):
        off = pl.multiple_of(c * ck, ck)
        for h in range(N_HEADS):
            k = kc_ref[0, pl.ds(off, ck), (h // 2) * LANES:(h // 2 + 1) * LANES]
            s_ref[h] = _dot_nt(k, qs[h])

    def consume(c, s_ref, carry):
        ms, ls = carry
        off = pl.multiple_of(c * ck, ck)
        sel = keys_ref[pl.ds(off, ck), :] >= t_fin
        new_ms, new_ls = [], []
        for h in range(N_HEADS):
            pair = h // 2
            p, m_new, l8, alpha = _softmax_step(s_ref[h], ms[h], ls[h], sel)
            vt = vct_ref[0, h * HEAD_DIM:(h + 1) * HEAD_DIM, pl.ds(off, ck)]
            acc_ref[h] = alpha * acc_ref[h] + _dot(vt, p)
            new_ms.append(m_new)
            new_ls.append(l8)
        return tuple(new_ms), tuple(new_ls)

    def pair_body(p, carry):
        c = 2 * p
        scores(c + 1, sb_ref)
        carry = consume(c, sa_ref, carry)
        scores(c + 2, sa_ref)
        return consume(c + 1, sb_ref, carry)

    carry = (tuple(jnp.full((1, tq), NEG, F32) for _ in range(N_HEADS)),
             tuple(jnp.zeros((SUBLANES, tq), F32) for _ in range(N_HEADS)))
    scores(0, sa_ref)
    carry = lax.fori_loop(0, n_full // 2, pair_body, carry)

    def tail_odd(carry):
        scores(n_full, sb_ref)
        return consume(n_full, sb_ref, consume(n_full - 1, sa_ref, carry))

    _, ls = lax.cond(n_full % 2 == 1, tail_odd, lambda cr: consume(n_full, sa_ref, cr), carry)
    for pair in range(2):
        o0 = acc_ref[2 * pair] / jnp.sum(ls[2 * pair], axis=0, keepdims=True)
        o1 = acc_ref[2 * pair + 1] / jnp.sum(ls[2 * pair + 1], axis=0, keepdims=True)
        o_t = jnp.concatenate([o0, o1], axis=0)
        o_ref[0, :, pair * LANES:(pair + 1) * LANES] = o_t.T.astype(o_ref.dtype)


def _dsa(qi, misc_t, ki, qc, kc, vc_t, *, tq, ck, topk):
    batch, seq, _ = qc.shape
    return pl.pallas_call(
        functools.partial(_dsa_kernel, tq=tq, ck=ck, topk=topk, seq=seq),
        grid=(batch, seq // tq),
        in_specs=[pl.BlockSpec((1, tq, IDX_HEADS * IDX_DIM), lambda b, i: (b, i, 0)),
                  pl.BlockSpec((1, MISC_ROWS, tq), lambda b, i: (b, 0, i)),
                  pl.BlockSpec((1, seq, LANES), lambda b, i: (b, 0, 0)),
                  pl.BlockSpec((1, tq, GROUP_WIDTH), lambda b, i: (b, i, 0)),
                  pl.BlockSpec((1, seq, GROUP_WIDTH), lambda b, i: (b, 0, 0)),
                  pl.BlockSpec((1, GROUP_WIDTH, seq), lambda b, i: (b, 0, 0))],
        out_specs=pl.BlockSpec((1, tq, GROUP_WIDTH), lambda b, i: (b, i, 0)),
        out_shape=jax.ShapeDtypeStruct((batch, seq, GROUP_WIDTH), BF16),
        scratch_shapes=[pltpu.VMEM((seq, tq), I32),
                        pltpu.VMEM((N_HEADS, HEAD_DIM, tq), F32),
                        pltpu.VMEM((N_HEADS, ck, tq), F32),
                        pltpu.VMEM((N_HEADS, ck, tq), F32)],
        compiler_params=pltpu.CompilerParams(
            dimension_semantics=("parallel", "arbitrary"), vmem_limit_bytes=VMEM_LIMIT),
        name="dsa",
    )(qi, misc_t, ki, qc, kc, vc_t)


def _mlp_kernel(oa_ref, ob_ref, oc_ref, od_ref, x_ref, wo_ref, g2_ref, wup_ref, wdn_ref,
                gf_ref, out_ref, *, ff_chunk, final):
    gw = GROUP_WIDTH
    mix = None
    for g, o_ref in enumerate((oa_ref, ob_ref, oc_ref, od_ref)):
        y = _dot(o_ref[...], wo_ref[g * gw:(g + 1) * gw, :])
        mix = y if mix is None else mix + y
    x1 = x_ref[...] + mix
    h2 = _rms(x1, g2_ref[...]).astype(BF16)
    ffn = None
    for c in range(wup_ref.shape[1] // ff_chunk):
        u = _dot(h2, wup_ref[:, c * ff_chunk:(c + 1) * ff_chunk])
        a = jnp.square(jnp.maximum(u, 0.0)).astype(BF16)
        y = _dot(a, wdn_ref[c * ff_chunk:(c + 1) * ff_chunk, :])
        ffn = y if ffn is None else ffn + y
    acc = x1 + ffn
    if final:
        acc = _rms(acc, gf_ref[...])
    out_ref[...] = acc


def _mlp(oa, ob, oc, od, x2d, wo, g2, wup, wdn, gf, *, tm, final):
    n, d = x2d.shape
    dff = wup.shape[1]
    row = lambda i: (i, 0)
    const = lambda i: (0, 0)
    once = pl.Buffered(1)
    o_spec = pl.BlockSpec((tm, GROUP_WIDTH), row)
    return pl.pallas_call(
        functools.partial(_mlp_kernel, ff_chunk=1024, final=final),
        grid=(n // tm,),
        in_specs=[o_spec, o_spec, o_spec, o_spec,
                  pl.BlockSpec((tm, d), row),
                  pl.BlockSpec((d, d), const, pipeline_mode=once),
                  pl.BlockSpec((1, d), const),
                  pl.BlockSpec((d, dff), const, pipeline_mode=once),
                  pl.BlockSpec((dff, d), const, pipeline_mode=once),
                  pl.BlockSpec((1, d), const)],
        out_specs=pl.BlockSpec((tm, d), row),
        out_shape=jax.ShapeDtypeStruct((n, d), F32),
        compiler_params=pltpu.CompilerParams(
            dimension_semantics=("parallel",), vmem_limit_bytes=VMEM_LIMIT),
        name="outproj_mlp",
    )(oa, ob, oc, od, x2d, wo, g2, wup, wdn, gf)


def _rope_tables(seq):
    pos = jnp.arange(seq, dtype=F32)[:, None]
    lane = jnp.arange(LANES)

    def table(dim, active):
        half = dim // 2
        inv_freq = 1.0 / (ROPE_THETA ** (jnp.arange(0, half, dtype=F32) * 2.0 / dim))
        ang = pos * inv_freq[None, :][:, lane % half]
        sign = jnp.where((lane % dim) < half, -1.0, 1.0)
        cos = jnp.where(active, jnp.cos(ang), 1.0)
        sin = jnp.where(active, jnp.sin(ang) * sign, 0.0)
        return cos.astype(F32), sin.astype(F32)

    c64, s64 = table(HEAD_DIM, jnp.ones((LANES,), bool))
    c32, s32 = table(IDX_DIM, jnp.ones((LANES,), bool))
    cm, sm = table(MLA_ROPE, (lane >= MLA_NOPE) & (lane < MLA_NOPE + MLA_ROPE))
    return c64, s64, c32, s32, cm, sm


def _pack_weights(w_in, mla_w_uq, mla_w_ukv, w_out):
    depth, d, _ = w_in.shape
    splits = (MLA_Q_RANK, MLA_KV_RANK, MLA_ROPE,
              GROUP_WIDTH, SWA_KV_HEADS * HEAD_DIM, SWA_KV_HEADS * HEAD_DIM,
              GROUP_WIDTH, GROUP_WIDTH, GROUP_WIDTH, IDX_HEADS * IDX_DIM, IDX_DIM, IDX_HEADS,
              GROUP_WIDTH, GROUP_WIDTH, GROUP_WIDTH, N_HEADS)
    offs = [0]
    for s in splits:
        offs.append(offs[-1] + s)
    (a_cq, a_ckv, a_kr, b_q, b_k, b_v, c_q, c_k, c_v, c_qi, c_ki, c_w,
     d_q, d_k, d_v, d_f) = [w_in[:, :, offs[j]:offs[j + 1]] for j in range(len(splits))]
    zeros = lambda n: jnp.zeros((depth, d, n), w_in.dtype)
    swap = jnp.array([0, 2, 1, 3])
    a_kr = jnp.concatenate([zeros(MLA_NOPE), a_kr, zeros(LANES - MLA_NOPE - MLA_ROPE)], -1)
    b_q = b_q.reshape(depth, d, N_HEADS, HEAD_DIM)[:, :, swap].reshape(depth, d, GROUP_WIDTH)
    c_ki = jnp.concatenate([c_ki] * (LANES // IDX_DIM), -1)
    w_row = jnp.concatenate(
        [a_cq, a_ckv, a_kr, b_q, b_k, c_q, c_k, c_qi, c_ki, d_q, d_k], -1).astype(BF16)
    w_t = jnp.concatenate([c_v, d_v, b_v, c_w, d_f, zeros(MISC_ROWS - IDX_HEADS - N_HEADS)], -1)
    w_t = jnp.swapaxes(w_t, 1, 2).astype(BF16)

    pad = LANES - MLA_NOPE - MLA_ROPE
    wuq = mla_w_uq.reshape(depth, MLA_Q_RANK, N_HEADS, MLA_NOPE + MLA_ROPE)
    wuq = jnp.pad(wuq, ((0, 0), (0, 0), (0, 0), (0, pad))).reshape(depth, MLA_Q_RANK, N_HEADS * LANES)
    wukv = mla_w_ukv.reshape(depth, MLA_KV_RANK, N_HEADS, MLA_NOPE + HEAD_DIM)
    wuk = jnp.pad(wukv[..., :MLA_NOPE], ((0, 0), (0, 0), (0, 0), (0, LANES - MLA_NOPE)))
    wuk = wuk.reshape(depth, MLA_KV_RANK, N_HEADS * LANES)
    wuvt = jnp.swapaxes(wukv[..., MLA_NOPE:].reshape(depth, MLA_KV_RANK, GROUP_WIDTH), 1, 2)

    wo_b = w_out[:, GROUP_WIDTH:2 * GROUP_WIDTH].reshape(depth, N_HEADS, HEAD_DIM, -1)[:, swap]
    wo = jnp.concatenate([w_out[:, :GROUP_WIDTH], wo_b.reshape(depth, GROUP_WIDTH, -1),
                          w_out[:, 2 * GROUP_WIDTH:]], 1)
    return w_row, w_t, wuq.astype(BF16), wuk.astype(BF16), wuvt.astype(BF16), wo.astype(BF16)


_MLA_HEADS = tuple((h * LANES, None, h * LANES) for h in range(N_HEADS))
_FOX_HEADS = tuple(((h // 2) * LANES, h % 2, (h // 2) * LANES) for h in range(N_HEADS))


def kernel(x, norm1, w_in, mla_q_norm, mla_kv_norm, mla_w_uq, mla_w_ukv, swa_sinks, fox_b_f,
           w_out, norm2, w_up, w_down, final_norm):
    batch, seq, d = x.shape
    depth = w_in.shape[0]
    n = batch * seq
    top_k = min(DSA_TOPK, seq // 4)
    tm = min(512, seq)
    tq_flash = min(256, seq)
    ck_flash = min(256, seq)
    tq_swa = min(512, seq)
    tq_dsa = min(256, seq)
    ck_dsa = min(256, seq)

    tables = _rope_tables(seq)
    w_row, w_t, wuq, wuk, wuvt, wo = _pack_weights(w_in, mla_w_uq, mla_w_ukv, w_out)
    wup = w_up.astype(BF16)
    wdn = w_down.astype(BF16)

    x2d = x.reshape(n, d)
    for l in range(depth):
        (qa, ka, qb, kb, qc, kc, qi, ki, qd, kd, va_t, vc_t, vd_t, vb_t, misc_t) = _inproj(
            x2d, norm1[l][None], w_row[l], w_t[l], mla_q_norm[l][None], mla_kv_norm[l][None],
            wuq[l], wuk[l], wuvt[l], tables, batch=batch, seq=seq, tm=tm)
        r3 = lambda a: a.reshape(batch, seq, a.shape[-1])
        cum = _forget_cumsum(misc_t[:, IDX_HEADS:IDX_HEADS + N_HEADS], fox_b_f[l],
                             batch=batch, seq=seq)
        cum_col = jnp.swapaxes(cum, 1, 2)
        o_a = _flash(r3(qa), r3(ka), va_t, None, _MLA_HEADS, tq=tq_flash, ck=ck_flash)
        o_b = _swa(r3(qb), r3(kb), vb_t, swa_sinks[l], tq=tq_swa)
        o_c = _dsa(r3(qi), misc_t, r3(ki), r3(qc), r3(kc), vc_t, tq=tq_dsa, ck=ck_dsa, topk=top_k)
        o_d = _flash(r3(qd), r3(kd), vd_t, cum_col, _FOX_HEADS, tq=tq_flash, ck=ck_flash)
        x2d = _mlp(o_a.reshape(n, -1), o_b.reshape(n, -1), o_c.reshape(n, -1), o_d.reshape(n, -1),
                   x2d, wo[l], norm2[l][None], wup[l], wdn[l], final_norm[None],
                   tm=tm, final=(l == depth - 1))
    return x2d.reshape(batch, seq, d)
```

```python
import functools
import math

import jax
import jax.numpy as jnp
from jax import lax
from jax.experimental import pallas as pl
from jax.experimental.pallas import tpu as pltpu

F32 = jnp.float32
BF16 = jnp.bfloat16
I32 = jnp.int32

HEAD_DIM = 64
N_HEADS = 4
GROUP_WIDTH = N_HEADS * HEAD_DIM
MLA_Q_RANK = 256
MLA_KV_RANK = 128
MLA_NOPE = 64
MLA_ROPE = 32
SWA_KV_HEADS = 2
SWA_WINDOW = 128
IDX_HEADS = 8
IDX_DIM = 32
DSA_TOPK = 256
ROPE_THETA = 10000.0
EPS = 1e-6

LANES = 128
SUBLANES = 8
VMEM_LIMIT = 56 * 1024 * 1024
NEG = -1e30
INT_MIN = -(2 ** 31)
LOG2E = math.log2(math.e)
BISECT_STEPS = 20

_ROW_GROUPS = (
    ("a_cq", 256), ("a_ckv", 128), ("a_kr", 128),
    ("b_q", 256), ("b_k", 128), ("c_ki", 128),
    ("c_q", 256), ("c_k", 256), ("c_qi", 256),
    ("d_q", 256), ("d_k", 256),
)
_ROW_OFF = {}
_acc = 0
for _n, _w in _ROW_GROUPS:
    _ROW_OFF[_n] = (_acc, _acc + _w)
    _acc += _w
ROW_WIDTH = _acc
MISC_ROWS = 16
SWA_KV_WIDTH = SWA_KV_HEADS * HEAD_DIM
T_ROWS = 2 * GROUP_WIDTH + SWA_KV_WIDTH + MISC_ROWS


def _dot(a, b):
    return jnp.dot(a, b, preferred_element_type=F32)


def _dot_nt(a, b):
    return lax.dot_general(a, b, (((1,), (1,)), ((), ())), preferred_element_type=F32)


def _rms(x, g):
    return x * lax.rsqrt(jnp.mean(x * x, axis=-1, keepdims=True) + EPS) * g


def _rope(x, cos, sin, half):
    width = x.shape[1]
    reps = width // LANES
    if reps > 1:
        cos = jnp.concatenate([cos] * reps, axis=1)
        sin = jnp.concatenate([sin] * reps, axis=1)
    lane = lax.broadcasted_iota(I32, x.shape, 1)
    first = (lane % (2 * half)) < half
    rot = jnp.where(first, pltpu.roll(x, width - half, 1), pltpu.roll(x, half, 1))
    return x * cos + rot * sin


def _half_mask(q, half):
    lane = lax.broadcasted_iota(I32, q.shape, 1)
    return jnp.where((lane >= HEAD_DIM) == bool(half), q, jnp.zeros_like(q))


def _softmax_step(s, m, l8, mask):
    ck, tq = s.shape
    if mask is not None:
        s = jnp.where(mask, s, NEG)
    m_new = jnp.maximum(m, jnp.max(s, axis=0, keepdims=True))
    alpha = jnp.exp2(m - m_new)
    p = jnp.exp2(s - m_new)
    l8 = alpha * l8 + jnp.sum(p.reshape(ck // SUBLANES, SUBLANES, tq), axis=0)
    return p.astype(BF16), m_new, l8, alpha


def _inproj_kernel(x_ref, g1_ref, w_ref, wt_ref, qn_ref, kvn_ref, wuq_ref, wuk_ref, wuvt_ref,
                   c64_ref, s64_ref, c32_ref, s32_ref, cm_ref, sm_ref,
                   qa_ref, ka_ref, qb_ref, kb_ref,
                   qc_ref, kc_ref, qi_ref, ki_ref, qd_ref, kd_ref,
                   vat_ref, vct_ref, vdt_ref, vbt_ref, misct_ref):
    h = _rms(x_ref[...], g1_ref[...]).astype(BF16)

    def proj(name):
        lo, hi = _ROW_OFF[name]
        return _dot(h, w_ref[:, lo:hi])

    def proj_pair(name0, name1):
        lo, mid = _ROW_OFF[name0]
        assert _ROW_OFF[name1][0] == mid
        both = _dot(h, w_ref[:, lo:_ROW_OFF[name1][1]])
        return both[:, :mid - lo], both[:, mid - lo:]

    c64, s64 = c64_ref[...], s64_ref[...]
    c32, s32 = c32_ref[...], s32_ref[...]
    cm, sm = cm_ref[...], sm_ref[...]
    scale_a = (MLA_NOPE + MLA_ROPE) ** -0.5 * LOG2E
    scale = HEAD_DIM ** -0.5 * LOG2E

    cq = _rms(proj("a_cq"), qn_ref[...]).astype(BF16)
    qa = _rope(_dot(cq, wuq_ref[...]), cm, sm, MLA_ROPE // 2) * scale_a
    qa_ref[...] = qa.astype(BF16)
    ckv, kr = proj_pair("a_ckv", "a_kr")
    ckv = _rms(ckv, kvn_ref[...]).astype(BF16)
    kr = _rope(kr, cm, sm, MLA_ROPE // 2)
    ka_ref[...] = (_dot(ckv, wuk_ref[...]) + jnp.concatenate([kr] * N_HEADS, axis=1)).astype(BF16)
    vat_ref[0] = _dot_nt(wuvt_ref[...], ckv).astype(BF16)

    qb_ref[...] = (_rope(proj("b_q"), c64, s64, HEAD_DIM // 2) * scale).astype(BF16)
    kb, ki = proj_pair("b_k", "c_ki")
    kb_ref[...] = _rope(kb, c64, s64, HEAD_DIM // 2).astype(BF16)

    qc_ref[...] = (_rope(proj("c_q"), c64, s64, HEAD_DIM // 2) * scale).astype(BF16)
    kc_ref[...] = _rope(proj("c_k"), c64, s64, HEAD_DIM // 2).astype(BF16)
    qi_ref[...] = _rope(proj("c_qi"), c32, s32, IDX_DIM // 2).astype(BF16)
    ki_ref[...] = _rope(ki, c32, s32, IDX_DIM // 2).astype(BF16)

    qd_ref[...] = (proj("d_q") * scale).astype(BF16)
    kd_ref[...] = proj("d_k").astype(BF16)

    t = _dot_nt(wt_ref[...], h)
    vct_ref[0] = t[:GROUP_WIDTH].astype(BF16)
    vdt_ref[0] = t[GROUP_WIDTH:2 * GROUP_WIDTH].astype(BF16)
    vbt_ref[0] = t[2 * GROUP_WIDTH:2 * GROUP_WIDTH + SWA_KV_WIDTH].astype(BF16)
    misct_ref[0] = t[2 * GROUP_WIDTH + SWA_KV_WIDTH:]


def _inproj(x2d, g1, w_row, w_t, qn, kvn, wuq, wuk, wuvt, tables, *, batch, seq, tm):
    n = x2d.shape[0]
    d = x2d.shape[1]
    spb = seq // tm
    row = lambda i: (i, 0)
    const = lambda i: (0, 0)
    tab = lambda i: (i % spb, 0)
    tr = lambda i: (i // spb, 0, i % spb)
    once = pl.Buffered(1)

    def out2(width):
        return jax.ShapeDtypeStruct((n, width), BF16), pl.BlockSpec((tm, width), row)

    def out_t(rows, dtype):
        return jax.ShapeDtypeStruct((batch, rows, seq), dtype), pl.BlockSpec((1, rows, tm), tr)

    outs = [out2(512), out2(512),
            out2(256), out2(128),
            out2(256), out2(256), out2(256), out2(128),
            out2(256), out2(256),
            out_t(GROUP_WIDTH, BF16), out_t(GROUP_WIDTH, BF16), out_t(GROUP_WIDTH, BF16),
            out_t(SWA_KV_WIDTH, BF16), out_t(MISC_ROWS, F32)]
    in_specs = [
        pl.BlockSpec((tm, d), row),
        pl.BlockSpec((1, d), const),
        pl.BlockSpec((d, ROW_WIDTH), const, pipeline_mode=once),
        pl.BlockSpec((T_ROWS, d), const, pipeline_mode=once),
        pl.BlockSpec((1, MLA_Q_RANK), const),
        pl.BlockSpec((1, MLA_KV_RANK), const),
        pl.BlockSpec((MLA_Q_RANK, N_HEADS * LANES), const, pipeline_mode=once),
        pl.BlockSpec((MLA_KV_RANK, N_HEADS * LANES), const, pipeline_mode=once),
        pl.BlockSpec((GROUP_WIDTH, MLA_KV_RANK), const, pipeline_mode=once),
    ] + [pl.BlockSpec((tm, LANES), tab)] * 6
    return pl.pallas_call(
        _inproj_kernel,
        grid=(n // tm,),
        in_specs=in_specs,
        out_specs=[o[1] for o in outs],
        out_shape=[o[0] for o in outs],
        compiler_params=pltpu.CompilerParams(
            dimension_semantics=("parallel",), vmem_limit_bytes=VMEM_LIMIT),
        name="inproj",
    )(x2d, g1, w_row, w_t, qn, kvn, wuq, wuk, wuvt, *tables)


def _cumsum_kernel(f_ref, b_ref, o_ref, *, segs):
    x = f_ref[...] + b_ref[...]
    ls = jnp.minimum(x, 0.0) - jnp.log1p(jnp.exp(-jnp.abs(x)))
    rows = x.shape[0]
    r = lax.broadcasted_iota(I32, (LANES, LANES), 0)
    c = lax.broadcasted_iota(I32, (LANES, LANES), 1)
    upper = (r <= c).astype(F32)
    within = jnp.dot(ls, upper, preferred_element_type=F32, precision=lax.Precision.HIGHEST)
    tot = jnp.broadcast_to(within[:, LANES - 1:LANES], (rows, LANES))
    rr = lax.broadcasted_iota(I32, (rows, rows), 0)
    cc = lax.broadcasted_iota(I32, (rows, rows), 1)
    before = ((cc < rr) & (cc // segs == rr // segs)).astype(F32)
    off = jnp.dot(before, tot, preferred_element_type=F32, precision=lax.Precision.HIGHEST)
    o_ref[...] = (within + off) * LOG2E


def _forget_cumsum(f_t, bias, *, batch, seq):
    segs = seq // LANES
    rows = N_HEADS * segs
    f2 = f_t.reshape(batch * rows, LANES)
    b2 = jnp.broadcast_to(jnp.repeat(bias, segs)[:, None], (rows, LANES))
    out = pl.pallas_call(
        functools.partial(_cumsum_kernel, segs=segs),
        grid=(batch,),
        in_specs=[pl.BlockSpec((rows, LANES), lambda b: (b, 0)),
                  pl.BlockSpec((rows, LANES), lambda b: (0, 0))],
        out_specs=pl.BlockSpec((rows, LANES), lambda b: (b, 0)),
        out_shape=jax.ShapeDtypeStruct((batch * rows, LANES), F32),
        compiler_params=pltpu.CompilerParams(dimension_semantics=("parallel",)),
        name="forget_cumsum",
    )(f2, b2)
    return out.reshape(batch, N_HEADS, seq)


def _flash_kernel(*refs, heads, tq, ck, has_bias):
    if has_bias:
        q_ref, k_ref, vt_ref, b_ref, o_ref, acc_ref, sa_ref, sb_ref = refs
    else:
        q_ref, k_ref, vt_ref, o_ref, acc_ref, sa_ref, sb_ref = refs
        b_ref = None
    assert tq == ck
    nh = len(heads)
    i = pl.program_id(1)
    q_start = i * tq
    n_full = q_start // ck
    kpos_l = lax.broadcasted_iota(I32, (ck, tq), 0)
    qpos = q_start + lax.broadcasted_iota(I32, (ck, tq), 1)
    qs = []
    for ql, qhalf, _ in heads:
        q = q_ref[0, :, ql:ql + LANES]
        qs.append(q if qhalf is None else _half_mask(q, qhalf))
    acc_ref[...] = jnp.zeros(acc_ref.shape, F32)

    def scores(c, s_ref):
        off = pl.multiple_of(c * ck, ck)
        for h, (_, _, kl) in enumerate(heads):
            s = _dot_nt(k_ref[0, pl.ds(off, ck), kl:kl + LANES], qs[h])
            if has_bias:
                s = s - b_ref[0, pl.ds(off, ck), h:h + 1]
            s_ref[h] = s

    def consume(c, s_ref, carry, masked):
        ms, ls = carry
        off = pl.multiple_of(c * ck, ck)
        mask = (off + kpos_l <= qpos) if masked else None
        new_ms, new_ls = [], []
        for h in range(nh):
            p, m_new, l8, alpha = _softmax_step(s_ref[h], ms[h], ls[h], mask)
            vt = vt_ref[0, h * HEAD_DIM:(h + 1) * HEAD_DIM, pl.ds(off, ck)]
            acc_ref[h] = alpha * acc_ref[h] + _dot(vt, p)
            new_ms.append(m_new)
            new_ls.append(l8)
        return tuple(new_ms), tuple(new_ls)

    def pair_body(p, carry):
        c = 2 * p
        scores(c + 1, sb_ref)
        carry = consume(c, sa_ref, carry, False)
        scores(c + 2, sa_ref)
        return consume(c + 1, sb_ref, carry, False)

    carry = (tuple(jnp.full((1, tq), NEG, F32) for _ in range(nh)),
             tuple(jnp.zeros((SUBLANES, tq), F32) for _ in range(nh)))
    scores(0, sa_ref)
    carry = lax.fori_loop(0, n_full // 2, pair_body, carry)
    odd = n_full % 2 == 1

    def tail_odd(carry):
        scores(n_full, sb_ref)
        carry = consume(n_full - 1, sa_ref, carry, False)
        return consume(n_full, sb_ref, carry, True)

    carry = lax.cond(odd, tail_odd, lambda cr: consume(n_full, sa_ref, cr, True), carry)
    _, ls = carry
    for pair in range(nh // 2):
        o0 = acc_ref[2 * pair] / jnp.sum(ls[2 * pair], axis=0, keepdims=True)
        o1 = acc_ref[2 * pair + 1] / jnp.sum(ls[2 * pair + 1], axis=0, keepdims=True)
        o_t = jnp.concatenate([o0, o1], axis=0)
        o_ref[0, :, pair * LANES:(pair + 1) * LANES] = o_t.T.astype(o_ref.dtype)


def _flash(q, k, v_t, bias, heads, *, tq, ck):
    batch, seq, wq = q.shape
    wk = k.shape[2]
    has_bias = bias is not None
    in_specs = [pl.BlockSpec((1, tq, wq), lambda b, i: (b, i, 0)),
                pl.BlockSpec((1, seq, wk), lambda b, i: (b, 0, 0)),
                pl.BlockSpec((1, GROUP_WIDTH, seq), lambda b, i: (b, 0, 0))]
    args = [q, k, v_t]
    if has_bias:
        in_specs.append(pl.BlockSpec((1, seq, N_HEADS), lambda b, i: (b, 0, 0)))
        args.append(bias)
    return pl.pallas_call(
        functools.partial(_flash_kernel, heads=heads, tq=tq, ck=ck, has_bias=has_bias),
        grid=(batch, seq // tq),
        in_specs=in_specs,
        out_specs=pl.BlockSpec((1, tq, GROUP_WIDTH), lambda b, i: (b, i, 0)),
        out_shape=jax.ShapeDtypeStruct((batch, seq, GROUP_WIDTH), BF16),
        scratch_shapes=[pltpu.VMEM((len(heads), HEAD_DIM, tq), F32),
                        pltpu.VMEM((len(heads), ck, tq), F32),
                        pltpu.VMEM((len(heads), ck, tq), F32)],
        compiler_params=pltpu.CompilerParams(
            dimension_semantics=("parallel", "arbitrary"), vmem_limit_bytes=VMEM_LIMIT),
        name="flash_bias" if has_bias else "flash",
    )(*args)


def _swa_kernel(sink_ref, q_ref, k_ref, vt_ref, o_ref, *, tq, window):
    i = pl.program_id(1)
    span = 2 * window
    for sub in range(tq // window):
        q_start = i * tq + sub * window
        start = pl.multiple_of(jnp.maximum(q_start - window, 0), window)
        k = k_ref[0, pl.ds(start, span), :]
        kpos = start + lax.broadcasted_iota(I32, (span, window), 0)
        qpos = q_start + lax.broadcasted_iota(I32, (span, window), 1)
        valid = (kpos <= qpos) & (qpos - kpos < window)
        for pair in range(2):
            qp = q_ref[0, sub * window:(sub + 1) * window, pair * LANES:(pair + 1) * LANES]
            outs = []
            for half in range(2):
                head = pair + 2 * half
                sink = sink_ref[head] * LOG2E
                s = jnp.where(valid, _dot_nt(k, _half_mask(qp, half)), NEG)
                m = jnp.maximum(jnp.max(s, axis=0, keepdims=True), sink)
                p = jnp.exp2(s - m)
                denom = jnp.sum(p, axis=0, keepdims=True) + jnp.exp2(sink - m)
                vt = vt_ref[0, half * HEAD_DIM:(half + 1) * HEAD_DIM, pl.ds(start, span)]
                outs.append(_dot(vt, p.astype(BF16)) / denom)
            o_t = jnp.concatenate(outs, axis=0)
            o_ref[0, sub * window:(sub + 1) * window, pair * LANES:(pair + 1) * LANES] = (
                o_t.T.astype(o_ref.dtype))


def _swa(q, k, v_t, sinks, *, tq):
    batch, seq, _ = q.shape
    return pl.pallas_call(
        functools.partial(_swa_kernel, tq=tq, window=SWA_WINDOW),
        grid=(batch, seq // tq),
        in_specs=[pl.BlockSpec(memory_space=pltpu.SMEM),
                  pl.BlockSpec((1, tq, GROUP_WIDTH), lambda b, i: (b, i, 0)),
                  pl.BlockSpec((1, seq, LANES), lambda b, i: (b, 0, 0)),
                  pl.BlockSpec((1, SWA_KV_HEADS * HEAD_DIM, seq), lambda b, i: (b, 0, 0))],
        out_specs=pl.BlockSpec((1, tq, GROUP_WIDTH), lambda b, i: (b, i, 0)),
        out_shape=jax.ShapeDtypeStruct((batch, seq, GROUP_WIDTH), BF16),
        compiler_params=pltpu.CompilerParams(
            dimension_semantics=("parallel", "arbitrary"), vmem_limit_bytes=VMEM_LIMIT),
        name="swa",
    )(sinks, q, k, v_t)


def _dsa_kernel(qi_ref, w_ref, ki_ref, qc_ref, kc_ref, vct_ref, o_ref,
                keys_ref, acc_ref, sa_ref, sb_ref, *, tq, ck, topk, seq):
    assert tq == ck
    i = pl.program_id(1)
    q_start = i * tq
    n_chunks = (q_start + tq) // ck
    n_full = q_start // ck
    lane = lax.broadcasted_iota(I32, (tq, LANES), 1)
    kpos_l = lax.broadcasted_iota(I32, (ck, tq), 0)
    qpos = q_start + lax.broadcasted_iota(I32, (ck, tq), 1)
    qpos_row = q_start + lax.broadcasted_iota(I32, (1, tq), 1)
    msb = jnp.int32(INT_MIN)

    qm = []
    for h in range(IDX_HEADS):
        g, r = divmod(h, LANES // IDX_DIM)
        qg = qi_ref[0, :, g * LANES:(g + 1) * LANES]
        qm.append(jnp.where(lane // IDX_DIM == r, qg, jnp.zeros_like(qg)))
    w = w_ref[0]

    def flip(v):
        return v ^ ((v >> 31) & jnp.int32(0x7FFFFFFF))

    def group_reduce(fn, v):
        quarter = v.shape[0] // 4
        parts = [fn(v[j * quarter:(j + 1) * quarter].reshape(quarter // SUBLANES, SUBLANES, tq), axis=0)
                 for j in range(4)]
        return fn(jnp.stack([fn(jnp.stack(parts[:2]), axis=0), fn(jnp.stack(parts[2:]), axis=0)]),
                  axis=0)

    half_heads = IDX_HEADS // 2

    def index_dots(c, first, s_ref):
        ki = ki_ref[0, pl.ds(pl.multiple_of(c * ck, ck), ck), :]
        for j in range(half_heads):
            s_ref[j] = _dot_nt(ki, qm[first + j])

    def weighted_relu(first, s_ref):
        acc = None
        for j in range(half_heads):
            term = jnp.maximum(s_ref[j], 0.0) * w[first + j:first + j + 1, :]
            acc = term if acc is None else acc + term
        return acc

    def score_chunk(c, carry, last):
        kmax8, kmin8 = carry
        off = pl.multiple_of(c * ck, ck)
        index_dots(c, half_heads, sb_ref)
        acc = weighted_relu(0, sa_ref)
        if not last:
            index_dots(c + 1, 0, sa_ref)
        acc = acc + weighted_relu(half_heads, sb_ref)
        key = flip(lax.bitcast_convert_type(acc, I32))
        key_lo = key
        if last:
            causal = off + kpos_l <= qpos
            key_lo = jnp.where(causal, key, jnp.int32(2 ** 31 - 1))
            key = jnp.where(causal, key, msb)
        keys_ref[pl.ds(off, ck), :] = key
        return (jnp.maximum(kmax8, group_reduce(jnp.max, key)),
                jnp.minimum(kmin8, group_reduce(jnp.min, key_lo)))

    ext = (jnp.full((SUBLANES, tq), INT_MIN, I32), jnp.full((SUBLANES, tq), 2 ** 31 - 1, I32))
    index_dots(0, 0, sa_ref)
    ext = lax.fori_loop(0, n_full, lambda c, cr: score_chunk(c, cr, False), ext)
    ext = score_chunk(n_full, ext, True)
    kmax = jnp.max(ext[0], axis=0, keepdims=True)
    kmin = jnp.min(ext[1], axis=0, keepdims=True)

    cb = 2 * ck
    n_blocks = (n_chunks + 1) // 2

    @pl.when(n_chunks % 2 == 1)
    def _():
        keys_ref[pl.ds(pl.multiple_of(n_chunks * ck, ck), ck), :] = jnp.full((ck, tq), INT_MIN, I32)

    def key_blocks(fn, init):
        def body(b, carry):
            off = pl.multiple_of(b * cb, cb)
            return fn(keys_ref[pl.ds(off, cb), :], off, carry)
        return lax.fori_loop(0, n_blocks, body, init)

    def count(pred):
        part = key_blocks(
            lambda kk, off, part: part + group_reduce(jnp.sum, jnp.where(pred(kk, off), 1, 0)),
            jnp.zeros((SUBLANES, tq), I32))
        return jnp.sum(part, axis=0, keepdims=True)

    need = qpos_row >= topk

    def settle(state, cand_k, c, lo_k, hi_k):
        done, tie, t_res = state
        hit = c == topk
        fin = (done == 0) & (hit | (hi_k - lo_k == 1))
        t_res = jnp.where(fin, jnp.where(hit, cand_k, lo_k), t_res)
        tie = jnp.where(fin & ~hit, 1, tie)
        return jnp.where(fin, 1, done), tie, t_res

    def bisect_step(state):
        it, flags, lo_k, hi_k, c_hi = state
        cand_k = lo_k + lax.shift_right_logical(hi_k - lo_k, 1)
        c = count(lambda kk, off: kk >= cand_k)
        ge = c >= topk
        lo_k = jnp.where(ge, cand_k, lo_k)
        hi_k = jnp.where(ge, hi_k, cand_k)
        return it + 1, settle(flags, cand_k, c, lo_k, hi_k), lo_k, hi_k, jnp.where(ge, c_hi, c)

    def snap_step(state):
        it, flags, lo_k, hi_k, c_hi = state

        top8 = key_blocks(
            lambda kk, off, part: jnp.maximum(
                part, group_reduce(jnp.max, jnp.where(kk < hi_k, kk, msb))),
            jnp.full((SUBLANES, tq), INT_MIN, I32))
        cand_k = jnp.maximum(jnp.max(top8, axis=0, keepdims=True), lo_k)
        c = count(lambda kk, off: kk >= cand_k)
        ge = c >= topk
        lo_k = jnp.where(ge, cand_k, lo_k)
        hi_k = jnp.where(ge, cand_k + 1, cand_k)
        return it + 1, settle(flags, cand_k, c, lo_k, hi_k), lo_k, hi_k, jnp.where(ge, c_hi, c)

    def unsettled(state):
        return jnp.min(state[1][0]) == 0

    lo0, hi0 = kmin, kmax + 1
    flat0 = need & (hi0 - lo0 == 1)
    flags = (jnp.where(need & ~flat0, 0, 1), jnp.where(flat0, 1, 0), jnp.where(need, lo0, msb + 1))
    state = (jnp.int32(0), flags, lo0, hi0, jnp.zeros((1, tq), I32))
    state = lax.fori_loop(0, BISECT_STEPS, lambda _, st: bisect_step(st), state)
    state = bisect_step(snap_step(state))
    state = lax.while_loop(unsettled, lambda st: bisect_step(snap_step(st)), state)
    _, (_, tie_flag, t_s), _, _, n_gt = state
    tie = tie_flag > 0

    @pl.when(jnp.max(jnp.where(tie, 1, 0)) > 0)
    def _():
        want = (topk - n_gt).astype(F32)
        tri = (lax.broadcasted_iota(I32, (ck, ck), 0) > lax.broadcasted_iota(I32, (ck, ck), 1))
        tri = jnp.where(tri, 1.0, 0.0).astype(BF16)

        def demote(c, before):
            off = pl.multiple_of(c * ck, ck)
            kk = keys_ref[pl.ds(off, ck), :]
            tied = tie & (kk == t_s)
            ind = jnp.where(tied, 1.0, 0.0)
            rank = before + _dot(tri, ind.astype(BF16))
            keys_ref[pl.ds(off, ck), :] = jnp.where(tied & (rank >= want), msb, kk)
            return before + jnp.sum(ind, axis=0, keepdims=True)

        lax.fori_loop(0, n_chunks, demote, jnp.zeros((1, tq), F32))

    t_fin = jnp.maximum(t_s, msb + 1)
    qs = []
    for h in range(N_HEADS):
        pair, half = divmod(h, 2)
        qs.append(_half_mask(qc_ref[0, :, pair * LANES:(pair + 1) * LANES], half))
    acc_ref[...] = jnp.zeros(acc_ref.shape, F32)

    def scores(c, s_ref):
        off = pl.multiple_of(c * ck, ck)
        for h in range(N_HEADS):
            k = kc_ref[0, pl.ds(off, ck), (h // 2) * LANES:(h // 2 + 1) * LANES]
            s_ref[h] = _dot_nt(k, qs[h])

    def consume(c, s_ref, carry):
        ms, ls = carry
        off = pl.multiple_of(c * ck, ck)
        sel = keys_ref[pl.ds(off, ck), :] >= t_fin
        new_ms, new_ls = [], []
        for h in range(N_HEADS):
            pair = h // 2
            p, m_new, l8, alpha = _softmax_step(s_ref[h], ms[h], ls[h], sel)
            vt = vct_ref[0, h * HEAD_DIM:(h + 1) * HEAD_DIM, pl.ds(off, ck)]
            acc_ref[h] = alpha * acc_ref[h] + _dot(vt, p)
            new_ms.append(m_new)
            new_ls.append(l8)
        return tuple(new_ms), tuple(new_ls)

    def pair_body(p, carry):
        c = 2 * p
        scores(c + 1, sb_ref)
        carry = consume(c, sa_ref, carry)
        scores(c + 2, sa_ref)
        return consume(c + 1, sb_ref, carry)

    carry = (tuple(jnp.full((1, tq), NEG, F32) for _ in range(N_HEADS)),
             tuple(jnp.zeros((SUBLANES, tq), F32) for _ in range(N_HEADS)))
    scores(0, sa_ref)
    carry = lax.fori_loop(0, n_full // 2, pair_body, carry)

    def tail_odd(carry):
        scores(n_full, sb_ref)
        return consume(n_full, sb_ref, consume(n_full - 1, sa_ref, carry))

    _, ls = lax.cond(n_full % 2 == 1, tail_odd, lambda cr: consume(n_full, sa_ref, cr), carry)
    for pair in range(2):
        o0 = acc_ref[2 * pair] / jnp.sum(ls[2 * pair], axis=0, keepdims=True)
        o1 = acc_ref[2 * pair + 1] / jnp.sum(ls[2 * pair + 1], axis=0, keepdims=True)
        o_t = jnp.concatenate([o0, o1], axis=0)
        o_ref[0, :, pair * LANES:(pair + 1) * LANES] = o_t.T.astype(o_ref.dtype)


def _dsa(qi, misc_t, ki, qc, kc, vc_t, *, tq, ck, topk):
    batch, seq, _ = qc.shape
    return pl.pallas_call(
        functools.partial(_dsa_kernel, tq=tq, ck=ck, topk=topk, seq=seq),
        grid=(batch, seq // tq),
        in_specs=[pl.BlockSpec((1, tq, IDX_HEADS * IDX_DIM), lambda b, i: (b, i, 0)),
                  pl.BlockSpec((1, MISC_ROWS, tq), lambda b, i: (b, 0, i)),
                  pl.BlockSpec((1, seq, LANES), lambda b, i: (b, 0, 0)),
                  pl.BlockSpec((1, tq, GROUP_WIDTH), lambda b, i: (b, i, 0)),
                  pl.BlockSpec((1, seq, GROUP_WIDTH), lambda b, i: (b, 0, 0)),
                  pl.BlockSpec((1, GROUP_WIDTH, seq), lambda b, i: (b, 0, 0))],
        out_specs=pl.BlockSpec((1, tq, GROUP_WIDTH), lambda b, i: (b, i, 0)),
        out_shape=jax.ShapeDtypeStruct((batch, seq, GROUP_WIDTH), BF16),
        scratch_shapes=[pltpu.VMEM((seq, tq), I32),
                        pltpu.VMEM((N_HEADS, HEAD_DIM, tq), F32),
                        pltpu.VMEM((N_HEADS, ck, tq), F32),
                        pltpu.VMEM((N_HEADS, ck, tq), F32)],
        compiler_params=pltpu.CompilerParams(
            dimension_semantics=("parallel", "arbitrary"), vmem_limit_bytes=VMEM_LIMIT),
        name="dsa",
    )(qi, misc_t, ki, qc, kc, vc_t)


def _mlp_kernel(oa_ref, ob_ref, oc_ref, od_ref, x_ref, wo_ref, g2_ref, wup_ref, wdn_ref,
                gf_ref, out_ref, *, ff_chunk, final):
    gw = GROUP_WIDTH
    mix = None
    for g, o_ref in enumerate((oa_ref, ob_ref, oc_ref, od_ref)):
        y = _dot(o_ref[...], wo_ref[g * gw:(g + 1) * gw, :])
        mix = y if mix is None else mix + y
    x1 = x_ref[...] + mix
    h2 = _rms(x1, g2_ref[...]).astype(BF16)
    ffn = None
    for c in range(wup_ref.shape[1] // ff_chunk):
        u = _dot(h2, wup_ref[:, c * ff_chunk:(c + 1) * ff_chunk])
        a = jnp.square(jnp.maximum(u, 0.0)).astype(BF16)
        y = _dot(a, wdn_ref[c * ff_chunk:(c + 1) * ff_chunk, :])
        ffn = y if ffn is None else ffn + y
    acc = x1 + ffn
    if final:
        acc = _rms(acc, gf_ref[...])
    out_ref[...] = acc


def _mlp(oa, ob, oc, od, x2d, wo, g2, wup, wdn, gf, *, tm, final):
    n, d = x2d.shape
    dff = wup.shape[1]
    row = lambda i: (i, 0)
    const = lambda i: (0, 0)
    once = pl.Buffered(1)
    o_spec = pl.BlockSpec((tm, GROUP_WIDTH), row)
    return pl.pallas_call(
        functools.partial(_mlp_kernel, ff_chunk=1024, final=final),
        grid=(n // tm,),
        in_specs=[o_spec, o_spec, o_spec, o_spec,
                  pl.BlockSpec((tm, d), row),
                  pl.BlockSpec((d, d), const, pipeline_mode=once),
                  pl.BlockSpec((1, d), const),
                  pl.BlockSpec((d, dff), const, pipeline_mode=once),
                  pl.BlockSpec((dff, d), const, pipeline_mode=once),
                  pl.BlockSpec((1, d), const)],
        out_specs=pl.BlockSpec((tm, d), row),
        out_shape=jax.ShapeDtypeStruct((n, d), F32),
        compiler_params=pltpu.CompilerParams(
            dimension_semantics=("parallel",), vmem_limit_bytes=VMEM_LIMIT),
        name="outproj_mlp",
    )(oa, ob, oc, od, x2d, wo, g2, wup, wdn, gf)


def _rope_tables(seq):
    pos = jnp.arange(seq, dtype=F32)[:, None]
    lane = jnp.arange(LANES)

    def table(dim, active):
        half = dim // 2
        inv_freq = 1.0 / (ROPE_THETA ** (jnp.arange(0, half, dtype=F32) * 2.0 / dim))
        ang = pos * inv_freq[None, :][:, lane % half]
        sign = jnp.where((lane % dim) < half, -1.0, 1.0)
        cos = jnp.where(active, jnp.cos(ang), 1.0)
        sin = jnp.where(active, jnp.sin(ang) * sign, 0.0)
        return cos.astype(F32), sin.astype(F32)

    c64, s64 = table(HEAD_DIM, jnp.ones((LANES,), bool))
    c32, s32 = table(IDX_DIM, jnp.ones((LANES,), bool))
    cm, sm = table(MLA_ROPE, (lane >= MLA_NOPE) & (lane < MLA_NOPE + MLA_ROPE))
    return c64, s64, c32, s32, cm, sm


def _pack_weights(w_in, mla_w_uq, mla_w_ukv, w_out):
    depth, d, _ = w_in.shape
    splits = (MLA_Q_RANK, MLA_KV_RANK, MLA_ROPE,
              GROUP_WIDTH, SWA_KV_HEADS * HEAD_DIM, SWA_KV_HEADS * HEAD_DIM,
              GROUP_WIDTH, GROUP_WIDTH, GROUP_WIDTH, IDX_HEADS * IDX_DIM, IDX_DIM, IDX_HEADS,
              GROUP_WIDTH, GROUP_WIDTH, GROUP_WIDTH, N_HEADS)
    offs = [0]
    for s in splits:
        offs.append(offs[-1] + s)
    (a_cq, a_ckv, a_kr, b_q, b_k, b_v, c_q, c_k, c_v, c_qi, c_ki, c_w,
     d_q, d_k, d_v, d_f) = [w_in[:, :, offs[j]:offs[j + 1]] for j in range(len(splits))]
    zeros = lambda n: jnp.zeros((depth, d, n), w_in.dtype)
    swap = jnp.array([0, 2, 1, 3])
    a_kr = jnp.concatenate([zeros(MLA_NOPE), a_kr, zeros(LANES - MLA_NOPE - MLA_ROPE)], -1)
    b_q = b_q.reshape(depth, d, N_HEADS, HEAD_DIM)[:, :, swap].reshape(depth, d, GROUP_WIDTH)
    c_ki = jnp.concatenate([c_ki] * (LANES // IDX_DIM), -1)
    w_row = jnp.concatenate(
        [a_cq, a_ckv, a_kr, b_q, b_k, c_ki, c_q, c_k, c_qi, d_q, d_k], -1).astype(BF16)
    w_t = jnp.concatenate([c_v, d_v, b_v, c_w, d_f, zeros(MISC_ROWS - IDX_HEADS - N_HEADS)], -1)
    w_t = jnp.swapaxes(w_t, 1, 2).astype(BF16)

    pad = LANES - MLA_NOPE - MLA_ROPE
    wuq = mla_w_uq.reshape(depth, MLA_Q_RANK, N_HEADS, MLA_NOPE + MLA_ROPE)
    wuq = jnp.pad(wuq, ((0, 0), (0, 0), (0, 0), (0, pad))).reshape(depth, MLA_Q_RANK, N_HEADS * LANES)
    wukv = mla_w_ukv.reshape(depth, MLA_KV_RANK, N_HEADS, MLA_NOPE + HEAD_DIM)
    wuk = jnp.pad(wukv[..., :MLA_NOPE], ((0, 0), (0, 0), (0, 0), (0, LANES - MLA_NOPE)))
    wuk = wuk.reshape(depth, MLA_KV_RANK, N_HEADS * LANES)
    wuvt = jnp.swapaxes(wukv[..., MLA_NOPE:].reshape(depth, MLA_KV_RANK, GROUP_WIDTH), 1, 2)

    wo_b = w_out[:, GROUP_WIDTH:2 * GROUP_WIDTH].reshape(depth, N_HEADS, HEAD_DIM, -1)[:, swap]
    wo = jnp.concatenate([w_out[:, :GROUP_WIDTH], wo_b.reshape(depth, GROUP_WIDTH, -1),
                          w_out[:, 2 * GROUP_WIDTH:]], 1)
    return w_row, w_t, wuq.astype(BF16), wuk.astype(BF16), wuvt.astype(BF16), wo.astype(BF16)


_MLA_HEADS = tuple((h * LANES, None, h * LANES) for h in range(N_HEADS))
_FOX_HEADS = tuple(((h // 2) * LANES, h % 2, (h // 2) * LANES) for h in range(N_HEADS))


def kernel(x, norm1, w_in, mla_q_norm, mla_kv_norm, mla_w_uq, mla_w_ukv, swa_sinks, fox_b_f,
           w_out, norm2, w_up, w_down, final_norm):
    batch, seq, d = x.shape
    depth = w_in.shape[0]
    n = batch * seq
    top_k = min(DSA_TOPK, seq // 4)
    tm = min(512, seq)
    tq_flash = min(256, seq)
    ck_flash = min(256, seq)
    tq_swa = min(512, seq)
    tq_dsa = min(256, seq)
    ck_dsa = min(256, seq)

    tables = _rope_tables(seq)
    w_row, w_t, wuq, wuk, wuvt, wo = _pack_weights(w_in, mla_w_uq, mla_w_ukv, w_out)
    wup = w_up.astype(BF16)
    wdn = w_down.astype(BF16)

    x2d = x.reshape(n, d)
    for l in range(depth):
        (qa, ka, qb, kb, qc, kc, qi, ki, qd, kd, va_t, vc_t, vd_t, vb_t, misc_t) = _inproj(
            x2d, norm1[l][None], w_row[l], w_t[l], mla_q_norm[l][None], mla_kv_norm[l][None],
            wuq[l], wuk[l], wuvt[l], tables, batch=batch, seq=seq, tm=tm)
        r3 = lambda a: a.reshape(batch, seq, a.shape[-1])
        cum = _forget_cumsum(misc_t[:, IDX_HEADS:IDX_HEADS + N_HEADS], fox_b_f[l],
                             batch=batch, seq=seq)
        cum_col = jnp.swapaxes(cum, 1, 2)
        o_a = _flash(r3(qa), r3(ka), va_t, None, _MLA_HEADS, tq=tq_flash, ck=ck_flash)
        o_b = _swa(r3(qb), r3(kb), vb_t, swa_sinks[l], tq=tq_swa)
        o_c = _dsa(r3(qi), misc_t, r3(ki), r3(qc), r3(kc), vc_t, tq=tq_dsa, ck=ck_dsa, topk=top_k)
        o_d = _flash(r3(qd), r3(kd), vd_t, cum_col, _FOX_HEADS, tq=tq_flash, ck=ck_flash)
        x2d = _mlp(o_a.reshape(n, -1), o_b.reshape(n, -1), o_c.reshape(n, -1), o_d.reshape(n, -1),
                   x2d, wo[l], norm2[l][None], wup[l], wdn[l], final_norm[None],
                   tm=tm, final=(l == depth - 1))
    return x2d.reshape(batch, seq, d)
```

```python
import functools
import math

import jax
import jax.numpy as jnp
from jax import lax
from jax.experimental import pallas as pl
from jax.experimental.pallas import tpu as pltpu

F32 = jnp.float32
BF16 = jnp.bfloat16
I32 = jnp.int32

HEAD_DIM = 64
N_HEADS = 4
GROUP_WIDTH = N_HEADS * HEAD_DIM
MLA_Q_RANK = 256
MLA_KV_RANK = 128
MLA_NOPE = 64
MLA_ROPE = 32
SWA_KV_HEADS = 2
SWA_WINDOW = 128
IDX_HEADS = 8
IDX_DIM = 32
DSA_TOPK = 256
ROPE_THETA = 10000.0
EPS = 1e-6

LANES = 128
SUBLANES = 8
VMEM_LIMIT = 56 * 1024 * 1024
NEG = -1e30
INT_MIN = -(2 ** 31)
LOG2E = math.log2(math.e)
BISECT_STEPS = 20

_ROW_GROUPS = (
    ("a_cq", 256), ("a_ckv", 128), ("a_kr", 128),
    ("b_q", 256), ("b_k", 128), ("c_ki", 128),
    ("c_q", 256), ("c_k", 256), ("c_qi", 256),
    ("d_q", 256), ("d_k", 256),
)
_ROW_OFF = {}
_acc = 0
for _n, _w in _ROW_GROUPS:
    _ROW_OFF[_n] = (_acc, _acc + _w)
    _acc += _w
ROW_WIDTH = _acc
MISC_ROWS = 16
SWA_KV_WIDTH = SWA_KV_HEADS * HEAD_DIM
T_ROWS = 2 * GROUP_WIDTH + SWA_KV_WIDTH + MISC_ROWS


def _dot(a, b):
    return jnp.dot(a, b, preferred_element_type=F32)


def _dot_nt(a, b):
    return lax.dot_general(a, b, (((1,), (1,)), ((), ())), preferred_element_type=F32)


def _rms(x, g):
    return x * lax.rsqrt(jnp.mean(x * x, axis=-1, keepdims=True) + EPS) * g


def _rope(x, cos, sin, half):
    width = x.shape[1]
    reps = width // LANES
    if reps > 1:
        cos = jnp.concatenate([cos] * reps, axis=1)
        sin = jnp.concatenate([sin] * reps, axis=1)
    lane = lax.broadcasted_iota(I32, x.shape, 1)
    first = (lane % (2 * half)) < half
    rot = jnp.where(first, pltpu.roll(x, width - half, 1), pltpu.roll(x, half, 1))
    return x * cos + rot * sin


def _half_mask(q, half):
    lane = lax.broadcasted_iota(I32, q.shape, 1)
    return jnp.where((lane >= HEAD_DIM) == bool(half), q, jnp.zeros_like(q))


def _softmax_step(s, m, l8, mask):
    ck, tq = s.shape
    if mask is not None:
        s = jnp.where(mask, s, NEG)
    m_new = jnp.maximum(m, jnp.max(s, axis=0, keepdims=True))
    alpha = jnp.exp2(m - m_new)
    p = jnp.exp2(s - m_new)
    l8 = alpha * l8 + jnp.sum(p.reshape(ck // SUBLANES, SUBLANES, tq), axis=0)
    return p.astype(BF16), m_new, l8, alpha


def _inproj_kernel(x_ref, g1_ref, w_ref, wt_ref, qn_ref, kvn_ref, wuq_ref, wuk_ref, wuvt_ref,
                   c64_ref, s64_ref, c32_ref, s32_ref, cm_ref, sm_ref,
                   qa_ref, ka_ref, qb_ref, kb_ref,
                   qc_ref, kc_ref, qi_ref, ki_ref, qd_ref, kd_ref,
                   vat_ref, vct_ref, vdt_ref, vbt_ref, misct_ref):
    h = _rms(x_ref[...], g1_ref[...]).astype(BF16)

    def proj(name):
        lo, hi = _ROW_OFF[name]
        return _dot(h, w_ref[:, lo:hi])

    def proj_pair(name0, name1):
        lo, mid = _ROW_OFF[name0]
        assert _ROW_OFF[name1][0] == mid
        both = _dot(h, w_ref[:, lo:_ROW_OFF[name1][1]])
        return both[:, :mid - lo], both[:, mid - lo:]

    c64, s64 = c64_ref[...], s64_ref[...]
    c32, s32 = c32_ref[...], s32_ref[...]
    cm, sm = cm_ref[...], sm_ref[...]
    scale_a = (MLA_NOPE + MLA_ROPE) ** -0.5 * LOG2E
    scale = HEAD_DIM ** -0.5 * LOG2E

    cq = _rms(proj("a_cq"), qn_ref[...]).astype(BF16)
    qa = _rope(_dot(cq, wuq_ref[...]), cm, sm, MLA_ROPE // 2) * scale_a
    qa_ref[...] = qa.astype(BF16)
    ckv, kr = proj_pair("a_ckv", "a_kr")
    ckv = _rms(ckv, kvn_ref[...]).astype(BF16)
    kr = _rope(kr, cm, sm, MLA_ROPE // 2)
    ka_ref[...] = (_dot(ckv, wuk_ref[...]) + jnp.concatenate([kr] * N_HEADS, axis=1)).astype(BF16)
    vat_ref[0] = _dot_nt(wuvt_ref[...], ckv).astype(BF16)

    qb_ref[...] = (_rope(proj("b_q"), c64, s64, HEAD_DIM // 2) * scale).astype(BF16)
    kb, ki = proj_pair("b_k", "c_ki")
    kb_ref[...] = _rope(kb, c64, s64, HEAD_DIM // 2).astype(BF16)

    qc_ref[...] = (_rope(proj("c_q"), c64, s64, HEAD_DIM // 2) * scale).astype(BF16)
    kc_ref[...] = _rope(proj("c_k"), c64, s64, HEAD_DIM // 2).astype(BF16)
    qi_ref[...] = _rope(proj("c_qi"), c32, s32, IDX_DIM // 2).astype(BF16)
    ki_ref[...] = _rope(ki, c32, s32, IDX_DIM // 2).astype(BF16)

    qd_ref[...] = (proj("d_q") * scale).astype(BF16)
    kd_ref[...] = proj("d_k").astype(BF16)

    t = _dot_nt(wt_ref[...], h)
    vct_ref[0] = t[:GROUP_WIDTH].astype(BF16)
    vdt_ref[0] = t[GROUP_WIDTH:2 * GROUP_WIDTH].astype(BF16)
    vbt_ref[0] = t[2 * GROUP_WIDTH:2 * GROUP_WIDTH + SWA_KV_WIDTH].astype(BF16)
    misct_ref[0] = t[2 * GROUP_WIDTH + SWA_KV_WIDTH:]


def _inproj(x2d, g1, w_row, w_t, qn, kvn, wuq, wuk, wuvt, tables, *, batch, seq, tm):
    n = x2d.shape[0]
    d = x2d.shape[1]
    spb = seq // tm
    row = lambda i: (i, 0)
    const = lambda i: (0, 0)
    tab = lambda i: (i % spb, 0)
    tr = lambda i: (i // spb, 0, i % spb)
    once = pl.Buffered(1)

    def out2(width):
        return jax.ShapeDtypeStruct((n, width), BF16), pl.BlockSpec((tm, width), row)

    def out_t(rows, dtype):
        return jax.ShapeDtypeStruct((batch, rows, seq), dtype), pl.BlockSpec((1, rows, tm), tr)

    outs = [out2(512), out2(512),
            out2(256), out2(128),
            out2(256), out2(256), out2(256), out2(128),
            out2(256), out2(256),
            out_t(GROUP_WIDTH, BF16), out_t(GROUP_WIDTH, BF16), out_t(GROUP_WIDTH, BF16),
            out_t(SWA_KV_WIDTH, BF16), out_t(MISC_ROWS, F32)]
    in_specs = [
        pl.BlockSpec((tm, d), row),
        pl.BlockSpec((1, d), const),
        pl.BlockSpec((d, ROW_WIDTH), const, pipeline_mode=once),
        pl.BlockSpec((T_ROWS, d), const, pipeline_mode=once),
        pl.BlockSpec((1, MLA_Q_RANK), const),
        pl.BlockSpec((1, MLA_KV_RANK), const),
        pl.BlockSpec((MLA_Q_RANK, N_HEADS * LANES), const, pipeline_mode=once),
        pl.BlockSpec((MLA_KV_RANK, N_HEADS * LANES), const, pipeline_mode=once),
        pl.BlockSpec((GROUP_WIDTH, MLA_KV_RANK), const, pipeline_mode=once),
    ] + [pl.BlockSpec((tm, LANES), tab)] * 6
    return pl.pallas_call(
        _inproj_kernel,
        grid=(n // tm,),
        in_specs=in_specs,
        out_specs=[o[1] for o in outs],
        out_shape=[o[0] for o in outs],
        compiler_params=pltpu.CompilerParams(
            dimension_semantics=("parallel",), vmem_limit_bytes=VMEM_LIMIT),
        name="inproj",
    )(x2d, g1, w_row, w_t, qn, kvn, wuq, wuk, wuvt, *tables)


def _cumsum_kernel(f_ref, b_ref, o_ref, *, segs):
    x = f_ref[...] + b_ref[...]
    ls = jnp.minimum(x, 0.0) - jnp.log1p(jnp.exp(-jnp.abs(x)))
    rows = x.shape[0]
    r = lax.broadcasted_iota(I32, (LANES, LANES), 0)
    c = lax.broadcasted_iota(I32, (LANES, LANES), 1)
    upper = (r <= c).astype(F32)
    within = jnp.dot(ls, upper, preferred_element_type=F32, precision=lax.Precision.HIGHEST)
    tot = jnp.broadcast_to(within[:, LANES - 1:LANES], (rows, LANES))
    rr = lax.broadcasted_iota(I32, (rows, rows), 0)
    cc = lax.broadcasted_iota(I32, (rows, rows), 1)
    before = ((cc < rr) & (cc // segs == rr // segs)).astype(F32)
    off = jnp.dot(before, tot, preferred_element_type=F32, precision=lax.Precision.HIGHEST)
    o_ref[...] = (within + off) * LOG2E


def _forget_cumsum(f_t, bias, *, batch, seq):
    segs = seq // LANES
    rows = N_HEADS * segs
    f2 = f_t.reshape(batch * rows, LANES)
    b2 = jnp.broadcast_to(jnp.repeat(bias, segs)[:, None], (rows, LANES))
    out = pl.pallas_call(
        functools.partial(_cumsum_kernel, segs=segs),
        grid=(batch,),
        in_specs=[pl.BlockSpec((rows, LANES), lambda b: (b, 0)),
                  pl.BlockSpec((rows, LANES), lambda b: (0, 0))],
        out_specs=pl.BlockSpec((rows, LANES), lambda b: (b, 0)),
        out_shape=jax.ShapeDtypeStruct((batch * rows, LANES), F32),
        compiler_params=pltpu.CompilerParams(dimension_semantics=("parallel",)),
        name="forget_cumsum",
    )(f2, b2)
    return out.reshape(batch, N_HEADS, seq)


def _flash_kernel(*refs, heads, tq, ck, has_bias):
    if has_bias:
        q_ref, k_ref, vt_ref, b_ref, o_ref, acc_ref, sa_ref, sb_ref = refs
    else:
        q_ref, k_ref, vt_ref, o_ref, acc_ref, sa_ref, sb_ref = refs
        b_ref = None
    assert tq == ck
    nh = len(heads)
    i = pl.program_id(1)
    q_start = i * tq
    n_full = q_start // ck
    kpos_l = lax.broadcasted_iota(I32, (ck, tq), 0)
    qpos = q_start + lax.broadcasted_iota(I32, (ck, tq), 1)
    qs = []
    for ql, qhalf, _ in heads:
        q = q_ref[0, :, ql:ql + LANES]
        qs.append(q if qhalf is None else _half_mask(q, qhalf))
    acc_ref[...] = jnp.zeros(acc_ref.shape, F32)

    def scores(c, s_ref):
        off = pl.multiple_of(c * ck, ck)
        for h, (_, _, kl) in enumerate(heads):
            s = _dot_nt(k_ref[0, pl.ds(off, ck), kl:kl + LANES], qs[h])
            if has_bias:
                s = s - b_ref[0, pl.ds(off, ck), h:h + 1]
            s_ref[h] = s

    def consume(c, s_ref, carry, masked):
        ms, ls = carry
        off = pl.multiple_of(c * ck, ck)
        mask = (off + kpos_l <= qpos) if masked else None
        new_ms, new_ls = [], []
        for h in range(nh):
            p, m_new, l8, alpha = _softmax_step(s_ref[h], ms[h], ls[h], mask)
            vt = vt_ref[0, h * HEAD_DIM:(h + 1) * HEAD_DIM, pl.ds(off, ck)]
            acc_ref[h] = alpha * acc_ref[h] + _dot(vt, p)
            new_ms.append(m_new)
            new_ls.append(l8)
        return tuple(new_ms), tuple(new_ls)

    def pair_body(p, carry):
        c = 2 * p
        scores(c + 1, sb_ref)
        carry = consume(c, sa_ref, carry, False)
        scores(c + 2, sa_ref)
        return consume(c + 1, sb_ref, carry, False)

    carry = (tuple(jnp.full((1, tq), NEG, F32) for _ in range(nh)),
             tuple(jnp.zeros((SUBLANES, tq), F32) for _ in range(nh)))
    scores(0, sa_ref)
    carry = lax.fori_loop(0, n_full // 2, pair_body, carry)
    odd = n_full % 2 == 1

    def tail_odd(carry):
        scores(n_full, sb_ref)
        carry = consume(n_full - 1, sa_ref, carry, False)
        return consume(n_full, sb_ref, carry, True)

    carry = lax.cond(odd, tail_odd, lambda cr: consume(n_full, sa_ref, cr, True), carry)
    _, ls = carry
    for pair in range(nh // 2):
        o0 = acc_ref[2 * pair] / jnp.sum(ls[2 * pair], axis=0, keepdims=True)
        o1 = acc_ref[2 * pair + 1] / jnp.sum(ls[2 * pair + 1], axis=0, keepdims=True)
        o_t = jnp.concatenate([o0, o1], axis=0)
        o_ref[0, :, pair * LANES:(pair + 1) * LANES] = o_t.T.astype(o_ref.dtype)


def _flash(q, k, v_t, bias, heads, *, tq, ck):
    batch, seq, wq = q.shape
    wk = k.shape[2]
    has_bias = bias is not None
    in_specs = [pl.BlockSpec((1, tq, wq), lambda b, i: (b, i, 0)),
                pl.BlockSpec((1, seq, wk), lambda b, i: (b, 0, 0)),
                pl.BlockSpec((1, GROUP_WIDTH, seq), lambda b, i: (b, 0, 0))]
    args = [q, k, v_t]
    if has_bias:
        in_specs.append(pl.BlockSpec((1, seq, N_HEADS), lambda b, i: (b, 0, 0)))
        args.append(bias)
    return pl.pallas_call(
        functools.partial(_flash_kernel, heads=heads, tq=tq, ck=ck, has_bias=has_bias),
        grid=(batch, seq // tq),
        in_specs=in_specs,
        out_specs=pl.BlockSpec((1, tq, GROUP_WIDTH), lambda b, i: (b, i, 0)),
        out_shape=jax.ShapeDtypeStruct((batch, seq, GROUP_WIDTH), BF16),
        scratch_shapes=[pltpu.VMEM((len(heads), HEAD_DIM, tq), F32),
                        pltpu.VMEM((len(heads), ck, tq), F32),
                        pltpu.VMEM((len(heads), ck, tq), F32)],
        compiler_params=pltpu.CompilerParams(
            dimension_semantics=("parallel", "arbitrary"), vmem_limit_bytes=VMEM_LIMIT),
        name="flash_bias" if has_bias else "flash",
    )(*args)


def _swa_kernel(sink_ref, q_ref, k_ref, vt_ref, o_ref, sa_ref, sb_ref, *, tq, window):
    i = pl.program_id(1)
    span = 2 * window
    n_sub = tq // window
    assert n_sub % 2 == 0

    def key_start(j):
        return pl.multiple_of(jnp.maximum(i * tq + j * window - window, 0), window)

    def scores(j, s_ref):
        j = jnp.minimum(j, n_sub - 1)
        k = k_ref[0, pl.ds(key_start(j), span), :]
        rows = pl.ds(pl.multiple_of(j * window, window), window)
        for pair in range(2):
            qp = q_ref[0, rows, pair * LANES:(pair + 1) * LANES]
            for half in range(2):
                s_ref[2 * pair + half] = _dot_nt(k, _half_mask(qp, half))

    def consume(j, s_ref):
        start = key_start(j)
        kpos = start + lax.broadcasted_iota(I32, (span, window), 0)
        qpos = i * tq + j * window + lax.broadcasted_iota(I32, (span, window), 1)
        valid = (kpos <= qpos) & (qpos - kpos < window)
        rows = pl.ds(pl.multiple_of(j * window, window), window)
        for pair in range(2):
            outs = []
            for half in range(2):
                head = pair + 2 * half
                sink = sink_ref[head] * LOG2E
                s = jnp.where(valid, s_ref[2 * pair + half], NEG)
                m = jnp.maximum(jnp.max(s, axis=0, keepdims=True), sink)
                p = jnp.exp2(s - m)
                denom = jnp.sum(p, axis=0, keepdims=True) + jnp.exp2(sink - m)
                vt = vt_ref[0, half * HEAD_DIM:(half + 1) * HEAD_DIM, pl.ds(start, span)]
                outs.append(_dot(vt, p.astype(BF16)) / denom)
            o_t = jnp.concatenate(outs, axis=0)
            o_ref[0, rows, pair * LANES:(pair + 1) * LANES] = o_t.T.astype(o_ref.dtype)

    def pair_body(p, carry):
        j = 2 * p
        scores(j + 1, sb_ref)
        consume(j, sa_ref)
        scores(j + 2, sa_ref)
        consume(j + 1, sb_ref)
        return carry

    scores(0, sa_ref)
    lax.fori_loop(0, n_sub // 2, pair_body, 0)


def _swa(q, k, v_t, sinks, *, tq):
    batch, seq, _ = q.shape
    return pl.pallas_call(
        functools.partial(_swa_kernel, tq=tq, window=SWA_WINDOW),
        grid=(batch, seq // tq),
        in_specs=[pl.BlockSpec(memory_space=pltpu.SMEM),
                  pl.BlockSpec((1, tq, GROUP_WIDTH), lambda b, i: (b, i, 0)),
                  pl.BlockSpec((1, seq, LANES), lambda b, i: (b, 0, 0)),
                  pl.BlockSpec((1, SWA_KV_HEADS * HEAD_DIM, seq), lambda b, i: (b, 0, 0))],
        out_specs=pl.BlockSpec((1, tq, GROUP_WIDTH), lambda b, i: (b, i, 0)),
        out_shape=jax.ShapeDtypeStruct((batch, seq, GROUP_WIDTH), BF16),
        scratch_shapes=[pltpu.VMEM((N_HEADS, 2 * SWA_WINDOW, SWA_WINDOW), F32),
                        pltpu.VMEM((N_HEADS, 2 * SWA_WINDOW, SWA_WINDOW), F32)],
        compiler_params=pltpu.CompilerParams(
            dimension_semantics=("parallel", "arbitrary"), vmem_limit_bytes=VMEM_LIMIT),
        name="swa",
    )(sinks, q, k, v_t)


def _dsa_kernel(qi_ref, w_ref, ki_ref, qc_ref, kc_ref, vct_ref, o_ref,
                keys_ref, acc_ref, sa_ref, sb_ref, *, tq, ck, topk, seq):
    assert tq == ck
    i = pl.program_id(1)
    q_start = i * tq
    n_chunks = (q_start + tq) // ck
    n_full = q_start // ck
    lane = lax.broadcasted_iota(I32, (tq, LANES), 1)
    kpos_l = lax.broadcasted_iota(I32, (ck, tq), 0)
    qpos = q_start + lax.broadcasted_iota(I32, (ck, tq), 1)
    qpos_row = q_start + lax.broadcasted_iota(I32, (1, tq), 1)
    msb = jnp.int32(INT_MIN)

    qm = []
    for h in range(IDX_HEADS):
        g, r = divmod(h, LANES // IDX_DIM)
        qg = qi_ref[0, :, g * LANES:(g + 1) * LANES]
        qm.append(jnp.where(lane // IDX_DIM == r, qg, jnp.zeros_like(qg)))
    w = w_ref[0]

    def flip(v):
        return v ^ ((v >> 31) & jnp.int32(0x7FFFFFFF))

    def group_reduce(fn, v):
        quarter = v.shape[0] // 4
        parts = [fn(v[j * quarter:(j + 1) * quarter].reshape(quarter // SUBLANES, SUBLANES, tq), axis=0)
                 for j in range(4)]
        return fn(jnp.stack([fn(jnp.stack(parts[:2]), axis=0), fn(jnp.stack(parts[2:]), axis=0)]),
                  axis=0)

    half_heads = IDX_HEADS // 2

    def index_dots(c, first, s_ref):
        ki = ki_ref[0, pl.ds(pl.multiple_of(c * ck, ck), ck), :]
        for j in range(half_heads):
            s_ref[j] = _dot_nt(ki, qm[first + j])

    def weighted_relu(first, s_ref):
        acc = None
        for j in range(half_heads):
            term = jnp.maximum(s_ref[j], 0.0) * w[first + j:first + j + 1, :]
            acc = term if acc is None else acc + term
        return acc

    def score_chunk(c, carry, last):
        kmax8, kmin8 = carry
        off = pl.multiple_of(c * ck, ck)
        index_dots(c, half_heads, sb_ref)
        acc = weighted_relu(0, sa_ref)
        if not last:
            index_dots(c + 1, 0, sa_ref)
        acc = acc + weighted_relu(half_heads, sb_ref)
        key = flip(lax.bitcast_convert_type(acc, I32))
        key_lo = key
        if last:
            causal = off + kpos_l <= qpos
            key_lo = jnp.where(causal, key, jnp.int32(2 ** 31 - 1))
            key = jnp.where(causal, key, msb)
        keys_ref[pl.ds(off, ck), :] = key
        return (jnp.maximum(kmax8, group_reduce(jnp.max, key)),
                jnp.minimum(kmin8, group_reduce(jnp.min, key_lo)))

    ext = (jnp.full((SUBLANES, tq), INT_MIN, I32), jnp.full((SUBLANES, tq), 2 ** 31 - 1, I32))
    index_dots(0, 0, sa_ref)
    ext = lax.fori_loop(0, n_full, lambda c, cr: score_chunk(c, cr, False), ext)
    ext = score_chunk(n_full, ext, True)
    kmax = jnp.max(ext[0], axis=0, keepdims=True)
    kmin = jnp.min(ext[1], axis=0, keepdims=True)

    cb = 2 * ck
    n_blocks = (n_chunks + 1) // 2

    @pl.when(n_chunks % 2 == 1)
    def _():
        keys_ref[pl.ds(pl.multiple_of(n_chunks * ck, ck), ck), :] = jnp.full((ck, tq), INT_MIN, I32)

    def key_blocks(fn, init):
        def body(b, carry):
            off = pl.multiple_of(b * cb, cb)
            return fn(keys_ref[pl.ds(off, cb), :], off, carry)
        return lax.fori_loop(0, n_blocks, body, init)

    def count(pred):
        part = key_blocks(
            lambda kk, off, part: part + group_reduce(jnp.sum, jnp.where(pred(kk, off), 1, 0)),
            jnp.zeros((SUBLANES, tq), I32))
        return jnp.sum(part, axis=0, keepdims=True)

    need = qpos_row >= topk

    def settle(state, cand_k, c, lo_k, hi_k):
        done, tie, t_res = state
        hit = c == topk
        fin = (done == 0) & (hit | (hi_k - lo_k == 1))
        t_res = jnp.where(fin, jnp.where(hit, cand_k, lo_k), t_res)
        tie = jnp.where(fin & ~hit, 1, tie)
        return jnp.where(fin, 1, done), tie, t_res

    def bisect_step(state):
        it, flags, lo_k, hi_k, c_hi = state
        cand_k = lo_k + lax.shift_right_logical(hi_k - lo_k, 1)
        c = count(lambda kk, off: kk >= cand_k)
        ge = c >= topk
        lo_k = jnp.where(ge, cand_k, lo_k)
        hi_k = jnp.where(ge, hi_k, cand_k)
        return it + 1, settle(flags, cand_k, c, lo_k, hi_k), lo_k, hi_k, jnp.where(ge, c_hi, c)

    def snap_step(state):
        it, flags, lo_k, hi_k, c_hi = state

        top8 = key_blocks(
            lambda kk, off, part: jnp.maximum(
                part, group_reduce(jnp.max, jnp.where(kk < hi_k, kk, msb))),
            jnp.full((SUBLANES, tq), INT_MIN, I32))
        cand_k = jnp.maximum(jnp.max(top8, axis=0, keepdims=True), lo_k)
        c = count(lambda kk, off: kk >= cand_k)
        ge = c >= topk
        lo_k = jnp.where(ge, cand_k, lo_k)
        hi_k = jnp.where(ge, cand_k + 1, cand_k)
        return it + 1, settle(flags, cand_k, c, lo_k, hi_k), lo_k, hi_k, jnp.where(ge, c_hi, c)

    def unsettled(state):
        return jnp.min(state[1][0]) == 0

    lo0, hi0 = kmin, kmax + 1
    flat0 = need & (hi0 - lo0 == 1)
    flags = (jnp.where(need & ~flat0, 0, 1), jnp.where(flat0, 1, 0), jnp.where(need, lo0, msb + 1))
    state = (jnp.int32(0), flags, lo0, hi0, jnp.zeros((1, tq), I32))
    state = lax.fori_loop(0, BISECT_STEPS, lambda _, st: bisect_step(st), state)
    state = bisect_step(snap_step(state))
    state = lax.while_loop(unsettled, lambda st: bisect_step(snap_step(st)), state)
    _, (_, tie_flag, t_s), _, _, n_gt = state
    tie = tie_flag > 0

    @pl.when(jnp.max(jnp.where(tie, 1, 0)) > 0)
    def _():
        want = (topk - n_gt).astype(F32)
        tri = (lax.broadcasted_iota(I32, (ck, ck), 0) > lax.broadcasted_iota(I32, (ck, ck), 1))
        tri = jnp.where(tri, 1.0, 0.0).astype(BF16)

        def demote(c, before):
            off = pl.multiple_of(c * ck, ck)
            kk = keys_ref[pl.ds(off, ck), :]
            tied = tie & (kk == t_s)
            ind = jnp.where(tied, 1.0, 0.0)
            rank = before + _dot(tri, ind.astype(BF16))
            keys_ref[pl.ds(off, ck), :] = jnp.where(tied & (rank >= want), msb, kk)
            return before + jnp.sum(ind, axis=0, keepdims=True)

        lax.fori_loop(0, n_chunks, demote, jnp.zeros((1, tq), F32))

    t_fin = jnp.maximum(t_s, msb + 1)
    qs = []
    for h in range(N_HEADS):
        pair, half = divmod(h, 2)
        qs.append(_half_mask(qc_ref[0, :, pair * LANES:(pair + 1) * LANES], half))
    acc_ref[...] = jnp.zeros(acc_ref.shape, F32)

    def scores(c, s_ref):
        off = pl.multiple_of(c * ck, ck)
        for h in range(N_HEADS):
            k = kc_ref[0, pl.ds(off, ck), (h // 2) * LANES:(h // 2 + 1) * LANES]
            s_ref[h] = _dot_nt(k, qs[h])

    def consume(c, s_ref, carry):
        ms, ls = carry
        off = pl.multiple_of(c * ck, ck)
        sel = keys_ref[pl.ds(off, ck), :] >= t_fin
        new_ms, new_ls = [], []
        for h in range(N_HEADS):
            pair = h // 2
            p, m_new, l8, alpha = _softmax_step(s_ref[h], ms[h], ls[h], sel)
            vt = vct_ref[0, h * HEAD_DIM:(h + 1) * HEAD_DIM, pl.ds(off, ck)]
            acc_ref[h] = alpha * acc_ref[h] + _dot(vt, p)
            new_ms.append(m_new)
            new_ls.append(l8)
        return tuple(new_ms), tuple(new_ls)

    def pair_body(p, carry):
        c = 2 * p
        scores(c + 1, sb_ref)
        carry = consume(c, sa_ref, carry)
        scores(c + 2, sa_ref)
        return consume(c + 1, sb_ref, carry)

    carry = (tuple(jnp.full((1, tq), NEG, F32) for _ in range(N_HEADS)),
             tuple(jnp.zeros((SUBLANES, tq), F32) for _ in range(N_HEADS)))
    scores(0, sa_ref)
    carry = lax.fori_loop(0, n_full // 2, pair_body, carry)

    def tail_odd(carry):
        scores(n_full, sb_ref)
        return consume(n_full, sb_ref, consume(n_full - 1, sa_ref, carry))

    _, ls = lax.cond(n_full % 2 == 1, tail_odd, lambda cr: consume(n_full, sa_ref, cr), carry)
    for pair in range(2):
        o0 = acc_ref[2 * pair] / jnp.sum(ls[2 * pair], axis=0, keepdims=True)
        o1 = acc_ref[2 * pair + 1] / jnp.sum(ls[2 * pair + 1], axis=0, keepdims=True)
        o_t = jnp.concatenate([o0, o1], axis=0)
        o_ref[0, :, pair * LANES:(pair + 1) * LANES] = o_t.T.astype(o_ref.dtype)


def _dsa(qi, misc_t, ki, qc, kc, vc_t, *, tq, ck, topk):
    batch, seq, _ = qc.shape
    return pl.pallas_call(
        functools.partial(_dsa_kernel, tq=tq, ck=ck, topk=topk, seq=seq),
        grid=(batch, seq // tq),
        in_specs=[pl.BlockSpec((1, tq, IDX_HEADS * IDX_DIM), lambda b, i: (b, i, 0)),
                  pl.BlockSpec((1, MISC_ROWS, tq), lambda b, i: (b, 0, i)),
                  pl.BlockSpec((1, seq, LANES), lambda b, i: (b, 0, 0)),
                  pl.BlockSpec((1, tq, GROUP_WIDTH), lambda b, i: (b, i, 0)),
                  pl.BlockSpec((1, seq, GROUP_WIDTH), lambda b, i: (b, 0, 0)),
                  pl.BlockSpec((1, GROUP_WIDTH, seq), lambda b, i: (b, 0, 0))],
        out_specs=pl.BlockSpec((1, tq, GROUP_WIDTH), lambda b, i: (b, i, 0)),
        out_shape=jax.ShapeDtypeStruct((batch, seq, GROUP_WIDTH), BF16),
        scratch_shapes=[pltpu.VMEM((seq, tq), I32),
                        pltpu.VMEM((N_HEADS, HEAD_DIM, tq), F32),
                        pltpu.VMEM((N_HEADS, ck, tq), F32),
                        pltpu.VMEM((N_HEADS, ck, tq), F32)],
        compiler_params=pltpu.CompilerParams(
            dimension_semantics=("parallel", "arbitrary"), vmem_limit_bytes=VMEM_LIMIT),
        name="dsa",
    )(qi, misc_t, ki, qc, kc, vc_t)


def _mlp_kernel(oa_ref, ob_ref, oc_ref, od_ref, x_ref, wo_ref, g2_ref, wup_ref, wdn_ref,
                gf_ref, out_ref, *, ff_chunk, final):
    gw = GROUP_WIDTH
    mix = None
    for g, o_ref in enumerate((oa_ref, ob_ref, oc_ref, od_ref)):
        y = _dot(o_ref[...], wo_ref[g * gw:(g + 1) * gw, :])
        mix = y if mix is None else mix + y
    x1 = x_ref[...] + mix
    h2 = _rms(x1, g2_ref[...]).astype(BF16)
    ffn = None
    for c in range(wup_ref.shape[1] // ff_chunk):
        u = _dot(h2, wup_ref[:, c * ff_chunk:(c + 1) * ff_chunk])
        a = jnp.square(jnp.maximum(u, 0.0)).astype(BF16)
        y = _dot(a, wdn_ref[c * ff_chunk:(c + 1) * ff_chunk, :])
        ffn = y if ffn is None else ffn + y
    acc = x1 + ffn
    if final:
        acc = _rms(acc, gf_ref[...])
    out_ref[...] = acc


def _mlp(oa, ob, oc, od, x2d, wo, g2, wup, wdn, gf, *, tm, final):
    n, d = x2d.shape
    dff = wup.shape[1]
    row = lambda i: (i, 0)
    const = lambda i: (0, 0)
    once = pl.Buffered(1)
    o_spec = pl.BlockSpec((tm, GROUP_WIDTH), row)
    return pl.pallas_call(
        functools.partial(_mlp_kernel, ff_chunk=1024, final=final),
        grid=(n // tm,),
        in_specs=[o_spec, o_spec, o_spec, o_spec,
                  pl.BlockSpec((tm, d), row),
                  pl.BlockSpec((d, d), const, pipeline_mode=once),
                  pl.BlockSpec((1, d), const),
                  pl.BlockSpec((d, dff), const, pipeline_mode=once),
                  pl.BlockSpec((dff, d), const, pipeline_mode=once),
                  pl.BlockSpec((1, d), const)],
        out_specs=pl.BlockSpec((tm, d), row),
        out_shape=jax.ShapeDtypeStruct((n, d), F32),
        compiler_params=pltpu.CompilerParams(
            dimension_semantics=("parallel",), vmem_limit_bytes=VMEM_LIMIT),
        name="outproj_mlp",
    )(oa, ob, oc, od, x2d, wo, g2, wup, wdn, gf)


def _rope_tables(seq):
    pos = jnp.arange(seq, dtype=F32)[:, None]
    lane = jnp.arange(LANES)

    def table(dim, active):
        half = dim // 2
        inv_freq = 1.0 / (ROPE_THETA ** (jnp.arange(0, half, dtype=F32) * 2.0 / dim))
        ang = pos * inv_freq[None, :][:, lane % half]
        sign = jnp.where((lane % dim) < half, -1.0, 1.0)
        cos = jnp.where(active, jnp.cos(ang), 1.0)
        sin = jnp.where(active, jnp.sin(ang) * sign, 0.0)
        return cos.astype(F32), sin.astype(F32)

    c64, s64 = table(HEAD_DIM, jnp.ones((LANES,), bool))
    c32, s32 = table(IDX_DIM, jnp.ones((LANES,), bool))
    cm, sm = table(MLA_ROPE, (lane >= MLA_NOPE) & (lane < MLA_NOPE + MLA_ROPE))
    return c64, s64, c32, s32, cm, sm


def _pack_weights(w_in, mla_w_uq, mla_w_ukv, w_out):
    depth, d, _ = w_in.shape
    splits = (MLA_Q_RANK, MLA_KV_RANK, MLA_ROPE,
              GROUP_WIDTH, SWA_KV_HEADS * HEAD_DIM, SWA_KV_HEADS * HEAD_DIM,
              GROUP_WIDTH, GROUP_WIDTH, GROUP_WIDTH, IDX_HEADS * IDX_DIM, IDX_DIM, IDX_HEADS,
              GROUP_WIDTH, GROUP_WIDTH, GROUP_WIDTH, N_HEADS)
    offs = [0]
    for s in splits:
        offs.append(offs[-1] + s)
    (a_cq, a_ckv, a_kr, b_q, b_k, b_v, c_q, c_k, c_v, c_qi, c_ki, c_w,
     d_q, d_k, d_v, d_f) = [w_in[:, :, offs[j]:offs[j + 1]] for j in range(len(splits))]
    zeros = lambda n: jnp.zeros((depth, d, n), w_in.dtype)
    swap = jnp.array([0, 2, 1, 3])
    a_kr = jnp.concatenate([zeros(MLA_NOPE), a_kr, zeros(LANES - MLA_NOPE - MLA_ROPE)], -1)
    b_q = b_q.reshape(depth, d, N_HEADS, HEAD_DIM)[:, :, swap].reshape(depth, d, GROUP_WIDTH)
    c_ki = jnp.concatenate([c_ki] * (LANES // IDX_DIM), -1)
    w_row = jnp.concatenate(
        [a_cq, a_ckv, a_kr, b_q, b_k, c_ki, c_q, c_k, c_qi, d_q, d_k], -1).astype(BF16)
    w_t = jnp.concatenate([c_v, d_v, b_v, c_w, d_f, zeros(MISC_ROWS - IDX_HEADS - N_HEADS)], -1)
    w_t = jnp.swapaxes(w_t, 1, 2).astype(BF16)

    pad = LANES - MLA_NOPE - MLA_ROPE
    wuq = mla_w_uq.reshape(depth, MLA_Q_RANK, N_HEADS, MLA_NOPE + MLA_ROPE)
    wuq = jnp.pad(wuq, ((0, 0), (0, 0), (0, 0), (0, pad))).reshape(depth, MLA_Q_RANK, N_HEADS * LANES)
    wukv = mla_w_ukv.reshape(depth, MLA_KV_RANK, N_HEADS, MLA_NOPE + HEAD_DIM)
    wuk = jnp.pad(wukv[..., :MLA_NOPE], ((0, 0), (0, 0), (0, 0), (0, LANES - MLA_NOPE)))
    wuk = wuk.reshape(depth, MLA_KV_RANK, N_HEADS * LANES)
    wuvt = jnp.swapaxes(wukv[..., MLA_NOPE:].reshape(depth, MLA_KV_RANK, GROUP_WIDTH), 1, 2)

    wo_b = w_out[:, GROUP_WIDTH:2 * GROUP_WIDTH].reshape(depth, N_HEADS, HEAD_DIM, -1)[:, swap]
    wo = jnp.concatenate([w_out[:, :GROUP_WIDTH], wo_b.reshape(depth, GROUP_WIDTH, -1),
                          w_out[:, 2 * GROUP_WIDTH:]], 1)
    return w_row, w_t, wuq.astype(BF16), wuk.astype(BF16), wuvt.astype(BF16), wo.astype(BF16)


_MLA_HEADS = tuple((h * LANES, None, h * LANES) for h in range(N_HEADS))
_FOX_HEADS = tuple(((h // 2) * LANES, h % 2, (h // 2) * LANES) for h in range(N_HEADS))


def kernel(x, norm1, w_in, mla_q_norm, mla_kv_norm, mla_w_uq, mla_w_ukv, swa_sinks, fox_b_f,
           w_out, norm2, w_up, w_down, final_norm):
    batch, seq, d = x.shape
    depth = w_in.shape[0]
    n = batch * seq
    top_k = min(DSA_TOPK, seq // 4)
    tm = min(512, seq)
    tq_flash = min(256, seq)
    ck_flash = min(256, seq)
    tq_swa = min(2048, seq)
    tq_dsa = min(256, seq)
    ck_dsa = min(256, seq)

    tables = _rope_tables(seq)
    w_row, w_t, wuq, wuk, wuvt, wo = _pack_weights(w_in, mla_w_uq, mla_w_ukv, w_out)
    wup = w_up.astype(BF16)
    wdn = w_down.astype(BF16)

    x2d = x.reshape(n, d)
    for l in range(depth):
        (qa, ka, qb, kb, qc, kc, qi, ki, qd, kd, va_t, vc_t, vd_t, vb_t, misc_t) = _inproj(
            x2d, norm1[l][None], w_row[l], w_t[l], mla_q_norm[l][None], mla_kv_norm[l][None],
            wuq[l], wuk[l], wuvt[l], tables, batch=batch, seq=seq, tm=tm)
        r3 = lambda a: a.reshape(batch, seq, a.shape[-1])
        cum = _forget_cumsum(misc_t[:, IDX_HEADS:IDX_HEADS + N_HEADS], fox_b_f[l],
                             batch=batch, seq=seq)
        cum_col = jnp.swapaxes(cum, 1, 2)
        o_a = _flash(r3(qa), r3(ka), va_t, None, _MLA_HEADS, tq=tq_flash, ck=ck_flash)
        o_b = _swa(r3(qb), r3(kb), vb_t, swa_sinks[l], tq=tq_swa)
        o_c = _dsa(r3(qi), misc_t, r3(ki), r3(qc), r3(kc), vc_t, tq=tq_dsa, ck=ck_dsa, topk=top_k)
        o_d = _flash(r3(qd), r3(kd), vd_t, cum_col, _FOX_HEADS, tq=tq_flash, ck=ck_flash)
        x2d = _mlp(o_a.reshape(n, -1), o_b.reshape(n, -1), o_c.reshape(n, -1), o_d.reshape(n, -1),
                   x2d, wo[l], norm2[l][None], wup[l], wdn[l], final_norm[None],
                   tm=tm, final=(l == depth - 1))
    return x2d.reshape(batch, seq, d)
```

```python
import functools
import math

import jax
import jax.numpy as jnp
from jax import lax
from jax.experimental import pallas as pl
from jax.experimental.pallas import tpu as pltpu

F32 = jnp.float32
BF16 = jnp.bfloat16
I32 = jnp.int32

HEAD_DIM = 64
N_HEADS = 4
GROUP_WIDTH = N_HEADS * HEAD_DIM
MLA_Q_RANK = 256
MLA_KV_RANK = 128
MLA_NOPE = 64
MLA_ROPE = 32
SWA_KV_HEADS = 2
SWA_WINDOW = 128
IDX_HEADS = 8
IDX_DIM = 32
DSA_TOPK = 256
ROPE_THETA = 10000.0
EPS = 1e-6

LANES = 128
SUBLANES = 8
VMEM_LIMIT = 56 * 1024 * 1024
NEG = -1e30
INT_MIN = -(2 ** 31)
LOG2E = math.log2(math.e)
BISECT_STEPS = 20

_ROW_GROUPS = (
    ("a_cq", 256), ("a_ckv", 128), ("a_kr", 128),
    ("b_q", 256), ("b_k", 128), ("c_ki", 128),
    ("c_q", 256), ("c_k", 256), ("c_qi", 256),
    ("d_q", 256), ("d_k", 256),
)
_ROW_OFF = {}
_acc = 0
for _n, _w in _ROW_GROUPS:
    _ROW_OFF[_n] = (_acc, _acc + _w)
    _acc += _w
ROW_WIDTH = _acc
MISC_ROWS = 16
SWA_KV_WIDTH = SWA_KV_HEADS * HEAD_DIM
T_ROWS = 2 * GROUP_WIDTH + SWA_KV_WIDTH + MISC_ROWS


def _dot(a, b):
    return jnp.dot(a, b, preferred_element_type=F32)


def _dot_nt(a, b):
    return lax.dot_general(a, b, (((1,), (1,)), ((), ())), preferred_element_type=F32)


def _rms(x, g):
    return x * lax.rsqrt(jnp.mean(x * x, axis=-1, keepdims=True) + EPS) * g


def _rope(x, cos, sin, half):
    width = x.shape[1]
    reps = width // LANES
    if reps > 1:
        cos = jnp.concatenate([cos] * reps, axis=1)
        sin = jnp.concatenate([sin] * reps, axis=1)
    lane = lax.broadcasted_iota(I32, x.shape, 1)
    first = (lane % (2 * half)) < half
    rot = jnp.where(first, pltpu.roll(x, width - half, 1), pltpu.roll(x, half, 1))
    return x * cos + rot * sin


def _half_mask(q, half):
    lane = lax.broadcasted_iota(I32, q.shape, 1)
    return jnp.where((lane >= HEAD_DIM) == bool(half), q, jnp.zeros_like(q))


def _softmax_step(s, m, l8, mask):
    ck, tq = s.shape
    if mask is not None:
        s = jnp.where(mask, s, NEG)
    m_new = jnp.maximum(m, jnp.max(s, axis=0, keepdims=True))
    alpha = jnp.exp2(m - m_new)
    p = jnp.exp2(s - m_new)
    l8 = alpha * l8 + jnp.sum(p.reshape(ck // SUBLANES, SUBLANES, tq), axis=0)
    return p.astype(BF16), m_new, l8, alpha


def _inproj_kernel(x_ref, g1_ref, w_ref, wt_ref, qn_ref, kvn_ref, wuq_ref, wuk_ref, wuvt_ref,
                   c64_ref, s64_ref, c32_ref, s32_ref, cm_ref, sm_ref,
                   qa_ref, ka_ref, qb_ref, kb_ref,
                   qc_ref, kc_ref, qi_ref, ki_ref, qd_ref, kd_ref,
                   vat_ref, vct_ref, vdt_ref, vbt_ref, misct_ref):
    h = _rms(x_ref[...], g1_ref[...]).astype(BF16)

    def proj(name):
        lo, hi = _ROW_OFF[name]
        return _dot(h, w_ref[:, lo:hi])

    def proj_pair(name0, name1):
        lo, mid = _ROW_OFF[name0]
        assert _ROW_OFF[name1][0] == mid
        both = _dot(h, w_ref[:, lo:_ROW_OFF[name1][1]])
        return both[:, :mid - lo], both[:, mid - lo:]

    c64, s64 = c64_ref[...], s64_ref[...]
    c32, s32 = c32_ref[...], s32_ref[...]
    cm, sm = cm_ref[...], sm_ref[...]
    scale_a = (MLA_NOPE + MLA_ROPE) ** -0.5 * LOG2E
    scale = HEAD_DIM ** -0.5 * LOG2E

    cq = _rms(proj("a_cq"), qn_ref[...]).astype(BF16)
    qa = _rope(_dot(cq, wuq_ref[...]), cm, sm, MLA_ROPE // 2) * scale_a
    qa_ref[...] = qa.astype(BF16)
    ckv, kr = proj_pair("a_ckv", "a_kr")
    ckv = _rms(ckv, kvn_ref[...]).astype(BF16)
    kr = _rope(kr, cm, sm, MLA_ROPE // 2)
    ka_ref[...] = (_dot(ckv, wuk_ref[...]) + jnp.concatenate([kr] * N_HEADS, axis=1)).astype(BF16)
    vat_ref[0] = _dot_nt(wuvt_ref[...], ckv).astype(BF16)

    qb_ref[...] = (_rope(proj("b_q"), c64, s64, HEAD_DIM // 2) * scale).astype(BF16)
    kb, ki = proj_pair("b_k", "c_ki")
    kb_ref[...] = _rope(kb, c64, s64, HEAD_DIM // 2).astype(BF16)

    qc_ref[...] = (_rope(proj("c_q"), c64, s64, HEAD_DIM // 2) * scale).astype(BF16)
    kc_ref[...] = _rope(proj("c_k"), c64, s64, HEAD_DIM // 2).astype(BF16)
    qi_ref[...] = _rope(proj("c_qi"), c32, s32, IDX_DIM // 2).astype(BF16)
    ki_ref[...] = _rope(ki, c32, s32, IDX_DIM // 2).astype(BF16)

    qd_ref[...] = (proj("d_q") * scale).astype(BF16)
    kd_ref[...] = proj("d_k").astype(BF16)

    t = _dot_nt(wt_ref[...], h)
    vct_ref[0] = t[:GROUP_WIDTH].astype(BF16)
    vdt_ref[0] = t[GROUP_WIDTH:2 * GROUP_WIDTH].astype(BF16)
    vbt_ref[0] = t[2 * GROUP_WIDTH:2 * GROUP_WIDTH + SWA_KV_WIDTH].astype(BF16)
    misct_ref[0] = t[2 * GROUP_WIDTH + SWA_KV_WIDTH:]


def _inproj(x2d, g1, w_row, w_t, qn, kvn, wuq, wuk, wuvt, tables, *, batch, seq, tm):
    n = x2d.shape[0]
    d = x2d.shape[1]
    spb = seq // tm
    row = lambda i: (i, 0)
    const = lambda i: (0, 0)
    tab = lambda i: (i % spb, 0)
    tr = lambda i: (i // spb, 0, i % spb)
    once = pl.Buffered(1)

    def out2(width):
        return jax.ShapeDtypeStruct((n, width), BF16), pl.BlockSpec((tm, width), row)

    def out_t(rows, dtype):
        return jax.ShapeDtypeStruct((batch, rows, seq), dtype), pl.BlockSpec((1, rows, tm), tr)

    outs = [out2(512), out2(512),
            out2(256), out2(128),
            out2(256), out2(256), out2(256), out2(128),
            out2(256), out2(256),
            out_t(GROUP_WIDTH, BF16), out_t(GROUP_WIDTH, BF16), out_t(GROUP_WIDTH, BF16),
            out_t(SWA_KV_WIDTH, BF16), out_t(MISC_ROWS, F32)]
    in_specs = [
        pl.BlockSpec((tm, d), row),
        pl.BlockSpec((1, d), const),
        pl.BlockSpec((d, ROW_WIDTH), const, pipeline_mode=once),
        pl.BlockSpec((T_ROWS, d), const, pipeline_mode=once),
        pl.BlockSpec((1, MLA_Q_RANK), const),
        pl.BlockSpec((1, MLA_KV_RANK), const),
        pl.BlockSpec((MLA_Q_RANK, N_HEADS * LANES), const, pipeline_mode=once),
        pl.BlockSpec((MLA_KV_RANK, N_HEADS * LANES), const, pipeline_mode=once),
        pl.BlockSpec((GROUP_WIDTH, MLA_KV_RANK), const, pipeline_mode=once),
    ] + [pl.BlockSpec((tm, LANES), tab)] * 6
    return pl.pallas_call(
        _inproj_kernel,
        grid=(n // tm,),
        in_specs=in_specs,
        out_specs=[o[1] for o in outs],
        out_shape=[o[0] for o in outs],
        compiler_params=pltpu.CompilerParams(
            dimension_semantics=("parallel",), vmem_limit_bytes=VMEM_LIMIT),
        name="inproj",
    )(x2d, g1, w_row, w_t, qn, kvn, wuq, wuk, wuvt, *tables)


def _cumsum_kernel(f_ref, b_ref, o_ref, *, segs):
    x = f_ref[...] + b_ref[...]
    ls = jnp.minimum(x, 0.0) - jnp.log1p(jnp.exp(-jnp.abs(x)))
    rows = x.shape[0]
    r = lax.broadcasted_iota(I32, (LANES, LANES), 0)
    c = lax.broadcasted_iota(I32, (LANES, LANES), 1)
    upper = (r <= c).astype(F32)
    within = jnp.dot(ls, upper, preferred_element_type=F32, precision=lax.Precision.HIGHEST)
    tot = jnp.broadcast_to(within[:, LANES - 1:LANES], (rows, LANES))
    rr = lax.broadcasted_iota(I32, (rows, rows), 0)
    cc = lax.broadcasted_iota(I32, (rows, rows), 1)
    before = ((cc < rr) & (cc // segs == rr // segs)).astype(F32)
    off = jnp.dot(before, tot, preferred_element_type=F32, precision=lax.Precision.HIGHEST)
    o_ref[...] = (within + off) * LOG2E


def _forget_cumsum(f_t, bias, *, batch, seq):
    segs = seq // LANES
    rows = N_HEADS * segs
    f2 = f_t.reshape(batch * rows, LANES)
    b2 = jnp.broadcast_to(jnp.repeat(bias, segs)[:, None], (rows, LANES))
    out = pl.pallas_call(
        functools.partial(_cumsum_kernel, segs=segs),
        grid=(batch,),
        in_specs=[pl.BlockSpec((rows, LANES), lambda b: (b, 0)),
                  pl.BlockSpec((rows, LANES), lambda b: (0, 0))],
        out_specs=pl.BlockSpec((rows, LANES), lambda b: (b, 0)),
        out_shape=jax.ShapeDtypeStruct((batch * rows, LANES), F32),
        compiler_params=pltpu.CompilerParams(dimension_semantics=("parallel",)),
        name="forget_cumsum",
    )(f2, b2)
    return out.reshape(batch, N_HEADS, seq)


def _flash_kernel(*refs, heads, tq, ck, has_bias, tiles):
    if has_bias:
        q_ref, k_ref, vt_ref, b_ref, o_ref, acc_ref, sa_ref, sb_ref = refs
    else:
        q_ref, k_ref, vt_ref, o_ref, acc_ref, sa_ref, sb_ref = refs
        b_ref = None
    assert tq == ck
    nh = len(heads)

    def one_tile(t, unused):
        i = pl.program_id(1) * tiles + t
        rows = pl.ds(pl.multiple_of(t * tq, tq), tq)
        q_start = i * tq
        n_full = q_start // ck
        kpos_l = lax.broadcasted_iota(I32, (ck, tq), 0)
        qpos = q_start + lax.broadcasted_iota(I32, (ck, tq), 1)
        qs = []
        for ql, qhalf, _ in heads:
            q = q_ref[0, rows, ql:ql + LANES]
            qs.append(q if qhalf is None else _half_mask(q, qhalf))
        acc_ref[...] = jnp.zeros(acc_ref.shape, F32)

        def scores(c, s_ref):
            off = pl.multiple_of(c * ck, ck)
            for h, (_, _, kl) in enumerate(heads):
                s = _dot_nt(k_ref[0, pl.ds(off, ck), kl:kl + LANES], qs[h])
                if has_bias:
                    s = s - b_ref[0, pl.ds(off, ck), h:h + 1]
                s_ref[h] = s

        def consume(c, s_ref, carry, masked):
            ms, ls = carry
            off = pl.multiple_of(c * ck, ck)
            mask = (off + kpos_l <= qpos) if masked else None
            new_ms, new_ls = [], []
            for h in range(nh):
                p, m_new, l8, alpha = _softmax_step(s_ref[h], ms[h], ls[h], mask)
                vt = vt_ref[0, h * HEAD_DIM:(h + 1) * HEAD_DIM, pl.ds(off, ck)]
                acc_ref[h] = alpha * acc_ref[h] + _dot(vt, p)
                new_ms.append(m_new)
                new_ls.append(l8)
            return tuple(new_ms), tuple(new_ls)

        def pair_body(p, carry):
            c = 2 * p
            scores(c + 1, sb_ref)
            carry = consume(c, sa_ref, carry, False)
            scores(c + 2, sa_ref)
            return consume(c + 1, sb_ref, carry, False)

        carry = (tuple(jnp.full((1, tq), NEG, F32) for _ in range(nh)),
                 tuple(jnp.zeros((SUBLANES, tq), F32) for _ in range(nh)))
        scores(0, sa_ref)
        carry = lax.fori_loop(0, n_full // 2, pair_body, carry)
        odd = n_full % 2 == 1

        def tail_odd(carry):
            scores(n_full, sb_ref)
            carry = consume(n_full - 1, sa_ref, carry, False)
            return consume(n_full, sb_ref, carry, True)

        carry = lax.cond(odd, tail_odd, lambda cr: consume(n_full, sa_ref, cr, True), carry)
        _, ls = carry
        for pair in range(nh // 2):
            o0 = acc_ref[2 * pair] / jnp.sum(ls[2 * pair], axis=0, keepdims=True)
            o1 = acc_ref[2 * pair + 1] / jnp.sum(ls[2 * pair + 1], axis=0, keepdims=True)
            o_t = jnp.concatenate([o0, o1], axis=0)
            o_ref[0, rows, pair * LANES:(pair + 1) * LANES] = o_t.T.astype(o_ref.dtype)
        return unused

    lax.fori_loop(0, tiles, one_tile, 0)


def _flash(q, k, v_t, bias, heads, *, tq, ck, tiles):
    batch, seq, wq = q.shape
    wk = k.shape[2]
    has_bias = bias is not None
    in_specs = [pl.BlockSpec((1, tiles * tq, wq), lambda b, i: (b, i, 0)),
                pl.BlockSpec((1, seq, wk), lambda b, i: (b, 0, 0)),
                pl.BlockSpec((1, GROUP_WIDTH, seq), lambda b, i: (b, 0, 0))]
    args = [q, k, v_t]
    if has_bias:
        in_specs.append(pl.BlockSpec((1, seq, N_HEADS), lambda b, i: (b, 0, 0)))
        args.append(bias)
    return pl.pallas_call(
        functools.partial(_flash_kernel, heads=heads, tq=tq, ck=ck, has_bias=has_bias, tiles=tiles),
        grid=(batch, seq // (tiles * tq)),
        in_specs=in_specs,
        out_specs=pl.BlockSpec((1, tiles * tq, GROUP_WIDTH), lambda b, i: (b, i, 0)),
        out_shape=jax.ShapeDtypeStruct((batch, seq, GROUP_WIDTH), BF16),
        scratch_shapes=[pltpu.VMEM((len(heads), HEAD_DIM, tq), F32),
                        pltpu.VMEM((len(heads), ck, tq), F32),
                        pltpu.VMEM((len(heads), ck, tq), F32)],
        compiler_params=pltpu.CompilerParams(
            dimension_semantics=("parallel", "arbitrary"), vmem_limit_bytes=VMEM_LIMIT),
        name="flash_bias" if has_bias else "flash",
    )(*args)


def _swa_kernel(sink_ref, q_ref, k_ref, vt_ref, o_ref, sa_ref, sb_ref, *, tq, window):
    i = pl.program_id(1)
    span = 2 * window
    n_sub = tq // window
    assert n_sub % 2 == 0

    def key_start(j):
        return pl.multiple_of(jnp.maximum(i * tq + j * window - window, 0), window)

    def scores(j, s_ref):
        j = jnp.minimum(j, n_sub - 1)
        k = k_ref[0, pl.ds(key_start(j), span), :]
        rows = pl.ds(pl.multiple_of(j * window, window), window)
        for pair in range(2):
            qp = q_ref[0, rows, pair * LANES:(pair + 1) * LANES]
            for half in range(2):
                s_ref[2 * pair + half] = _dot_nt(k, _half_mask(qp, half))

    def consume(j, s_ref):
        start = key_start(j)
        kpos = start + lax.broadcasted_iota(I32, (span, window), 0)
        qpos = i * tq + j * window + lax.broadcasted_iota(I32, (span, window), 1)
        valid = (kpos <= qpos) & (qpos - kpos < window)
        rows = pl.ds(pl.multiple_of(j * window, window), window)
        for pair in range(2):
            outs = []
            for half in range(2):
                head = pair + 2 * half
                sink = sink_ref[head] * LOG2E
                s = jnp.where(valid, s_ref[2 * pair + half], NEG)
                m = jnp.maximum(jnp.max(s, axis=0, keepdims=True), sink)
                p = jnp.exp2(s - m)
                denom = jnp.sum(p, axis=0, keepdims=True) + jnp.exp2(sink - m)
                vt = vt_ref[0, half * HEAD_DIM:(half + 1) * HEAD_DIM, pl.ds(start, span)]
                outs.append(_dot(vt, p.astype(BF16)) / denom)
            o_t = jnp.concatenate(outs, axis=0)
            o_ref[0, rows, pair * LANES:(pair + 1) * LANES] = o_t.T.astype(o_ref.dtype)

    def pair_body(p, carry):
        j = 2 * p
        scores(j + 1, sb_ref)
        consume(j, sa_ref)
        scores(j + 2, sa_ref)
        consume(j + 1, sb_ref)
        return carry

    scores(0, sa_ref)
    lax.fori_loop(0, n_sub // 2, pair_body, 0)


def _swa(q, k, v_t, sinks, *, tq):
    batch, seq, _ = q.shape
    return pl.pallas_call(
        functools.partial(_swa_kernel, tq=tq, window=SWA_WINDOW),
        grid=(batch, seq // tq),
        in_specs=[pl.BlockSpec(memory_space=pltpu.SMEM),
                  pl.BlockSpec((1, tq, GROUP_WIDTH), lambda b, i: (b, i, 0)),
                  pl.BlockSpec((1, seq, LANES), lambda b, i: (b, 0, 0)),
                  pl.BlockSpec((1, SWA_KV_HEADS * HEAD_DIM, seq), lambda b, i: (b, 0, 0))],
        out_specs=pl.BlockSpec((1, tq, GROUP_WIDTH), lambda b, i: (b, i, 0)),
        out_shape=jax.ShapeDtypeStruct((batch, seq, GROUP_WIDTH), BF16),
        scratch_shapes=[pltpu.VMEM((N_HEADS, 2 * SWA_WINDOW, SWA_WINDOW), F32),
                        pltpu.VMEM((N_HEADS, 2 * SWA_WINDOW, SWA_WINDOW), F32)],
        compiler_params=pltpu.CompilerParams(
            dimension_semantics=("parallel", "arbitrary"), vmem_limit_bytes=VMEM_LIMIT),
        name="swa",
    )(sinks, q, k, v_t)


def _dsa_kernel(qi_ref, w_ref, ki_ref, qc_ref, kc_ref, vct_ref, o_ref,
                keys_ref, acc_ref, sa_ref, sb_ref, *, tq, ck, topk, seq):
    assert tq == ck
    i = pl.program_id(1)
    q_start = i * tq
    n_chunks = (q_start + tq) // ck
    n_full = q_start // ck
    lane = lax.broadcasted_iota(I32, (tq, LANES), 1)
    kpos_l = lax.broadcasted_iota(I32, (ck, tq), 0)
    qpos = q_start + lax.broadcasted_iota(I32, (ck, tq), 1)
    qpos_row = q_start + lax.broadcasted_iota(I32, (1, tq), 1)
    msb = jnp.int32(INT_MIN)

    qm = []
    for h in range(IDX_HEADS):
        g, r = divmod(h, LANES // IDX_DIM)
        qg = qi_ref[0, :, g * LANES:(g + 1) * LANES]
        qm.append(jnp.where(lane // IDX_DIM == r, qg, jnp.zeros_like(qg)))
    w = w_ref[0]

    def flip(v):
        return v ^ ((v >> 31) & jnp.int32(0x7FFFFFFF))

    def group_reduce(fn, v):
        quarter = v.shape[0] // 4
        parts = [fn(v[j * quarter:(j + 1) * quarter].reshape(quarter // SUBLANES, SUBLANES, tq), axis=0)
                 for j in range(4)]
        return fn(jnp.stack([fn(jnp.stack(parts[:2]), axis=0), fn(jnp.stack(parts[2:]), axis=0)]),
                  axis=0)

    half_heads = IDX_HEADS // 2

    def index_dots(c, first, s_ref):
        ki = ki_ref[0, pl.ds(pl.multiple_of(c * ck, ck), ck), :]
        for j in range(half_heads):
            s_ref[j] = _dot_nt(ki, qm[first + j])

    def weighted_relu(first, s_ref):
        acc = None
        for j in range(half_heads):
            term = jnp.maximum(s_ref[j], 0.0) * w[first + j:first + j + 1, :]
            acc = term if acc is None else acc + term
        return acc

    def score_chunk(c, carry, last):
        kmax8, kmin8 = carry
        off = pl.multiple_of(c * ck, ck)
        index_dots(c, half_heads, sb_ref)
        acc = weighted_relu(0, sa_ref)
        if not last:
            index_dots(c + 1, 0, sa_ref)
        acc = acc + weighted_relu(half_heads, sb_ref)
        key = flip(lax.bitcast_convert_type(acc, I32))
        key_lo = key
        if last:
            causal = off + kpos_l <= qpos
            key_lo = jnp.where(causal, key, jnp.int32(2 ** 31 - 1))
            key = jnp.where(causal, key, msb)
        keys_ref[pl.ds(off, ck), :] = key
        return (jnp.maximum(kmax8, group_reduce(jnp.max, key)),
                jnp.minimum(kmin8, group_reduce(jnp.min, key_lo)))

    ext = (jnp.full((SUBLANES, tq), INT_MIN, I32), jnp.full((SUBLANES, tq), 2 ** 31 - 1, I32))
    index_dots(0, 0, sa_ref)
    ext = lax.fori_loop(0, n_full, lambda c, cr: score_chunk(c, cr, False), ext)
    ext = score_chunk(n_full, ext, True)
    kmax = jnp.max(ext[0], axis=0, keepdims=True)
    kmin = jnp.min(ext[1], axis=0, keepdims=True)

    qs = []
    for h in range(N_HEADS):
        pair, half = divmod(h, 2)
        qs.append(_half_mask(qc_ref[0, :, pair * LANES:(pair + 1) * LANES], half))

    def scores(c, s_ref):
        off = pl.multiple_of(c * ck, ck)
        for h in range(N_HEADS):
            k = kc_ref[0, pl.ds(off, ck), (h // 2) * LANES:(h // 2 + 1) * LANES]
            s_ref[h] = _dot_nt(k, qs[h])

    scores(0, sa_ref)

    cb = 2 * ck
    n_blocks = (n_chunks + 1) // 2

    @pl.when(n_chunks % 2 == 1)
    def _():
        keys_ref[pl.ds(pl.multiple_of(n_chunks * ck, ck), ck), :] = jnp.full((ck, tq), INT_MIN, I32)

    def key_blocks(fn, init):
        def body(b, carry):
            off = pl.multiple_of(b * cb, cb)
            return fn(keys_ref[pl.ds(off, cb), :], off, carry)
        return lax.fori_loop(0, n_blocks, body, init)

    def count(pred):
        part = key_blocks(
            lambda kk, off, part: part + group_reduce(jnp.sum, jnp.where(pred(kk, off), 1, 0)),
            jnp.zeros((SUBLANES, tq), I32))
        return jnp.sum(part, axis=0, keepdims=True)

    need = qpos_row >= topk

    def settle(state, cand_k, c, lo_k, hi_k):
        done, tie, t_res = state
        hit = c == topk
        fin = (done == 0) & (hit | (hi_k - lo_k == 1))
        t_res = jnp.where(fin, jnp.where(hit, cand_k, lo_k), t_res)
        tie = jnp.where(fin & ~hit, 1, tie)
        return jnp.where(fin, 1, done), tie, t_res

    def bisect_step(state):
        it, flags, lo_k, hi_k, c_hi = state
        cand_k = lo_k + lax.shift_right_logical(hi_k - lo_k, 1)
        c = count(lambda kk, off: kk >= cand_k)
        ge = c >= topk
        lo_k = jnp.where(ge, cand_k, lo_k)
        hi_k = jnp.where(ge, hi_k, cand_k)
        return it + 1, settle(flags, cand_k, c, lo_k, hi_k), lo_k, hi_k, jnp.where(ge, c_hi, c)

    def snap_step(state):
        it, flags, lo_k, hi_k, c_hi = state

        top8 = key_blocks(
            lambda kk, off, part: jnp.maximum(
                part, group_reduce(jnp.max, jnp.where(kk < hi_k, kk, msb))),
            jnp.full((SUBLANES, tq), INT_MIN, I32))
        cand_k = jnp.maximum(jnp.max(top8, axis=0, keepdims=True), lo_k)
        c = count(lambda kk, off: kk >= cand_k)
        ge = c >= topk
        lo_k = jnp.where(ge, cand_k, lo_k)
        hi_k = jnp.where(ge, cand_k + 1, cand_k)
        return it + 1, settle(flags, cand_k, c, lo_k, hi_k), lo_k, hi_k, jnp.where(ge, c_hi, c)

    def unsettled(state):
        return jnp.min(state[1][0]) == 0

    lo0, hi0 = kmin, kmax + 1
    flat0 = need & (hi0 - lo0 == 1)
    flags = (jnp.where(need & ~flat0, 0, 1), jnp.where(flat0, 1, 0), jnp.where(need, lo0, msb + 1))
    state = (jnp.int32(0), flags, lo0, hi0, jnp.zeros((1, tq), I32))
    state = lax.fori_loop(0, BISECT_STEPS, lambda _, st: bisect_step(st), state)
    state = bisect_step(snap_step(state))
    state = lax.while_loop(unsettled, lambda st: bisect_step(snap_step(st)), state)
    _, (_, tie_flag, t_s), _, _, n_gt = state
    tie = tie_flag > 0

    @pl.when(jnp.max(jnp.where(tie, 1, 0)) > 0)
    def _():
        want = (topk - n_gt).astype(F32)
        tri = (lax.broadcasted_iota(I32, (ck, ck), 0) > lax.broadcasted_iota(I32, (ck, ck), 1))
        tri = jnp.where(tri, 1.0, 0.0).astype(BF16)

        def demote(c, before):
            off = pl.multiple_of(c * ck, ck)
            kk = keys_ref[pl.ds(off, ck), :]
            tied = tie & (kk == t_s)
            ind = jnp.where(tied, 1.0, 0.0)
            rank = before + _dot(tri, ind.astype(BF16))
            keys_ref[pl.ds(off, ck), :] = jnp.where(tied & (rank >= want), msb, kk)
            return before + jnp.sum(ind, axis=0, keepdims=True)

        lax.fori_loop(0, n_chunks, demote, jnp.zeros((1, tq), F32))

    t_fin = jnp.maximum(t_s, msb + 1)
    acc_ref[...] = jnp.zeros(acc_ref.shape, F32)

    def consume(c, s_ref, carry):
        ms, ls = carry
        off = pl.multiple_of(c * ck, ck)
        sel = keys_ref[pl.ds(off, ck), :] >= t_fin
        new_ms, new_ls = [], []
        for h in range(N_HEADS):
            pair = h // 2
            p, m_new, l8, alpha = _softmax_step(s_ref[h], ms[h], ls[h], sel)
            vt = vct_ref[0, h * HEAD_DIM:(h + 1) * HEAD_DIM, pl.ds(off, ck)]
            acc_ref[h] = alpha * acc_ref[h] + _dot(vt, p)
            new_ms.append(m_new)
            new_ls.append(l8)
        return tuple(new_ms), tuple(new_ls)

    def pair_body(p, carry):
        c = 2 * p
        scores(c + 1, sb_ref)
        carry = consume(c, sa_ref, carry)
        scores(c + 2, sa_ref)
        return consume(c + 1, sb_ref, carry)

    carry = (tuple(jnp.full((1, tq), NEG, F32) for _ in range(N_HEADS)),
             tuple(jnp.zeros((SUBLANES, tq), F32) for _ in range(N_HEADS)))
    carry = lax.fori_loop(0, n_full // 2, pair_body, carry)

    def tail_odd(carry):
        scores(n_full, sb_ref)
        return consume(n_full, sb_ref, consume(n_full - 1, sa_ref, carry))

    _, ls = lax.cond(n_full % 2 == 1, tail_odd, lambda cr: consume(n_full, sa_ref, cr), carry)
    for pair in range(2):
        o0 = acc_ref[2 * pair] / jnp.sum(ls[2 * pair], axis=0, keepdims=True)
        o1 = acc_ref[2 * pair + 1] / jnp.sum(ls[2 * pair + 1], axis=0, keepdims=True)
        o_t = jnp.concatenate([o0, o1], axis=0)
        o_ref[0, :, pair * LANES:(pair + 1) * LANES] = o_t.T.astype(o_ref.dtype)


def _dsa(qi, misc_t, ki, qc, kc, vc_t, *, tq, ck, topk):
    batch, seq, _ = qc.shape
    return pl.pallas_call(
        functools.partial(_dsa_kernel, tq=tq, ck=ck, topk=topk, seq=seq),
        grid=(batch, seq // tq),
        in_specs=[pl.BlockSpec((1, tq, IDX_HEADS * IDX_DIM), lambda b, i: (b, i, 0)),
                  pl.BlockSpec((1, MISC_ROWS, tq), lambda b, i: (b, 0, i)),
                  pl.BlockSpec((1, seq, LANES), lambda b, i: (b, 0, 0)),
                  pl.BlockSpec((1, tq, GROUP_WIDTH), lambda b, i: (b, i, 0)),
                  pl.BlockSpec((1, seq, GROUP_WIDTH), lambda b, i: (b, 0, 0)),
                  pl.BlockSpec((1, GROUP_WIDTH, seq), lambda b, i: (b, 0, 0))],
        out_specs=pl.BlockSpec((1, tq, GROUP_WIDTH), lambda b, i: (b, i, 0)),
        out_shape=jax.ShapeDtypeStruct((batch, seq, GROUP_WIDTH), BF16),
        scratch_shapes=[pltpu.VMEM((seq, tq), I32),
                        pltpu.VMEM((N_HEADS, HEAD_DIM, tq), F32),
                        pltpu.VMEM((N_HEADS, ck, tq), F32),
                        pltpu.VMEM((N_HEADS, ck, tq), F32)],
        compiler_params=pltpu.CompilerParams(
            dimension_semantics=("parallel", "arbitrary"), vmem_limit_bytes=VMEM_LIMIT),
        name="dsa",
    )(qi, misc_t, ki, qc, kc, vc_t)


def _mlp_kernel(oa_ref, ob_ref, oc_ref, od_ref, x_ref, wo_ref, g2_ref, wup_ref, wdn_ref,
                gf_ref, out_ref, *, ff_chunk, final):
    gw = GROUP_WIDTH
    mix = None
    for g, o_ref in enumerate((oa_ref, ob_ref, oc_ref, od_ref)):
        y = _dot(o_ref[...], wo_ref[g * gw:(g + 1) * gw, :])
        mix = y if mix is None else mix + y
    x1 = x_ref[...] + mix
    h2 = _rms(x1, g2_ref[...]).astype(BF16)
    ffn = None
    for c in range(wup_ref.shape[1] // ff_chunk):
        u = _dot(h2, wup_ref[:, c * ff_chunk:(c + 1) * ff_chunk])
        a = jnp.square(jnp.maximum(u, 0.0)).astype(BF16)
        y = _dot(a, wdn_ref[c * ff_chunk:(c + 1) * ff_chunk, :])
        ffn = y if ffn is None else ffn + y
    acc = x1 + ffn
    if final:
        acc = _rms(acc, gf_ref[...])
    out_ref[...] = acc


def _mlp(oa, ob, oc, od, x2d, wo, g2, wup, wdn, gf, *, tm, final):
    n, d = x2d.shape
    dff = wup.shape[1]
    row = lambda i: (i, 0)
    const = lambda i: (0, 0)
    once = pl.Buffered(1)
    o_spec = pl.BlockSpec((tm, GROUP_WIDTH), row)
    return pl.pallas_call(
        functools.partial(_mlp_kernel, ff_chunk=1024, final=final),
        grid=(n // tm,),
        in_specs=[o_spec, o_spec, o_spec, o_spec,
                  pl.BlockSpec((tm, d), row),
                  pl.BlockSpec((d, d), const, pipeline_mode=once),
                  pl.BlockSpec((1, d), const),
                  pl.BlockSpec((d, dff), const, pipeline_mode=once),
                  pl.BlockSpec((dff, d), const, pipeline_mode=once),
                  pl.BlockSpec((1, d), const)],
        out_specs=pl.BlockSpec((tm, d), row),
        out_shape=jax.ShapeDtypeStruct((n, d), F32),
        compiler_params=pltpu.CompilerParams(
            dimension_semantics=("parallel",), vmem_limit_bytes=VMEM_LIMIT),
        name="outproj_mlp",
    )(oa, ob, oc, od, x2d, wo, g2, wup, wdn, gf)


def _rope_tables(seq):
    pos = jnp.arange(seq, dtype=F32)[:, None]
    lane = jnp.arange(LANES)

    def table(dim, active):
        half = dim // 2
        inv_freq = 1.0 / (ROPE_THETA ** (jnp.arange(0, half, dtype=F32) * 2.0 / dim))
        ang = pos * inv_freq[None, :][:, lane % half]
        sign = jnp.where((lane % dim) < half, -1.0, 1.0)
        cos = jnp.where(active, jnp.cos(ang), 1.0)
        sin = jnp.where(active, jnp.sin(ang) * sign, 0.0)
        return cos.astype(F32), sin.astype(F32)

    c64, s64 = table(HEAD_DIM, jnp.ones((LANES,), bool))
    c32, s32 = table(IDX_DIM, jnp.ones((LANES,), bool))
    cm, sm = table(MLA_ROPE, (lane >= MLA_NOPE) & (lane < MLA_NOPE + MLA_ROPE))
    return c64, s64, c32, s32, cm, sm


def _pack_weights(w_in, mla_w_uq, mla_w_ukv, w_out):
    depth, d, _ = w_in.shape
    splits = (MLA_Q_RANK, MLA_KV_RANK, MLA_ROPE,
              GROUP_WIDTH, SWA_KV_HEADS * HEAD_DIM, SWA_KV_HEADS * HEAD_DIM,
              GROUP_WIDTH, GROUP_WIDTH, GROUP_WIDTH, IDX_HEADS * IDX_DIM, IDX_DIM, IDX_HEADS,
              GROUP_WIDTH, GROUP_WIDTH, GROUP_WIDTH, N_HEADS)
    offs = [0]
    for s in splits:
        offs.append(offs[-1] + s)
    (a_cq, a_ckv, a_kr, b_q, b_k, b_v, c_q, c_k, c_v, c_qi, c_ki, c_w,
     d_q, d_k, d_v, d_f) = [w_in[:, :, offs[j]:offs[j + 1]] for j in range(len(splits))]
    zeros = lambda n: jnp.zeros((depth, d, n), w_in.dtype)
    swap = jnp.array([0, 2, 1, 3])
    a_kr = jnp.concatenate([zeros(MLA_NOPE), a_kr, zeros(LANES - MLA_NOPE - MLA_ROPE)], -1)
    b_q = b_q.reshape(depth, d, N_HEADS, HEAD_DIM)[:, :, swap].reshape(depth, d, GROUP_WIDTH)
    c_ki = jnp.concatenate([c_ki] * (LANES // IDX_DIM), -1)
    w_row = jnp.concatenate(
        [a_cq, a_ckv, a_kr, b_q, b_k, c_ki, c_q, c_k, c_qi, d_q, d_k], -1).astype(BF16)
    w_t = jnp.concatenate([c_v, d_v, b_v, c_w, d_f, zeros(MISC_ROWS - IDX_HEADS - N_HEADS)], -1)
    w_t = jnp.swapaxes(w_t, 1, 2).astype(BF16)

    pad = LANES - MLA_NOPE - MLA_ROPE
    wuq = mla_w_uq.reshape(depth, MLA_Q_RANK, N_HEADS, MLA_NOPE + MLA_ROPE)
    wuq = jnp.pad(wuq, ((0, 0), (0, 0), (0, 0), (0, pad))).reshape(depth, MLA_Q_RANK, N_HEADS * LANES)
    wukv = mla_w_ukv.reshape(depth, MLA_KV_RANK, N_HEADS, MLA_NOPE + HEAD_DIM)
    wuk = jnp.pad(wukv[..., :MLA_NOPE], ((0, 0), (0, 0), (0, 0), (0, LANES - MLA_NOPE)))
    wuk = wuk.reshape(depth, MLA_KV_RANK, N_HEADS * LANES)
    wuvt = jnp.swapaxes(wukv[..., MLA_NOPE:].reshape(depth, MLA_KV_RANK, GROUP_WIDTH), 1, 2)

    wo_b = w_out[:, GROUP_WIDTH:2 * GROUP_WIDTH].reshape(depth, N_HEADS, HEAD_DIM, -1)[:, swap]
    wo = jnp.concatenate([w_out[:, :GROUP_WIDTH], wo_b.reshape(depth, GROUP_WIDTH, -1),
                          w_out[:, 2 * GROUP_WIDTH:]], 1)
    return w_row, w_t, wuq.astype(BF16), wuk.astype(BF16), wuvt.astype(BF16), wo.astype(BF16)


_MLA_HEADS = tuple((h * LANES, None, h * LANES) for h in range(N_HEADS))
_FOX_HEADS = tuple(((h // 2) * LANES, h % 2, (h // 2) * LANES) for h in range(N_HEADS))


def kernel(x, norm1, w_in, mla_q_norm, mla_kv_norm, mla_w_uq, mla_w_ukv, swa_sinks, fox_b_f,
           w_out, norm2, w_up, w_down, final_norm):
    batch, seq, d = x.shape
    depth = w_in.shape[0]
    n = batch * seq
    top_k = min(DSA_TOPK, seq // 4)
    tm = min(512, seq)
    tm_in = min(1024, seq)
    tq_flash = min(256, seq)
    ck_flash = min(256, seq)
    tiles_flash = max(1, min(4, seq // tq_flash))
    tq_swa = min(2048, seq)
    tq_dsa = min(256, seq)
    ck_dsa = min(256, seq)

    tables = _rope_tables(seq)
    w_row, w_t, wuq, wuk, wuvt, wo = _pack_weights(w_in, mla_w_uq, mla_w_ukv, w_out)
    wup = w_up.astype(BF16)
    wdn = w_down.astype(BF16)

    x2d = x.reshape(n, d)
    for l in range(depth):
        (qa, ka, qb, kb, qc, kc, qi, ki, qd, kd, va_t, vc_t, vd_t, vb_t, misc_t) = _inproj(
            x2d, norm1[l][None], w_row[l], w_t[l], mla_q_norm[l][None], mla_kv_norm[l][None],
            wuq[l], wuk[l], wuvt[l], tables, batch=batch, seq=seq, tm=tm_in)
        r3 = lambda a: a.reshape(batch, seq, a.shape[-1])
        cum = _forget_cumsum(misc_t[:, IDX_HEADS:IDX_HEADS + N_HEADS], fox_b_f[l],
                             batch=batch, seq=seq)
        cum_col = jnp.swapaxes(cum, 1, 2)
        o_a = _flash(r3(qa), r3(ka), va_t, None, _MLA_HEADS, tq=tq_flash, ck=ck_flash,
                     tiles=tiles_flash)
        o_b = _swa(r3(qb), r3(kb), vb_t, swa_sinks[l], tq=tq_swa)
        o_c = _dsa(r3(qi), misc_t, r3(ki), r3(qc), r3(kc), vc_t, tq=tq_dsa, ck=ck_dsa, topk=top_k)
        o_d = _flash(r3(qd), r3(kd), vd_t, cum_col, _FOX_HEADS, tq=tq_flash, ck=ck_flash,
                     tiles=tiles_flash)
        x2d = _mlp(o_a.reshape(n, -1), o_b.reshape(n, -1), o_c.reshape(n, -1), o_d.reshape(n, -1),
                   x2d, wo[l], norm2[l][None], wup[l], wdn[l], final_norm[None],
                   tm=tm, final=(l == depth - 1))
    return x2d.reshape(batch, seq, d)
```

```python
import functools
import math

import jax
import jax.numpy as jnp
from jax import lax
from jax.experimental import pallas as pl
from jax.experimental.pallas import tpu as pltpu

F32 = jnp.float32
BF16 = jnp.bfloat16
I32 = jnp.int32

HEAD_DIM = 64
N_HEADS = 4
GROUP_WIDTH = N_HEADS * HEAD_DIM
MLA_Q_RANK = 256
MLA_KV_RANK = 128
MLA_NOPE = 64
MLA_ROPE = 32
SWA_KV_HEADS = 2
SWA_WINDOW = 128
IDX_HEADS = 8
IDX_DIM = 32
DSA_TOPK = 256
ROPE_THETA = 10000.0
EPS = 1e-6

LANES = 128
SUBLANES = 8
VMEM_LIMIT = 56 * 1024 * 1024
NEG = -1e30
INT_MIN = -(2 ** 31)
LOG2E = math.log2(math.e)
BISECT_STEPS = 20

_ROW_GROUPS = (
    ("a_cq", 256), ("a_ckv", 128), ("a_kr", 128),
    ("b_q", 256), ("b_k", 128), ("c_ki", 128),
    ("c_q", 256), ("c_k", 256), ("c_qi", 256),
    ("d_q", 256), ("d_k", 256),
)
_ROW_OFF = {}
_acc = 0
for _n, _w in _ROW_GROUPS:
    _ROW_OFF[_n] = (_acc, _acc + _w)
    _acc += _w
ROW_WIDTH = _acc
MISC_ROWS = 16
SWA_KV_WIDTH = SWA_KV_HEADS * HEAD_DIM
T_ROWS = 2 * GROUP_WIDTH + SWA_KV_WIDTH + MISC_ROWS


def _dot(a, b):
    return jnp.dot(a, b, preferred_element_type=F32)


def _dot_nt(a, b):
    return lax.dot_general(a, b, (((1,), (1,)), ((), ())), preferred_element_type=F32)


def _rms(x, g):
    return x * lax.rsqrt(jnp.mean(x * x, axis=-1, keepdims=True) + EPS) * g


def _rope(x, cos, sin, half):
    width = x.shape[1]
    reps = width // LANES
    if reps > 1:
        cos = jnp.concatenate([cos] * reps, axis=1)
        sin = jnp.concatenate([sin] * reps, axis=1)
    lane = lax.broadcasted_iota(I32, x.shape, 1)
    first = (lane % (2 * half)) < half
    rot = jnp.where(first, pltpu.roll(x, width - half, 1), pltpu.roll(x, half, 1))
    return x * cos + rot * sin


def _half_mask(q, half):
    lane = lax.broadcasted_iota(I32, q.shape, 1)
    return jnp.where((lane >= HEAD_DIM) == bool(half), q, jnp.zeros_like(q))


def _softmax_step(s, m, l8, mask):
    ck, tq = s.shape
    if mask is not None:
        s = jnp.where(mask, s, NEG)
    m_new = jnp.maximum(m, jnp.max(s, axis=0, keepdims=True))
    alpha = jnp.exp2(m - m_new)
    p = jnp.exp2(s - m_new)
    l8 = alpha * l8 + jnp.sum(p.reshape(ck // SUBLANES, SUBLANES, tq), axis=0)
    return p.astype(BF16), m_new, l8, alpha


def _inproj_kernel(x_ref, g1_ref, w_ref, wt_ref, qn_ref, kvn_ref, wuq_ref, wuk_ref, wuvt_ref,
                   c64_ref, s64_ref, c32_ref, s32_ref, cm_ref, sm_ref,
                   qa_ref, ka_ref, qb_ref, kb_ref,
                   qc_ref, kc_ref, qi_ref, ki_ref, qd_ref, kd_ref,
                   vat_ref, vct_ref, vdt_ref, vbt_ref, misct_ref):
    h = _rms(x_ref[...], g1_ref[...]).astype(BF16)

    def proj(name):
        lo, hi = _ROW_OFF[name]
        return _dot(h, w_ref[:, lo:hi])

    def proj_pair(name0, name1):
        lo, mid = _ROW_OFF[name0]
        assert _ROW_OFF[name1][0] == mid
        both = _dot(h, w_ref[:, lo:_ROW_OFF[name1][1]])
        return both[:, :mid - lo], both[:, mid - lo:]

    c64, s64 = c64_ref[...], s64_ref[...]
    c32, s32 = c32_ref[...], s32_ref[...]
    cm, sm = cm_ref[...], sm_ref[...]
    scale_a = (MLA_NOPE + MLA_ROPE) ** -0.5 * LOG2E
    scale = HEAD_DIM ** -0.5 * LOG2E

    cq = _rms(proj("a_cq"), qn_ref[...]).astype(BF16)
    qa = _rope(_dot(cq, wuq_ref[...]), cm, sm, MLA_ROPE // 2) * scale_a
    qa_ref[...] = qa.astype(BF16)
    ckv, kr = proj_pair("a_ckv", "a_kr")
    ckv = _rms(ckv, kvn_ref[...]).astype(BF16)
    kr = _rope(kr, cm, sm, MLA_ROPE // 2)
    ka_ref[...] = (_dot(ckv, wuk_ref[...]) + jnp.concatenate([kr] * N_HEADS, axis=1)).astype(BF16)
    vat_ref[0] = _dot_nt(wuvt_ref[...], ckv).astype(BF16)

    qb_ref[...] = (_rope(proj("b_q"), c64, s64, HEAD_DIM // 2) * scale).astype(BF16)
    kb, ki = proj_pair("b_k", "c_ki")
    kb_ref[...] = _rope(kb, c64, s64, HEAD_DIM // 2).astype(BF16)

    qc_ref[...] = (_rope(proj("c_q"), c64, s64, HEAD_DIM // 2) * scale).astype(BF16)
    kc_ref[...] = _rope(proj("c_k"), c64, s64, HEAD_DIM // 2).astype(BF16)
    qi_ref[...] = _rope(proj("c_qi"), c32, s32, IDX_DIM // 2).astype(BF16)
    ki_ref[...] = _rope(ki, c32, s32, IDX_DIM // 2).astype(BF16)

    qd_ref[...] = (proj("d_q") * scale).astype(BF16)
    kd_ref[...] = proj("d_k").astype(BF16)

    t = _dot_nt(wt_ref[...], h)
    vct_ref[0] = t[:GROUP_WIDTH].astype(BF16)
    vdt_ref[0] = t[GROUP_WIDTH:2 * GROUP_WIDTH].astype(BF16)
    vbt_ref[0] = t[2 * GROUP_WIDTH:2 * GROUP_WIDTH + SWA_KV_WIDTH].astype(BF16)
    misct_ref[0] = t[2 * GROUP_WIDTH + SWA_KV_WIDTH:]


def _inproj(x2d, g1, w_row, w_t, qn, kvn, wuq, wuk, wuvt, tables, *, batch, seq, tm):
    n = x2d.shape[0]
    d = x2d.shape[1]
    spb = seq // tm
    row = lambda i: (i, 0)
    const = lambda i: (0, 0)
    tab = lambda i: (i % spb, 0)
    tr = lambda i: (i // spb, 0, i % spb)
    once = pl.Buffered(1)

    def out2(width):
        return jax.ShapeDtypeStruct((n, width), BF16), pl.BlockSpec((tm, width), row)

    def out_t(rows, dtype):
        return jax.ShapeDtypeStruct((batch, rows, seq), dtype), pl.BlockSpec((1, rows, tm), tr)

    outs = [out2(512), out2(512),
            out2(256), out2(128),
            out2(256), out2(256), out2(256), out2(128),
            out2(256), out2(256),
            out_t(GROUP_WIDTH, BF16), out_t(GROUP_WIDTH, BF16), out_t(GROUP_WIDTH, BF16),
            out_t(SWA_KV_WIDTH, BF16), out_t(MISC_ROWS, F32)]
    in_specs = [
        pl.BlockSpec((tm, d), row),
        pl.BlockSpec((1, d), const),
        pl.BlockSpec((d, ROW_WIDTH), const, pipeline_mode=once),
        pl.BlockSpec((T_ROWS, d), const, pipeline_mode=once),
        pl.BlockSpec((1, MLA_Q_RANK), const),
        pl.BlockSpec((1, MLA_KV_RANK), const),
        pl.BlockSpec((MLA_Q_RANK, N_HEADS * LANES), const, pipeline_mode=once),
        pl.BlockSpec((MLA_KV_RANK, N_HEADS * LANES), const, pipeline_mode=once),
        pl.BlockSpec((GROUP_WIDTH, MLA_KV_RANK), const, pipeline_mode=once),
    ] + [pl.BlockSpec((tm, LANES), tab)] * 6
    return pl.pallas_call(
        _inproj_kernel,
        grid=(n // tm,),
        in_specs=in_specs,
        out_specs=[o[1] for o in outs],
        out_shape=[o[0] for o in outs],
        compiler_params=pltpu.CompilerParams(
            dimension_semantics=("parallel",), vmem_limit_bytes=VMEM_LIMIT),
        name="inproj",
    )(x2d, g1, w_row, w_t, qn, kvn, wuq, wuk, wuvt, *tables)


def _cumsum_kernel(f_ref, b_ref, o_ref, *, segs):
    x = f_ref[...] + b_ref[...]
    ls = jnp.minimum(x, 0.0) - jnp.log1p(jnp.exp(-jnp.abs(x)))
    rows = x.shape[0]
    r = lax.broadcasted_iota(I32, (LANES, LANES), 0)
    c = lax.broadcasted_iota(I32, (LANES, LANES), 1)
    upper = (r <= c).astype(F32)
    within = jnp.dot(ls, upper, preferred_element_type=F32, precision=lax.Precision.HIGHEST)
    tot = jnp.broadcast_to(within[:, LANES - 1:LANES], (rows, LANES))
    rr = lax.broadcasted_iota(I32, (rows, rows), 0)
    cc = lax.broadcasted_iota(I32, (rows, rows), 1)
    before = ((cc < rr) & (cc // segs == rr // segs)).astype(F32)
    off = jnp.dot(before, tot, preferred_element_type=F32, precision=lax.Precision.HIGHEST)
    o_ref[...] = (within + off) * LOG2E


def _forget_cumsum(f_t, bias, *, batch, seq):
    segs = seq // LANES
    rows = N_HEADS * segs
    f2 = f_t.reshape(batch * rows, LANES)
    b2 = jnp.broadcast_to(jnp.repeat(bias, segs)[:, None], (rows, LANES))
    out = pl.pallas_call(
        functools.partial(_cumsum_kernel, segs=segs),
        grid=(batch,),
        in_specs=[pl.BlockSpec((rows, LANES), lambda b: (b, 0)),
                  pl.BlockSpec((rows, LANES), lambda b: (0, 0))],
        out_specs=pl.BlockSpec((rows, LANES), lambda b: (b, 0)),
        out_shape=jax.ShapeDtypeStruct((batch * rows, LANES), F32),
        compiler_params=pltpu.CompilerParams(dimension_semantics=("parallel",)),
        name="forget_cumsum",
    )(f2, b2)
    return out.reshape(batch, N_HEADS, seq)


def _flash_kernel(*refs, heads, tq, ck, has_bias, tiles):
    if has_bias:
        q_ref, k_ref, vt_ref, b_ref, o_ref, acc_ref, sa_ref, sb_ref = refs
    else:
        q_ref, k_ref, vt_ref, o_ref, acc_ref, sa_ref, sb_ref = refs
        b_ref = None
    assert tq == ck
    nh = len(heads)

    def one_tile(t, unused):
        i = pl.program_id(1) * tiles + t
        rows = pl.ds(pl.multiple_of(t * tq, tq), tq)
        q_start = i * tq
        n_full = q_start // ck
        kpos_l = lax.broadcasted_iota(I32, (ck, tq), 0)
        qpos = q_start + lax.broadcasted_iota(I32, (ck, tq), 1)
        qs = []
        for ql, qhalf, _ in heads:
            q = q_ref[0, rows, ql:ql + LANES]
            qs.append(q if qhalf is None else _half_mask(q, qhalf))
        acc_ref[...] = jnp.zeros(acc_ref.shape, F32)

        def scores(c, s_ref):
            off = pl.multiple_of(c * ck, ck)
            for h, (_, _, kl) in enumerate(heads):
                s = _dot_nt(k_ref[0, pl.ds(off, ck), kl:kl + LANES], qs[h])
                if has_bias:
                    s = s - b_ref[0, pl.ds(off, ck), h:h + 1]
                s_ref[h] = s

        def consume(c, s_ref, carry, masked):
            ms, ls = carry
            off = pl.multiple_of(c * ck, ck)
            mask = (off + kpos_l <= qpos) if masked else None
            new_ms, new_ls = [], []
            for h in range(nh):
                p, m_new, l8, alpha = _softmax_step(s_ref[h], ms[h], ls[h], mask)
                vt = vt_ref[0, h * HEAD_DIM:(h + 1) * HEAD_DIM, pl.ds(off, ck)]
                acc_ref[h] = alpha * acc_ref[h] + _dot(vt, p)
                new_ms.append(m_new)
                new_ls.append(l8)
            return tuple(new_ms), tuple(new_ls)

        def pair_body(p, carry):
            c = 2 * p
            scores(c + 1, sb_ref)
            carry = consume(c, sa_ref, carry, False)
            scores(c + 2, sa_ref)
            return consume(c + 1, sb_ref, carry, False)

        carry = (tuple(jnp.full((1, tq), NEG, F32) for _ in range(nh)),
                 tuple(jnp.zeros((SUBLANES, tq), F32) for _ in range(nh)))
        scores(0, sa_ref)
        carry = lax.fori_loop(0, n_full // 2, pair_body, carry)
        odd = n_full % 2 == 1

        def tail_odd(carry):
            scores(n_full, sb_ref)
            carry = consume(n_full - 1, sa_ref, carry, False)
            return consume(n_full, sb_ref, carry, True)

        carry = lax.cond(odd, tail_odd, lambda cr: consume(n_full, sa_ref, cr, True), carry)
        _, ls = carry
        for pair in range(nh // 2):
            o0 = acc_ref[2 * pair] / jnp.sum(ls[2 * pair], axis=0, keepdims=True)
            o1 = acc_ref[2 * pair + 1] / jnp.sum(ls[2 * pair + 1], axis=0, keepdims=True)
            o_t = jnp.concatenate([o0, o1], axis=0)
            o_ref[0, rows, pair * LANES:(pair + 1) * LANES] = o_t.T.astype(o_ref.dtype)
        return unused

    lax.fori_loop(0, tiles, one_tile, 0)


def _flash(q, k, v_t, bias, heads, *, tq, ck, tiles):
    batch, seq, wq = q.shape
    wk = k.shape[2]
    has_bias = bias is not None
    in_specs = [pl.BlockSpec((1, tiles * tq, wq), lambda b, i: (b, i, 0)),
                pl.BlockSpec((1, seq, wk), lambda b, i: (b, 0, 0)),
                pl.BlockSpec((1, GROUP_WIDTH, seq), lambda b, i: (b, 0, 0))]
    args = [q, k, v_t]
    if has_bias:
        in_specs.append(pl.BlockSpec((1, seq, N_HEADS), lambda b, i: (b, 0, 0)))
        args.append(bias)
    return pl.pallas_call(
        functools.partial(_flash_kernel, heads=heads, tq=tq, ck=ck, has_bias=has_bias, tiles=tiles),
        grid=(batch, seq // (tiles * tq)),
        in_specs=in_specs,
        out_specs=pl.BlockSpec((1, tiles * tq, GROUP_WIDTH), lambda b, i: (b, i, 0)),
        out_shape=jax.ShapeDtypeStruct((batch, seq, GROUP_WIDTH), BF16),
        scratch_shapes=[pltpu.VMEM((len(heads), HEAD_DIM, tq), F32),
                        pltpu.VMEM((len(heads), ck, tq), F32),
                        pltpu.VMEM((len(heads), ck, tq), F32)],
        compiler_params=pltpu.CompilerParams(
            dimension_semantics=("parallel", "arbitrary"), vmem_limit_bytes=VMEM_LIMIT),
        name="flash_bias" if has_bias else "flash",
    )(*args)


def _swa_kernel(sink_ref, q_ref, k_ref, vt_ref, o_ref, sa_ref, sb_ref, *, tq, window):
    i = pl.program_id(1)
    span = 2 * window
    n_sub = tq // window
    assert n_sub % 2 == 0

    def key_start(j):
        return pl.multiple_of(jnp.maximum(i * tq + j * window - window, 0), window)

    def scores(j, s_ref):
        j = jnp.minimum(j, n_sub - 1)
        k = k_ref[0, pl.ds(key_start(j), span), :]
        rows = pl.ds(pl.multiple_of(j * window, window), window)
        for pair in range(2):
            qp = q_ref[0, rows, pair * LANES:(pair + 1) * LANES]
            for half in range(2):
                s_ref[2 * pair + half] = _dot_nt(k, _half_mask(qp, half))

    def consume(j, s_ref):
        start = key_start(j)
        kpos = start + lax.broadcasted_iota(I32, (span, window), 0)
        qpos = i * tq + j * window + lax.broadcasted_iota(I32, (span, window), 1)
        valid = (kpos <= qpos) & (qpos - kpos < window)
        rows = pl.ds(pl.multiple_of(j * window, window), window)
        for pair in range(2):
            outs = []
            for half in range(2):
                head = pair + 2 * half
                sink = sink_ref[head] * LOG2E
                s = jnp.where(valid, s_ref[2 * pair + half], NEG)
                m = jnp.maximum(jnp.max(s, axis=0, keepdims=True), sink)
                p = jnp.exp2(s - m)
                denom = jnp.sum(p, axis=0, keepdims=True) + jnp.exp2(sink - m)
                vt = vt_ref[0, half * HEAD_DIM:(half + 1) * HEAD_DIM, pl.ds(start, span)]
                outs.append(_dot(vt, p.astype(BF16)) / denom)
            o_t = jnp.concatenate(outs, axis=0)
            o_ref[0, rows, pair * LANES:(pair + 1) * LANES] = o_t.T.astype(o_ref.dtype)

    def pair_body(p, carry):
        j = 2 * p
        scores(j + 1, sb_ref)
        consume(j, sa_ref)
        scores(j + 2, sa_ref)
        consume(j + 1, sb_ref)
        return carry

    scores(0, sa_ref)
    lax.fori_loop(0, n_sub // 2, pair_body, 0)


def _swa(q, k, v_t, sinks, *, tq):
    batch, seq, _ = q.shape
    return pl.pallas_call(
        functools.partial(_swa_kernel, tq=tq, window=SWA_WINDOW),
        grid=(batch, seq // tq),
        in_specs=[pl.BlockSpec(memory_space=pltpu.SMEM),
                  pl.BlockSpec((1, tq, GROUP_WIDTH), lambda b, i: (b, i, 0)),
                  pl.BlockSpec((1, seq, LANES), lambda b, i: (b, 0, 0)),
                  pl.BlockSpec((1, SWA_KV_HEADS * HEAD_DIM, seq), lambda b, i: (b, 0, 0))],
        out_specs=pl.BlockSpec((1, tq, GROUP_WIDTH), lambda b, i: (b, i, 0)),
        out_shape=jax.ShapeDtypeStruct((batch, seq, GROUP_WIDTH), BF16),
        scratch_shapes=[pltpu.VMEM((N_HEADS, 2 * SWA_WINDOW, SWA_WINDOW), F32),
                        pltpu.VMEM((N_HEADS, 2 * SWA_WINDOW, SWA_WINDOW), F32)],
        compiler_params=pltpu.CompilerParams(
            dimension_semantics=("parallel", "arbitrary"), vmem_limit_bytes=VMEM_LIMIT),
        name="swa",
    )(sinks, q, k, v_t)


def _dsa_kernel(qi_ref, w_ref, ki_ref, qc_ref, kc_ref, vct_ref, o_ref,
                keys_ref, acc_ref, sa_ref, sb_ref, *, tq, ck, topk, seq):
    assert tq == ck
    i = pl.program_id(1)
    q_start = i * tq
    n_chunks = (q_start + tq) // ck
    n_full = q_start // ck
    lane = lax.broadcasted_iota(I32, (tq, LANES), 1)
    kpos_l = lax.broadcasted_iota(I32, (ck, tq), 0)
    qpos = q_start + lax.broadcasted_iota(I32, (ck, tq), 1)
    qpos_row = q_start + lax.broadcasted_iota(I32, (1, tq), 1)
    msb = jnp.int32(INT_MIN)

    qm = []
    for h in range(IDX_HEADS):
        g, r = divmod(h, LANES // IDX_DIM)
        qg = qi_ref[0, :, g * LANES:(g + 1) * LANES]
        qm.append(jnp.where(lane // IDX_DIM == r, qg, jnp.zeros_like(qg)))
    w = w_ref[0]

    def flip(v):
        return v ^ ((v >> 31) & jnp.int32(0x7FFFFFFF))

    def group_reduce(fn, v):
        quarter = v.shape[0] // 4
        parts = [fn(v[j * quarter:(j + 1) * quarter].reshape(quarter // SUBLANES, SUBLANES, tq), axis=0)
                 for j in range(4)]
        return fn(jnp.stack([fn(jnp.stack(parts[:2]), axis=0), fn(jnp.stack(parts[2:]), axis=0)]),
                  axis=0)

    half_heads = IDX_HEADS // 2

    def index_dots(c, first, s_ref):
        ki = ki_ref[0, pl.ds(pl.multiple_of(c * ck, ck), ck), :]
        for j in range(half_heads):
            s_ref[j] = _dot_nt(ki, qm[first + j])

    def weighted_relu(first, s_ref):
        acc = None
        for j in range(half_heads):
            term = jnp.maximum(s_ref[j], 0.0) * w[first + j:first + j + 1, :]
            acc = term if acc is None else acc + term
        return acc

    def score_chunk(c, carry, last):
        kmax8, kmin8 = carry
        off = pl.multiple_of(c * ck, ck)
        index_dots(c, half_heads, sb_ref)
        acc = weighted_relu(0, sa_ref)
        if not last:
            index_dots(c + 1, 0, sa_ref)
        acc = acc + weighted_relu(half_heads, sb_ref)
        key = flip(lax.bitcast_convert_type(acc, I32))
        key_lo = key
        if last:
            causal = off + kpos_l <= qpos
            key_lo = jnp.where(causal, key, jnp.int32(2 ** 31 - 1))
            key = jnp.where(causal, key, msb)
        keys_ref[pl.ds(off, ck), :] = key
        return (jnp.maximum(kmax8, group_reduce(jnp.max, key)),
                jnp.minimum(kmin8, group_reduce(jnp.min, key_lo)))

    ext = (jnp.full((SUBLANES, tq), INT_MIN, I32), jnp.full((SUBLANES, tq), 2 ** 31 - 1, I32))
    index_dots(0, 0, sa_ref)
    ext = lax.fori_loop(0, n_full, lambda c, cr: score_chunk(c, cr, False), ext)
    ext = score_chunk(n_full, ext, True)
    kmax = jnp.max(ext[0], axis=0, keepdims=True)
    kmin = jnp.min(ext[1], axis=0, keepdims=True)

    qs = []
    for h in range(N_HEADS):
        pair, half = divmod(h, 2)
        qs.append(_half_mask(qc_ref[0, :, pair * LANES:(pair + 1) * LANES], half))

    def scores(c, s_ref):
        off = pl.multiple_of(c * ck, ck)
        for h in range(N_HEADS):
            k = kc_ref[0, pl.ds(off, ck), (h // 2) * LANES:(h // 2 + 1) * LANES]
            s_ref[h] = _dot_nt(k, qs[h])

    scores(0, sa_ref)

    cb = 2 * ck
    n_blocks = (n_chunks + 1) // 2

    @pl.when(n_chunks % 2 == 1)
    def _():
        keys_ref[pl.ds(pl.multiple_of(n_chunks * ck, ck), ck), :] = jnp.full((ck, tq), INT_MIN, I32)

    def key_blocks(fn, init):
        def body(b, carry):
            off = pl.multiple_of(b * cb, cb)
            return fn(keys_ref[pl.ds(off, cb), :], off, carry)
        return lax.fori_loop(0, n_blocks, body, init)

    def count(pred):
        part = key_blocks(
            lambda kk, off, part: part + group_reduce(jnp.sum, jnp.where(pred(kk, off), 1, 0)),
            jnp.zeros((SUBLANES, tq), I32))
        return jnp.sum(part, axis=0, keepdims=True)

    need = qpos_row >= topk

    def settle(state, cand_k, c, lo_k, hi_k):
        done, tie, t_res = state
        hit = c == topk
        fin = (done == 0) & (hit | (hi_k - lo_k == 1))
        t_res = jnp.where(fin, jnp.where(hit, cand_k, lo_k), t_res)
        tie = jnp.where(fin & ~hit, 1, tie)
        return jnp.where(fin, 1, done), tie, t_res

    def bisect_step(state):
        it, flags, lo_k, hi_k, c_hi = state
        cand_k = lo_k + lax.shift_right_logical(hi_k - lo_k, 1)
        c = count(lambda kk, off: kk >= cand_k)
        ge = c >= topk
        lo_k = jnp.where(ge, cand_k, lo_k)
        hi_k = jnp.where(ge, hi_k, cand_k)
        return it + 1, settle(flags, cand_k, c, lo_k, hi_k), lo_k, hi_k, jnp.where(ge, c_hi, c)

    def snap_step(state):
        it, flags, lo_k, hi_k, c_hi = state

        top8 = key_blocks(
            lambda kk, off, part: jnp.maximum(
                part, group_reduce(jnp.max, jnp.where(kk < hi_k, kk, msb))),
            jnp.full((SUBLANES, tq), INT_MIN, I32))
        cand_k = jnp.maximum(jnp.max(top8, axis=0, keepdims=True), lo_k)
        c = count(lambda kk, off: kk >= cand_k)
        ge = c >= topk
        lo_k = jnp.where(ge, cand_k, lo_k)
        hi_k = jnp.where(ge, cand_k + 1, cand_k)
        return it + 1, settle(flags, cand_k, c, lo_k, hi_k), lo_k, hi_k, jnp.where(ge, c_hi, c)

    def unsettled(state):
        return jnp.min(state[1][0]) == 0

    lo0, hi0 = kmin, kmax + 1
    flat0 = need & (hi0 - lo0 == 1)
    flags = (jnp.where(need & ~flat0, 0, 1), jnp.where(flat0, 1, 0), jnp.where(need, lo0, msb + 1))
    state = (jnp.int32(0), flags, lo0, hi0, jnp.zeros((1, tq), I32))
    state = lax.fori_loop(0, BISECT_STEPS, lambda _, st: bisect_step(st), state)
    state = snap_step(state)
    state = lax.while_loop(unsettled, lambda st: snap_step(bisect_step(st)), state)
    _, (_, tie_flag, t_s), _, _, n_gt = state
    tie = tie_flag > 0

    @pl.when(jnp.max(jnp.where(tie, 1, 0)) > 0)
    def _():
        want = (topk - n_gt).astype(F32)
        tri = (lax.broadcasted_iota(I32, (ck, ck), 0) > lax.broadcasted_iota(I32, (ck, ck), 1))
        tri = jnp.where(tri, 1.0, 0.0).astype(BF16)

        def demote(c, before):
            off = pl.multiple_of(c * ck, ck)
            kk = keys_ref[pl.ds(off, ck), :]
            tied = tie & (kk == t_s)
            ind = jnp.where(tied, 1.0, 0.0)
            rank = before + _dot(tri, ind.astype(BF16))
            keys_ref[pl.ds(off, ck), :] = jnp.where(tied & (rank >= want), msb, kk)
            return before + jnp.sum(ind, axis=0, keepdims=True)

        lax.fori_loop(0, n_chunks, demote, jnp.zeros((1, tq), F32))

    t_fin = jnp.maximum(t_s, msb + 1)
    acc_ref[...] = jnp.zeros(acc_ref.shape, F32)

    def consume(c, s_ref, carry):
        ms, ls = carry
        off = pl.multiple_of(c * ck, ck)
        sel = keys_ref[pl.ds(off, ck), :] >= t_fin
        new_ms, new_ls = [], []
        for h in range(N_HEADS):
            pair = h // 2
            p, m_new, l8, alpha = _softmax_step(s_ref[h], ms[h], ls[h], sel)
            vt = vct_ref[0, h * HEAD_DIM:(h + 1) * HEAD_DIM, pl.ds(off, ck)]
            acc_ref[h] = alpha * acc_ref[h] + _dot(vt, p)
            new_ms.append(m_new)
            new_ls.append(l8)
        return tuple(new_ms), tuple(new_ls)

    def pair_body(p, carry):
        c = 2 * p
        scores(c + 1, sb_ref)
        carry = consume(c, sa_ref, carry)
        scores(c + 2, sa_ref)
        return consume(c + 1, sb_ref, carry)

    carry = (tuple(jnp.full((1, tq), NEG, F32) for _ in range(N_HEADS)),
             tuple(jnp.zeros((SUBLANES, tq), F32) for _ in range(N_HEADS)))
    carry = lax.fori_loop(0, n_full // 2, pair_body, carry)

    def tail_odd(carry):
        scores(n_full, sb_ref)
        return consume(n_full, sb_ref, consume(n_full - 1, sa_ref, carry))

    _, ls = lax.cond(n_full % 2 == 1, tail_odd, lambda cr: consume(n_full, sa_ref, cr), carry)
    for pair in range(2):
        o0 = acc_ref[2 * pair] / jnp.sum(ls[2 * pair], axis=0, keepdims=True)
        o1 = acc_ref[2 * pair + 1] / jnp.sum(ls[2 * pair + 1], axis=0, keepdims=True)
        o_t = jnp.concatenate([o0, o1], axis=0)
        o_ref[0, :, pair * LANES:(pair + 1) * LANES] = o_t.T.astype(o_ref.dtype)


def _dsa(qi, misc_t, ki, qc, kc, vc_t, *, tq, ck, topk):
    batch, seq, _ = qc.shape
    return pl.pallas_call(
        functools.partial(_dsa_kernel, tq=tq, ck=ck, topk=topk, seq=seq),
        grid=(batch, seq // tq),
        in_specs=[pl.BlockSpec((1, tq, IDX_HEADS * IDX_DIM), lambda b, i: (b, i, 0)),
                  pl.BlockSpec((1, MISC_ROWS, tq), lambda b, i: (b, 0, i)),
                  pl.BlockSpec((1, seq, LANES), lambda b, i: (b, 0, 0)),
                  pl.BlockSpec((1, tq, GROUP_WIDTH), lambda b, i: (b, i, 0)),
                  pl.BlockSpec((1, seq, GROUP_WIDTH), lambda b, i: (b, 0, 0)),
                  pl.BlockSpec((1, GROUP_WIDTH, seq), lambda b, i: (b, 0, 0))],
        out_specs=pl.BlockSpec((1, tq, GROUP_WIDTH), lambda b, i: (b, i, 0)),
        out_shape=jax.ShapeDtypeStruct((batch, seq, GROUP_WIDTH), BF16),
        scratch_shapes=[pltpu.VMEM((seq, tq), I32),
                        pltpu.VMEM((N_HEADS, HEAD_DIM, tq), F32),
                        pltpu.VMEM((N_HEADS, ck, tq), F32),
                        pltpu.VMEM((N_HEADS, ck, tq), F32)],
        compiler_params=pltpu.CompilerParams(
            dimension_semantics=("parallel", "arbitrary"), vmem_limit_bytes=VMEM_LIMIT),
        name="dsa",
    )(qi, misc_t, ki, qc, kc, vc_t)


def _mlp_kernel(oa_ref, ob_ref, oc_ref, od_ref, x_ref, wo_ref, g2_ref, wup_ref, wdn_ref,
                gf_ref, out_ref, *, ff_chunk, final):
    gw = GROUP_WIDTH
    mix = None
    for g, o_ref in enumerate((oa_ref, ob_ref, oc_ref, od_ref)):
        y = _dot(o_ref[...], wo_ref[g * gw:(g + 1) * gw, :])
        mix = y if mix is None else mix + y
    x1 = x_ref[...] + mix
    h2 = _rms(x1, g2_ref[...]).astype(BF16)
    ffn = None
    for c in range(wup_ref.shape[1] // ff_chunk):
        u = _dot(h2, wup_ref[:, c * ff_chunk:(c + 1) * ff_chunk])
        a = jnp.square(jnp.maximum(u, 0.0)).astype(BF16)
        y = _dot(a, wdn_ref[c * ff_chunk:(c + 1) * ff_chunk, :])
        ffn = y if ffn is None else ffn + y
    acc = x1 + ffn
    if final:
        acc = _rms(acc, gf_ref[...])
    out_ref[...] = acc


def _mlp(oa, ob, oc, od, x2d, wo, g2, wup, wdn, gf, *, tm, final):
    n, d = x2d.shape
    dff = wup.shape[1]
    row = lambda i: (i, 0)
    const = lambda i: (0, 0)
    once = pl.Buffered(1)
    o_spec = pl.BlockSpec((tm, GROUP_WIDTH), row)
    return pl.pallas_call(
        functools.partial(_mlp_kernel, ff_chunk=1024, final=final),
        grid=(n // tm,),
        in_specs=[o_spec, o_spec, o_spec, o_spec,
                  pl.BlockSpec((tm, d), row),
                  pl.BlockSpec((d, d), const, pipeline_mode=once),
                  pl.BlockSpec((1, d), const),
                  pl.BlockSpec((d, dff), const, pipeline_mode=once),
                  pl.BlockSpec((dff, d), const, pipeline_mode=once),
                  pl.BlockSpec((1, d), const)],
        out_specs=pl.BlockSpec((tm, d), row),
        out_shape=jax.ShapeDtypeStruct((n, d), F32),
        compiler_params=pltpu.CompilerParams(
            dimension_semantics=("parallel",), vmem_limit_bytes=VMEM_LIMIT),
        name="outproj_mlp",
    )(oa, ob, oc, od, x2d, wo, g2, wup, wdn, gf)


def _rope_tables(seq):
    pos = jnp.arange(seq, dtype=F32)[:, None]
    lane = jnp.arange(LANES)

    def table(dim, active):
        half = dim // 2
        inv_freq = 1.0 / (ROPE_THETA ** (jnp.arange(0, half, dtype=F32) * 2.0 / dim))
        ang = pos * inv_freq[None, :][:, lane % half]
        sign = jnp.where((lane % dim) < half, -1.0, 1.0)
        cos = jnp.where(active, jnp.cos(ang), 1.0)
        sin = jnp.where(active, jnp.sin(ang) * sign, 0.0)
        return cos.astype(F32), sin.astype(F32)

    c64, s64 = table(HEAD_DIM, jnp.ones((LANES,), bool))
    c32, s32 = table(IDX_DIM, jnp.ones((LANES,), bool))
    cm, sm = table(MLA_ROPE, (lane >= MLA_NOPE) & (lane < MLA_NOPE + MLA_ROPE))
    return c64, s64, c32, s32, cm, sm


def _pack_weights(w_in, mla_w_uq, mla_w_ukv, w_out):
    depth, d, _ = w_in.shape
    splits = (MLA_Q_RANK, MLA_KV_RANK, MLA_ROPE,
              GROUP_WIDTH, SWA_KV_HEADS * HEAD_DIM, SWA_KV_HEADS * HEAD_DIM,
              GROUP_WIDTH, GROUP_WIDTH, GROUP_WIDTH, IDX_HEADS * IDX_DIM, IDX_DIM, IDX_HEADS,
              GROUP_WIDTH, GROUP_WIDTH, GROUP_WIDTH, N_HEADS)
    offs = [0]
    for s in splits:
        offs.append(offs[-1] + s)
    (a_cq, a_ckv, a_kr, b_q, b_k, b_v, c_q, c_k, c_v, c_qi, c_ki, c_w,
     d_q, d_k, d_v, d_f) = [w_in[:, :, offs[j]:offs[j + 1]] for j in range(len(splits))]
    zeros = lambda n: jnp.zeros((depth, d, n), w_in.dtype)
    swap = jnp.array([0, 2, 1, 3])
    a_kr = jnp.concatenate([zeros(MLA_NOPE), a_kr, zeros(LANES - MLA_NOPE - MLA_ROPE)], -1)
    b_q = b_q.reshape(depth, d, N_HEADS, HEAD_DIM)[:, :, swap].reshape(depth, d, GROUP_WIDTH)
    c_ki = jnp.concatenate([c_ki] * (LANES // IDX_DIM), -1)
    w_row = jnp.concatenate(
        [a_cq, a_ckv, a_kr, b_q, b_k, c_ki, c_q, c_k, c_qi, d_q, d_k], -1).astype(BF16)
    w_t = jnp.concatenate([c_v, d_v, b_v, c_w, d_f, zeros(MISC_ROWS - IDX_HEADS - N_HEADS)], -1)
    w_t = jnp.swapaxes(w_t, 1, 2).astype(BF16)

    pad = LANES - MLA_NOPE - MLA_ROPE
    wuq = mla_w_uq.reshape(depth, MLA_Q_RANK, N_HEADS, MLA_NOPE + MLA_ROPE)
    wuq = jnp.pad(wuq, ((0, 0), (0, 0), (0, 0), (0, pad))).reshape(depth, MLA_Q_RANK, N_HEADS * LANES)
    wukv = mla_w_ukv.reshape(depth, MLA_KV_RANK, N_HEADS, MLA_NOPE + HEAD_DIM)
    wuk = jnp.pad(wukv[..., :MLA_NOPE], ((0, 0), (0, 0), (0, 0), (0, LANES - MLA_NOPE)))
    wuk = wuk.reshape(depth, MLA_KV_RANK, N_HEADS * LANES)
    wuvt = jnp.swapaxes(wukv[..., MLA_NOPE:].reshape(depth, MLA_KV_RANK, GROUP_WIDTH), 1, 2)

    wo_b = w_out[:, GROUP_WIDTH:2 * GROUP_WIDTH].reshape(depth, N_HEADS, HEAD_DIM, -1)[:, swap]
    wo = jnp.concatenate([w_out[:, :GROUP_WIDTH], wo_b.reshape(depth, GROUP_WIDTH, -1),
                          w_out[:, 2 * GROUP_WIDTH:]], 1)
    return w_row, w_t, wuq.astype(BF16), wuk.astype(BF16), wuvt.astype(BF16), wo.astype(BF16)


_MLA_HEADS = tuple((h * LANES, None, h * LANES) for h in range(N_HEADS))
_FOX_HEADS = tuple(((h // 2) * LANES, h % 2, (h // 2) * LANES) for h in range(N_HEADS))


def kernel(x, norm1, w_in, mla_q_norm, mla_kv_norm, mla_w_uq, mla_w_ukv, swa_sinks, fox_b_f,
           w_out, norm2, w_up, w_down, final_norm):
    batch, seq, d = x.shape
    depth = w_in.shape[0]
    n = batch * seq
    top_k = min(DSA_TOPK, seq // 4)
    tm = min(512, seq)
    tm_in = min(1024, seq)
    tq_flash = min(256, seq)
    ck_flash = min(256, seq)
    tiles_flash = max(1, min(4, seq // tq_flash))
    tq_swa = min(2048, seq)
    tq_dsa = min(256, seq)
    ck_dsa = min(256, seq)

    tables = _rope_tables(seq)
    w_row, w_t, wuq, wuk, wuvt, wo = _pack_weights(w_in, mla_w_uq, mla_w_ukv, w_out)
    wup = w_up.astype(BF16)
    wdn = w_down.astype(BF16)

    x2d = x.reshape(n, d)
    for l in range(depth):
        (qa, ka, qb, kb, qc, kc, qi, ki, qd, kd, va_t, vc_t, vd_t, vb_t, misc_t) = _inproj(
            x2d, norm1[l][None], w_row[l], w_t[l], mla_q_norm[l][None], mla_kv_norm[l][None],
            wuq[l], wuk[l], wuvt[l], tables, batch=batch, seq=seq, tm=tm_in)
        r3 = lambda a: a.reshape(batch, seq, a.shape[-1])
        cum = _forget_cumsum(misc_t[:, IDX_HEADS:IDX_HEADS + N_HEADS], fox_b_f[l],
                             batch=batch, seq=seq)
        cum_col = jnp.swapaxes(cum, 1, 2)
        o_a = _flash(r3(qa), r3(ka), va_t, None, _MLA_HEADS, tq=tq_flash, ck=ck_flash,
                     tiles=tiles_flash)
        o_b = _swa(r3(qb), r3(kb), vb_t, swa_sinks[l], tq=tq_swa)
        o_c = _dsa(r3(qi), misc_t, r3(ki), r3(qc), r3(kc), vc_t, tq=tq_dsa, ck=ck_dsa, topk=top_k)
        o_d = _flash(r3(qd), r3(kd), vd_t, cum_col, _FOX_HEADS, tq=tq_flash, ck=ck_flash,
                     tiles=tiles_flash)
        x2d = _mlp(o_a.reshape(n, -1), o_b.reshape(n, -1), o_c.reshape(n, -1), o_d.reshape(n, -1),
                   x2d, wo[l], norm2[l][None], wup[l], wdn[l], final_norm[None],
                   tm=tm, final=(l == depth - 1))
    return x2d.reshape(batch, seq, d)
```
